```python
import math
import jax, jax.numpy as jnp
from jax import lax
import numpy as np

D_MODEL = 1024
BATCH = 2
SEQ = 8192
DEPTH = 1

SB_HEADS = 8
SB_HEAD_DIM = 64
SB_WIDTH = SB_HEADS * SB_HEAD_DIM
MLA_HEADS = 8
MLA_NOPE_DIM = 64
MLA_ROPE_DIM = 32
MLA_QK_DIM = MLA_NOPE_DIM + MLA_ROPE_DIM
MLA_V_DIM = 64
MLA_WIDTH = MLA_HEADS * MLA_V_DIM
Q_LORA_RANK = 384
KV_LORA_RANK = 256
ROPE_THETA = 10000.0
MIX_WIDTH = SB_WIDTH + MLA_WIDTH
Q_BLOCK = 128
EPS = 1e-6
IN_SIZES = (SB_WIDTH, SB_WIDTH, SB_WIDTH, SB_WIDTH,
            Q_LORA_RANK, KV_LORA_RANK, MLA_ROPE_DIM, MLA_WIDTH)
IN_COLS = sum(IN_SIZES)

kernel_name = "hymba_stickbreaking_mla_adaln"


def _rms_norm(x, w):
    xf = x.astype(jnp.float32)
    y = xf * lax.rsqrt(jnp.mean(xf * xf, axis=-1, keepdims=True) + EPS)
    return (y * w.astype(jnp.float32)).astype(x.dtype)


def _rotate_half(x):
    x1, x2 = jnp.split(x, 2, axis=-1)
    return jnp.concatenate([-x2, x1], axis=-1)


def _to_blocks(q):
    B, H, S, d = q.shape
    return q.reshape(B, H, S // Q_BLOCK, Q_BLOCK, d).transpose(2, 0, 1, 3, 4)


def _from_blocks(o):
    nb, B, H, QB, d = o.shape
    return o.transpose(1, 2, 0, 3, 4).reshape(B, H, nb * QB, d)


def _stick_breaking_attention(q, k, v):
    S, d = q.shape[2], q.shape[3]
    key_pos = jnp.arange(S)
    inv_sqrt_d = 1.0 / math.sqrt(d)

    def block(args):
        i, qi = args
        q_pos = i * Q_BLOCK + jnp.arange(Q_BLOCK)
        z = jnp.einsum('bhqd,bhkd->bhqk', qi, k).astype(jnp.float32) * inv_sqrt_d
        strict = key_pos[None, :] < q_pos[:, None]
        log_keep = jnp.where(strict, jax.nn.log_sigmoid(-z), 0.0)
        after = lax.cumsum(log_keep, axis=3, reverse=True) - log_keep
        w = jnp.where(strict, jnp.exp(jax.nn.log_sigmoid(z) + after), 0.0)
        return jnp.einsum('bhqk,bhkd->bhqd', w.astype(v.dtype), v)

    out = lax.map(block, (jnp.arange(S // Q_BLOCK), _to_blocks(q)))
    return _from_blocks(out)


def _causal_softmax_attention(q, k, v):
    S, d = q.shape[2], q.shape[3]
    key_pos = jnp.arange(S)
    scale = 1.0 / math.sqrt(d)
    neg = jnp.finfo(jnp.float32).min

    def block(args):
        i, qi = args
        q_pos = i * Q_BLOCK + jnp.arange(Q_BLOCK)
        z = jnp.einsum('bhqd,bhkd->bhqk', qi, k).astype(jnp.float32) * scale
        causal = key_pos[None, :] <= q_pos[:, None]
        p = jax.nn.softmax(jnp.where(causal, z, neg), axis=-1)
        return jnp.einsum('bhqk,bhkd->bhqd', p.astype(v.dtype), v)

    out = lax.map(block, (jnp.arange(S // Q_BLOCK), _to_blocks(q)))
    return _from_blocks(out)


def _layer(x, c, cos, sin, w_ada, b_ada, norm_w, w_in, q_lora_norm, w_uq,
           kv_lora_norm, w_ukv, q_head_norm, k_head_norm, w_out):
    B, S, _ = x.shape
    ada = jax.nn.silu(c) @ w_ada + b_ada
    shift, scale, gate = jnp.split(ada[:, None, :], 3, axis=-1)
    h = _rms_norm(x, norm_w) * (1.0 + scale) + shift

    proj = h @ w_in
    q_sb, k_sb, v_sb, g_sb, c_q, c_kv, k_rope, g_mla = jnp.split(
        proj, np.cumsum(IN_SIZES)[:-1].tolist(), axis=-1)

    def heads_sb(t):
        return t.reshape(B, S, SB_HEADS, SB_HEAD_DIM).transpose(0, 2, 1, 3)
    o_sb = _stick_breaking_attention(heads_sb(q_sb), heads_sb(k_sb), heads_sb(v_sb))
    o_sb = o_sb.transpose(0, 2, 1, 3).reshape(B, S, SB_WIDTH) * jax.nn.silu(g_sb)

    q = (_rms_norm(c_q, q_lora_norm) @ w_uq).reshape(B, S, MLA_HEADS, MLA_QK_DIM)
    kv = (_rms_norm(c_kv, kv_lora_norm) @ w_ukv).reshape(
        B, S, MLA_HEADS, MLA_NOPE_DIM + MLA_V_DIM)
    k_nope, v_mla = kv[..., :MLA_NOPE_DIM], kv[..., MLA_NOPE_DIM:]
    k_r = jnp.broadcast_to(k_rope[:, :, None, :], (B, S, MLA_HEADS, MLA_ROPE_DIM))
    k = jnp.concatenate([k_nope, k_r], axis=-1)
    q = _rms_norm(q, q_head_norm)
    k = _rms_norm(k, k_head_norm)
    def rope(t):
        t_n, t_r = t[..., :MLA_NOPE_DIM], t[..., MLA_NOPE_DIM:]
        t_r = t_r * cos + _rotate_half(t_r) * sin
        return jnp.concatenate([t_n, t_r], axis=-1)
    q, k = rope(q), rope(k)
    o_mla = _causal_softmax_attention(q.transpose(0, 2, 1, 3), k.transpose(0, 2, 1, 3),
                                      v_mla.transpose(0, 2, 1, 3))
    o_mla = o_mla.transpose(0, 2, 1, 3).reshape(B, S, MLA_WIDTH) * jax.nn.silu(g_mla)

    mixed = jnp.concatenate([o_sb, o_mla], axis=-1)
    return x + gate * (mixed @ w_out)


def setup_inputs(seed: int = 0) -> dict:
    key = jax.random.key(seed)
    ks = jax.random.split(key, 16)
    D = D_MODEL

    def nrm(k, shape, std):
        return jax.random.normal(k, shape, jnp.float32) * std

    def gain(k, n):
        return 1.0 + 0.01 * jax.random.normal(k, (DEPTH, n), jnp.float32)

    return {
        "x": nrm(ks[0], (BATCH, SEQ, D), 1.0),
        "c": nrm(ks[1], (BATCH, D), 1.0),
        "positions": jnp.broadcast_to(jnp.arange(SEQ, dtype=jnp.int32), (BATCH, SEQ)),
        "w_ada": nrm(ks[2], (DEPTH, D, 3 * D), 0.5 * D ** -0.5),
        "b_ada": nrm(ks[3], (DEPTH, 3 * D), 0.01),
        "norm_w": gain(ks[4], D),
        "w_in": nrm(ks[5], (DEPTH, D, IN_COLS), D ** -0.5),
        "q_lora_norm": gain(ks[6], Q_LORA_RANK),
        "w_uq": nrm(ks[7], (DEPTH, Q_LORA_RANK, MLA_HEADS * MLA_QK_DIM), Q_LORA_RANK ** -0.5),
        "kv_lora_norm": gain(ks[8], KV_LORA_RANK),
        "w_ukv": nrm(ks[9], (DEPTH, KV_LORA_RANK, MLA_HEADS * (MLA_NOPE_DIM + MLA_V_DIM)),
                     KV_LORA_RANK ** -0.5),
        "q_head_norm": gain(ks[10], MLA_QK_DIM),
        "k_head_norm": gain(ks[11], MLA_QK_DIM),
        "w_out": nrm(ks[12], (DEPTH, MIX_WIDTH, D), MIX_WIDTH ** -0.5),
    }


def reference(x, c, positions, w_ada, b_ada, norm_w, w_in, q_lora_norm, w_uq,
              kv_lora_norm, w_ukv, q_head_norm, k_head_norm, w_out):
    inv_freq = ROPE_THETA ** (-jnp.arange(0, MLA_ROPE_DIM, 2, dtype=jnp.float32) / MLA_ROPE_DIM)
    ang = positions.astype(jnp.float32)[..., None] * inv_freq
    ang = jnp.concatenate([ang, ang], axis=-1)[:, :, None, :]
    cos = jnp.cos(ang).astype(x.dtype)
    sin = jnp.sin(ang).astype(x.dtype)
    for l in range(DEPTH):
        x = _layer(x, c, cos, sin, w_ada[l], b_ada[l], norm_w[l], w_in[l],
                   q_lora_norm[l], w_uq[l], kv_lora_norm[l], w_ukv[l],
                   q_head_norm[l], k_head_norm[l], w_out[l])
    return x
```

```python
import functools
import math

import jax
import jax.numpy as jnp
from jax import lax
from jax.experimental import pallas as pl
from jax.experimental.pallas import tpu as pltpu

F32 = jnp.float32
BF16 = jnp.bfloat16

D_MODEL = 1024
SB_HEADS = 8
SB_HEAD_DIM = 64
SB_WIDTH = SB_HEADS * SB_HEAD_DIM
MLA_HEADS = 8
MLA_NOPE_DIM = 64
MLA_ROPE_DIM = 32
MLA_QK_DIM = MLA_NOPE_DIM + MLA_ROPE_DIM
MLA_V_DIM = 64
MLA_WIDTH = MLA_HEADS * MLA_V_DIM
Q_LORA_RANK = 384
KV_LORA_RANK = 256
ROPE_THETA = 10000.0
EPS = 1e-6

LANES = 128
HEAD_PAIR = 2
MLA_PAD_WIDTH = MLA_HEADS * LANES

_C_QSB = 0
_C_KSB = _C_QSB + SB_WIDTH
_C_VSB = _C_KSB + SB_WIDTH
_C_GSB = _C_VSB + SB_WIDTH
_C_CQ = _C_GSB + SB_WIDTH
_C_CKV = _C_CQ + Q_LORA_RANK
_C_KR = _C_CKV + KV_LORA_RANK
_C_GMLA = _C_KR + LANES
_C_END = _C_GMLA + MLA_WIDTH

ROW_TILE = 512
SB_TQ = 256
SB_TK = 256
MLA_TQ = 512
MLA_TK = 512
VMEM_LIMIT = 56 * 1024 * 1024

SB_DEAD_LOG_WEIGHT = -110.0
MLA_Q_SCALE = math.log2(math.e) / math.sqrt(MLA_QK_DIM)


def _silu(g):
    return g * (1.0 / (1.0 + jnp.exp(-g)))


def _dot(a, b):
    return jnp.dot(a, b, preferred_element_type=F32)


def _dot_nt(a, b):
    return lax.dot_general(a, b, (((1,), (1,)), ((), ())), preferred_element_type=F32)


def _adaln_kernel(c_ref, w_ref, b_ref, o_ref):
    o_ref[...] = jnp.dot(_silu(c_ref[...]), w_ref[...], preferred_element_type=F32,
                         precision=lax.Precision.HIGHEST) + b_ref[...]


def _adaln(c_pad, w_ada, b_ada):
    rows, d = c_pad.shape
    n = w_ada.shape[1]
    bn = 512
    return pl.pallas_call(
        _adaln_kernel,
        grid=(n // bn,),
        in_specs=[pl.BlockSpec((rows, d), lambda j: (0, 0)),
                  pl.BlockSpec((d, bn), lambda j: (0, j)),
                  pl.BlockSpec((1, bn), lambda j: (0, j))],
        out_specs=pl.BlockSpec((rows, bn), lambda j: (0, j)),
        out_shape=jax.ShapeDtypeStruct((rows, n), F32),
        name="adaln",
    )(c_pad, w_ada, b_ada)


def _rope(t, tc, ts1, ts2):
    return (t * tc + pltpu.roll(t, LANES - MLA_ROPE_DIM // 2, 1) * ts1
            + pltpu.roll(t, MLA_ROPE_DIM // 2, 1) * ts2)


def _inproj_kernel(x_ref, mod_ref, nw_ref, win_ref, qln_ref, wuq_ref, kvln_ref, wukv_ref,
                   gq_ref, gk_ref, tc_ref, ts1_ref, ts2_ref,
                   qsb_ref, ksb_ref, vsb_ref, gsb_ref, qm_ref, km_ref, vm_ref, gm_ref):
    x = x_ref[0]
    y = x * lax.rsqrt(jnp.mean(x * x, axis=-1, keepdims=True) + EPS) * nw_ref[...]
    h = (y * (1.0 + mod_ref[0, 1:2, :]) + mod_ref[0, 0:1, :]).astype(BF16)

    def proj(lo, hi):
        return _dot(h, win_ref[:, lo:hi])

    qsb_ref[0] = (proj(_C_QSB, _C_KSB) * (1.0 / math.sqrt(SB_HEAD_DIM))).astype(BF16)
    ksb_ref[0] = proj(_C_KSB, _C_VSB).astype(BF16)
    vsb_ref[0] = proj(_C_VSB, _C_GSB).astype(BF16)
    gsb_ref[0] = proj(_C_GSB, _C_CQ)
    gm_ref[0] = proj(_C_GMLA, _C_END)

    tc, ts1, ts2 = tc_ref[0], ts1_ref[0], ts2_ref[0]

    cq = proj(_C_CQ, _C_CKV)
    cq = cq * lax.rsqrt(jnp.mean(cq * cq, axis=-1, keepdims=True) + EPS) * qln_ref[...]
    q = _dot(cq.astype(BF16), wuq_ref[...])
    for hd in range(MLA_HEADS):
        qh = q[:, hd * LANES:(hd + 1) * LANES]
        inv = lax.rsqrt(jnp.sum(qh * qh, axis=-1, keepdims=True) * (1.0 / MLA_QK_DIM) + EPS)
        roped = _rope(qh * gq_ref[...], tc, ts1, ts2)
        qm_ref[0, :, hd * LANES:(hd + 1) * LANES] = (roped * (inv * MLA_Q_SCALE)).astype(BF16)

    ckv = proj(_C_CKV, _C_KR)
    ckv = ckv * lax.rsqrt(jnp.mean(ckv * ckv, axis=-1, keepdims=True) + EPS) * kvln_ref[...]
    kv = _dot(ckv.astype(BF16), wukv_ref[...])
    vm_ref[0] = kv[:, MLA_PAD_WIDTH:].astype(BF16)
    kr = proj(_C_KR, _C_GMLA)
    kr_ssq = jnp.sum(kr * kr, axis=-1, keepdims=True)
    kr_roped = _rope(kr * gk_ref[...], tc, ts1, ts2)
    for hd in range(MLA_HEADS):
        kn = kv[:, hd * LANES:(hd + 1) * LANES]
        ssq = jnp.sum(kn * kn, axis=-1, keepdims=True) + kr_ssq
        inv = lax.rsqrt(ssq * (1.0 / MLA_QK_DIM) + EPS)
        km_ref[0, :, hd * LANES:(hd + 1) * LANES] = ((kn * gk_ref[...] + kr_roped) * inv).astype(BF16)


def _inproj(x, mod, norm_w, w_in_p, qln, w_uq_p, kvln, w_ukv_p, gq, gk, tc, ts1, ts2):
    B, S, D = x.shape
    ts = min(ROW_TILE, S)
    grid = (B, S // ts)

    def whole(a):
        return pl.BlockSpec(a.shape, lambda b, i: (0,) * a.ndim)

    def rows(width):
        return pl.BlockSpec((1, ts, width), lambda b, i: (b, i, 0))

    out_shapes = (
        jax.ShapeDtypeStruct((B, S, SB_WIDTH), BF16),
        jax.ShapeDtypeStruct((B, S, SB_WIDTH), BF16),
        jax.ShapeDtypeStruct((B, S, SB_WIDTH), BF16),
        jax.ShapeDtypeStruct((B, S, SB_WIDTH), F32),
        jax.ShapeDtypeStruct((B, S, MLA_PAD_WIDTH), BF16),
        jax.ShapeDtypeStruct((B, S, MLA_PAD_WIDTH), BF16),
        jax.ShapeDtypeStruct((B, S, MLA_WIDTH), BF16),
        jax.ShapeDtypeStruct((B, S, MLA_WIDTH), F32),
    )
    return pl.pallas_call(
        _inproj_kernel,
        grid=grid,
        in_specs=[rows(D),
                  pl.BlockSpec((1, 3, D), lambda b, i: (b, 0, 0)),
                  whole(norm_w), whole(w_in_p), whole(qln), whole(w_uq_p), whole(kvln),
                  whole(w_ukv_p), whole(gq), whole(gk),
                  rows(LANES), rows(LANES), rows(LANES)],
        out_specs=[rows(s.shape[-1]) for s in out_shapes],
        out_shape=out_shapes,
        compiler_params=pltpu.CompilerParams(
            dimension_semantics=("arbitrary", "arbitrary"), vmem_limit_bytes=VMEM_LIMIT),
        name="inproj",
    )(x, mod, norm_w, w_in_p, qln, w_uq_p, kvln, w_ukv_p, gq, gk, tc, ts1, ts2)


def _sb_kernel(q_ref, k_ref, v_ref, g_ref, o_ref, acc_ref, carry_ref):
    tq, tk = SB_TQ, SB_TK
    i = pl.program_id(2)
    lane = lax.broadcasted_iota(jnp.int32, (tq, LANES), 1)
    q = q_ref[0]
    zero = jnp.zeros_like(q)
    q_heads = [jnp.where((lane // SB_HEAD_DIM) == hh, q, zero) for hh in range(HEAD_PAIR)]

    row = lax.broadcasted_iota(jnp.int32, (tq, tk), 0)
    col = lax.broadcasted_iota(jnp.int32, (tq, tk), 1)
    strict = col < row
    upper = (lax.broadcasted_iota(jnp.int32, (tk, tk), 0)
             > lax.broadcasted_iota(jnp.int32, (tk, tk), 1)).astype(BF16)

    acc_ref[...] = jnp.zeros_like(acc_ref)
    carry_ref[...] = jnp.zeros_like(carry_ref)

    def visit(j, masked):
        start = pl.multiple_of(j * tk, tk)
        k_blk = k_ref[0, pl.ds(start, tk), :]
        v_blk = v_ref[0, pl.ds(start, tk), :]
        alive = None
        for hh in range(HEAD_PAIR):
            z = _dot_nt(q_heads[hh], k_blk)
            log_keep = jnp.minimum(-z, 0.0) - jnp.log(1.0 + jnp.exp(-jnp.abs(z)))
            if masked:
                log_keep = jnp.where(strict, log_keep, 0.0)
            hi = log_keep.astype(BF16)
            lo = (log_keep - hi.astype(F32)).astype(BF16)
            after = _dot(hi, upper) + _dot(lo, upper)
            carry = carry_ref[hh]
            w = jnp.exp((z + log_keep) + (after + carry))
            if masked:
                w = jnp.where(strict, w, 0.0)
            acc_ref[hh] += _dot(w.astype(BF16), v_blk)
            carry = carry + jnp.sum(log_keep, axis=-1, keepdims=True)
            carry_ref[hh] = carry
            head_alive = jnp.max(carry) > SB_DEAD_LOG_WEIGHT
            alive = head_alive if alive is None else jnp.logical_or(alive, head_alive)
        return alive

    alive0 = visit(i, True)

    def cond(state):
        j, alive = state
        return jnp.logical_and(j >= 0, alive)

    def body(state):
        j, _ = state
        return j - 1, visit(j, False)

    lax.while_loop(cond, body, (i - 1, alive0))

    o = jnp.where(lane < SB_HEAD_DIM, acc_ref[0], acc_ref[1])
    o_ref[0] = (o * _silu(g_ref[0])).astype(o_ref.dtype)


def _sb_attention(q, k, v, g):
    B, S, W = q.shape
    assert S % SB_TQ == 0 and SB_TQ == SB_TK
    grid = (B, W // LANES, S // SB_TQ)
    q_spec = pl.BlockSpec((1, SB_TQ, LANES), lambda b, p, i: (b, i, p))
    kv_spec = pl.BlockSpec((1, S, LANES), lambda b, p, i: (b, 0, p))
    return pl.pallas_call(
        _sb_kernel,
        grid=grid,
        in_specs=[q_spec, kv_spec, kv_spec, q_spec],
        out_specs=q_spec,
        out_shape=jax.ShapeDtypeStruct((B, S, W), BF16),
        scratch_shapes=[pltpu.VMEM((HEAD_PAIR, SB_TQ, LANES), F32),
                        pltpu.VMEM((HEAD_PAIR, SB_TQ, 1), F32)],
        compiler_params=pltpu.CompilerParams(
            dimension_semantics=("arbitrary", "arbitrary", "arbitrary"),
            vmem_limit_bytes=VMEM_LIMIT),
        name="sb_attn",
    )(q, k, v, g)


def _mla_kernel(q_ref, k_ref, v_ref, g_ref, o_ref, acc_ref, m_ref, l_ref):
    tq, tk = MLA_TQ, MLA_TK
    i = pl.program_id(2)
    causal = (lax.broadcasted_iota(jnp.int32, (tq, tk), 1)
              <= lax.broadcasted_iota(jnp.int32, (tq, tk), 0))
    neg = jnp.finfo(F32).min

    acc_ref[...] = jnp.zeros_like(acc_ref)
    l_ref[...] = jnp.zeros_like(l_ref)
    m_ref[...] = jnp.full_like(m_ref, neg)

    def visit(j, masked):
        start = pl.multiple_of(j * tk, tk)
        v_blk = v_ref[0, pl.ds(start, tk), :]
        for hh in range(HEAD_PAIR):
            q = q_ref[0, :, hh * LANES:(hh + 1) * LANES]
            k_blk = k_ref[0, pl.ds(start, tk), hh * LANES:(hh + 1) * LANES]
            s = _dot_nt(q, k_blk)
            if masked:
                s = jnp.where(causal, s, neg)
            m_old = m_ref[hh]
            m_new = jnp.maximum(m_old, jnp.max(s, axis=-1, keepdims=True))
            alpha = jnp.exp2(m_old - m_new)
            p = jnp.exp2(s - m_new)
            l_ref[hh] = alpha * l_ref[hh] + jnp.sum(p, axis=-1, keepdims=True)
            acc_ref[hh] = alpha * acc_ref[hh] + _dot(p.astype(BF16), v_blk)
            m_ref[hh] = m_new

    def body(j, carry):
        visit(j, False)
        return carry

    lax.fori_loop(0, i, body, 0)
    visit(i, True)

    lane = lax.broadcasted_iota(jnp.int32, (tq, LANES), 1)
    o = jnp.where(lane < MLA_V_DIM, acc_ref[0] / l_ref[0], acc_ref[1] / l_ref[1])
    o_ref[0] = (o * _silu(g_ref[0])).astype(o_ref.dtype)


def _mla_attention(q, k, v, g):
    B, S, W = v.shape
    tq = MLA_TQ
    assert S % tq == 0 and MLA_TQ == MLA_TK
    grid = (B, W // LANES, S // tq)
    return pl.pallas_call(
        _mla_kernel,
        grid=grid,
        in_specs=[pl.BlockSpec((1, tq, HEAD_PAIR * LANES), lambda b, p, i: (b, i, p)),
                  pl.BlockSpec((1, S, HEAD_PAIR * LANES), lambda b, p, i: (b, 0, p)),
                  pl.BlockSpec((1, S, LANES), lambda b, p, i: (b, 0, p)),
                  pl.BlockSpec((1, tq, LANES), lambda b, p, i: (b, i, p))],
        out_specs=pl.BlockSpec((1, tq, LANES), lambda b, p, i: (b, i, p)),
        out_shape=jax.ShapeDtypeStruct((B, S, W), BF16),
        scratch_shapes=[pltpu.VMEM((HEAD_PAIR, tq, LANES), F32),
                        pltpu.VMEM((HEAD_PAIR, tq, 1), F32),
                        pltpu.VMEM((HEAD_PAIR, tq, 1), F32)],
        compiler_params=pltpu.CompilerParams(
            dimension_semantics=("arbitrary", "arbitrary", "arbitrary"),
            vmem_limit_bytes=VMEM_LIMIT),
        name="mla_attn",
    )(q, k, v, g)


def _outproj_kernel(msb_ref, mmla_ref, x_ref, mod_ref, w_ref, o_ref):
    y = _dot(msb_ref[0], w_ref[:SB_WIDTH, :]) + _dot(mmla_ref[0], w_ref[SB_WIDTH:, :])
    o_ref[0] = x_ref[0] + mod_ref[0, 2:3, :] * y


def _outproj(mixed_sb, mixed_mla, x, mod, w_out):
    B, S, D = x.shape
    ts = min(ROW_TILE, S)

    def rows(width):
        return pl.BlockSpec((1, ts, width), lambda b, i: (b, i, 0))

    return pl.pallas_call(
        _outproj_kernel,
        grid=(B, S // ts),
        in_specs=[rows(SB_WIDTH), rows(MLA_WIDTH), rows(D),
                  pl.BlockSpec((1, 3, D), lambda b, i: (b, 0, 0)),
                  pl.BlockSpec(w_out.shape, lambda b, i: (0, 0))],
        out_specs=rows(D),
        out_shape=jax.ShapeDtypeStruct((B, S, D), x.dtype),
        compiler_params=pltpu.CompilerParams(
            dimension_semantics=("arbitrary", "arbitrary"), vmem_limit_bytes=VMEM_LIMIT),
        name="outproj",
    )(mixed_sb, mixed_mla, x, mod, w_out)


def _pad_last(a, width):
    return jnp.pad(a, [(0, 0)] * (a.ndim - 1) + [(0, width - a.shape[-1])])


def _layer(x, c, tables, w_ada, b_ada, norm_w, w_in, q_lora_norm, w_uq, kv_lora_norm, w_ukv,
           q_head_norm, k_head_norm, w_out):
    B, S, D = x.shape
    c_pad = jnp.pad(c, ((0, 8 - B), (0, 0)))
    ada = _adaln(c_pad, w_ada, b_ada[None, :])[:B]
    mod = ada.reshape(B, 3, D)

    zeros = lambda n: jnp.zeros((D, n), w_in.dtype)
    kr_lo = _C_KR
    w_in_p = jnp.concatenate(
        [w_in[:, :kr_lo], zeros(MLA_NOPE_DIM), w_in[:, kr_lo:kr_lo + MLA_ROPE_DIM],
         zeros(LANES - MLA_QK_DIM), w_in[:, kr_lo + MLA_ROPE_DIM:]], axis=1).astype(BF16)
    w_uq_p = _pad_last(w_uq.reshape(Q_LORA_RANK, MLA_HEADS, MLA_QK_DIM), LANES)
    w_uq_p = w_uq_p.reshape(Q_LORA_RANK, MLA_PAD_WIDTH).astype(BF16)
    w_ukv_h = w_ukv.reshape(KV_LORA_RANK, MLA_HEADS, MLA_NOPE_DIM + MLA_V_DIM)
    w_uk_p = _pad_last(w_ukv_h[:, :, :MLA_NOPE_DIM], LANES).reshape(KV_LORA_RANK, MLA_PAD_WIDTH)
    w_uv = w_ukv_h[:, :, MLA_NOPE_DIM:].reshape(KV_LORA_RANK, MLA_WIDTH)
    w_ukv_p = jnp.concatenate([w_uk_p, w_uv], axis=1).astype(BF16)
    gq = _pad_last(q_head_norm[None, :], LANES)
    gk = _pad_last(k_head_norm[None, :], LANES)

    (q_sb, k_sb, v_sb, g_sb, q_m, k_m, v_m, g_m) = _inproj(
        x, mod, norm_w[None, :], w_in_p, q_lora_norm[None, :], w_uq_p, kv_lora_norm[None, :],
        w_ukv_p, gq, gk, *tables)

    mixed_sb = _sb_attention(q_sb, k_sb, v_sb, g_sb)
    mixed_mla = _mla_attention(q_m, k_m, v_m, g_m)
    return _outproj(mixed_sb, mixed_mla, x, mod, w_out.astype(BF16))


def _rope_tables(positions, dtype):
    half = MLA_ROPE_DIM // 2
    inv_freq = ROPE_THETA ** (-jnp.arange(0, MLA_ROPE_DIM, 2, dtype=F32) / MLA_ROPE_DIM)
    ang = positions.astype(F32)[..., None] * inv_freq
    cos = jnp.cos(ang).astype(dtype)
    sin = jnp.sin(ang).astype(dtype)
    B, S = positions.shape
    z = lambda n: jnp.zeros((B, S, n), dtype)
    tail = LANES - MLA_QK_DIM
    tc = jnp.concatenate([jnp.ones((B, S, MLA_NOPE_DIM), dtype), cos, cos, z(tail)], axis=-1)
    ts1 = jnp.concatenate([z(MLA_NOPE_DIM), -sin, z(half), z(tail)], axis=-1)
    ts2 = jnp.concatenate([z(MLA_NOPE_DIM), z(half), sin, z(tail)], axis=-1)
    return tc, ts1, ts2


def kernel(x, c, positions, w_ada, b_ada, norm_w, w_in, q_lora_norm, w_uq, kv_lora_norm, w_ukv,
           q_head_norm, k_head_norm, w_out):
    tables = _rope_tables(positions, x.dtype)
    for l in range(w_ada.shape[0]):
        x = _layer(x, c, tables, w_ada[l], b_ada[l], norm_w[l], w_in[l], q_lora_norm[l], w_uq[l],
                   kv_lora_norm[l], w_ukv[l], q_head_norm[l], k_head_norm[l], w_out[l])
    return x
```

```python
import functools
import math

import jax
import jax.numpy as jnp
from jax import lax
from jax.experimental import pallas as pl
from jax.experimental.pallas import tpu as pltpu

F32 = jnp.float32
BF16 = jnp.bfloat16

D_MODEL = 1024
SB_HEADS = 8
SB_HEAD_DIM = 64
SB_WIDTH = SB_HEADS * SB_HEAD_DIM
MLA_HEADS = 8
MLA_NOPE_DIM = 64
MLA_ROPE_DIM = 32
MLA_QK_DIM = MLA_NOPE_DIM + MLA_ROPE_DIM
MLA_V_DIM = 64
MLA_WIDTH = MLA_HEADS * MLA_V_DIM
Q_LORA_RANK = 384
KV_LORA_RANK = 256
ROPE_THETA = 10000.0
EPS = 1e-6

LANES = 128
HEAD_PAIR = 2
MLA_PAD_WIDTH = MLA_HEADS * LANES

_C_QSB = 0
_C_KSB = _C_QSB + SB_WIDTH
_C_VSB = _C_KSB + SB_WIDTH
_C_GSB = _C_VSB + SB_WIDTH
_C_CQ = _C_GSB + SB_WIDTH
_C_CKV = _C_CQ + Q_LORA_RANK
_C_KR = _C_CKV + KV_LORA_RANK
_C_GMLA = _C_KR + LANES
_C_END = _C_GMLA + MLA_WIDTH

ROW_TILE = 512
SB_TQ = 256
SB_TK = 256
MLA_TQ = 512
MLA_TK = 256
VMEM_LIMIT = 56 * 1024 * 1024

SB_DEAD_LOG_WEIGHT = -110.0
MLA_Q_SCALE = math.log2(math.e) / math.sqrt(MLA_QK_DIM)


def _silu(g):
    return g * (1.0 / (1.0 + jnp.exp(-g)))


def _dot(a, b):
    return jnp.dot(a, b, preferred_element_type=F32)


def _dot_nt(a, b):
    return lax.dot_general(a, b, (((1,), (1,)), ((), ())), preferred_element_type=F32)


def _dot_tn(a, b):
    return lax.dot_general(a, b, (((0,), (0,)), ((), ())), preferred_element_type=F32)


def _adaln_kernel(c_ref, w_ref, b_ref, o_ref):
    o_ref[...] = jnp.dot(_silu(c_ref[...]), w_ref[...], preferred_element_type=F32,
                         precision=lax.Precision.HIGHEST) + b_ref[...]


def _adaln(c_pad, w_ada, b_ada):
    rows, d = c_pad.shape
    n = w_ada.shape[1]
    bn = 512
    return pl.pallas_call(
        _adaln_kernel,
        grid=(n // bn,),
        in_specs=[pl.BlockSpec((rows, d), lambda j: (0, 0)),
                  pl.BlockSpec((d, bn), lambda j: (0, j)),
                  pl.BlockSpec((1, bn), lambda j: (0, j))],
        out_specs=pl.BlockSpec((rows, bn), lambda j: (0, j)),
        out_shape=jax.ShapeDtypeStruct((rows, n), F32),
        name="adaln",
    )(c_pad, w_ada, b_ada)


def _rope(t, tc, ts1, ts2):
    return (t * tc + pltpu.roll(t, LANES - MLA_ROPE_DIM // 2, 1) * ts1
            + pltpu.roll(t, MLA_ROPE_DIM // 2, 1) * ts2)


def _inproj_kernel(x_ref, mod_ref, nw_ref, win_ref, qln_ref, wuq_ref, kvln_ref, wukv_ref,
                   gq_ref, gk_ref, tc_ref, ts1_ref, ts2_ref,
                   qsb_ref, ksb_ref, vsb_ref, gsb_ref, qm_ref, km_ref, vm_ref, gm_ref):
    x = x_ref[0]
    y = x * lax.rsqrt(jnp.mean(x * x, axis=-1, keepdims=True) + EPS) * nw_ref[...]
    h = (y * (1.0 + mod_ref[0, 1:2, :]) + mod_ref[0, 0:1, :]).astype(BF16)

    def proj(lo, hi):
        return _dot(h, win_ref[:, lo:hi])

    qsb_ref[0] = (proj(_C_QSB, _C_KSB) * (1.0 / math.sqrt(SB_HEAD_DIM))).astype(BF16)
    ksb_ref[0] = proj(_C_KSB, _C_VSB).astype(BF16)
    vsb_ref[0] = proj(_C_VSB, _C_GSB).astype(BF16)
    gsb_ref[0] = proj(_C_GSB, _C_CQ)
    gm_ref[0] = proj(_C_GMLA, _C_END)

    tc, ts1, ts2 = tc_ref[0], ts1_ref[0], ts2_ref[0]

    cq = proj(_C_CQ, _C_CKV)
    cq = cq * lax.rsqrt(jnp.mean(cq * cq, axis=-1, keepdims=True) + EPS) * qln_ref[...]
    q = _dot(cq.astype(BF16), wuq_ref[...])
    for hd in range(MLA_HEADS):
        qh = q[:, hd * LANES:(hd + 1) * LANES]
        inv = lax.rsqrt(jnp.sum(qh * qh, axis=-1, keepdims=True) * (1.0 / MLA_QK_DIM) + EPS)
        roped = _rope(qh * gq_ref[...], tc, ts1, ts2)
        qm_ref[0, :, hd * LANES:(hd + 1) * LANES] = (roped * (inv * MLA_Q_SCALE)).astype(BF16)

    ckv = proj(_C_CKV, _C_KR)
    ckv = ckv * lax.rsqrt(jnp.mean(ckv * ckv, axis=-1, keepdims=True) + EPS) * kvln_ref[...]
    kv = _dot(ckv.astype(BF16), wukv_ref[...])
    vm_ref[0] = kv[:, MLA_PAD_WIDTH:].astype(BF16)
    kr = proj(_C_KR, _C_GMLA)
    kr_ssq = jnp.sum(kr * kr, axis=-1, keepdims=True)
    kr_roped = _rope(kr * gk_ref[...], tc, ts1, ts2)
    for hd in range(MLA_HEADS):
        kn = kv[:, hd * LANES:(hd + 1) * LANES]
        ssq = jnp.sum(kn * kn, axis=-1, keepdims=True) + kr_ssq
        inv = lax.rsqrt(ssq * (1.0 / MLA_QK_DIM) + EPS)
        km_ref[0, :, hd * LANES:(hd + 1) * LANES] = ((kn * gk_ref[...] + kr_roped) * inv).astype(BF16)


def _inproj(x, mod, norm_w, w_in_p, qln, w_uq_p, kvln, w_ukv_p, gq, gk, tc, ts1, ts2):
    B, S, D = x.shape
    ts = min(ROW_TILE, S)
    grid = (B, S // ts)

    def whole(a):
        return pl.BlockSpec(a.shape, lambda b, i: (0,) * a.ndim)

    def rows(width):
        return pl.BlockSpec((1, ts, width), lambda b, i: (b, i, 0))

    out_shapes = (
        jax.ShapeDtypeStruct((B, S, SB_WIDTH), BF16),
        jax.ShapeDtypeStruct((B, S, SB_WIDTH), BF16),
        jax.ShapeDtypeStruct((B, S, SB_WIDTH), BF16),
        jax.ShapeDtypeStruct((B, S, SB_WIDTH), F32),
        jax.ShapeDtypeStruct((B, S, MLA_PAD_WIDTH), BF16),
        jax.ShapeDtypeStruct((B, S, MLA_PAD_WIDTH), BF16),
        jax.ShapeDtypeStruct((B, S, MLA_WIDTH), BF16),
        jax.ShapeDtypeStruct((B, S, MLA_WIDTH), F32),
    )
    return pl.pallas_call(
        _inproj_kernel,
        grid=grid,
        in_specs=[rows(D),
                  pl.BlockSpec((1, 3, D), lambda b, i: (b, 0, 0)),
                  whole(norm_w), whole(w_in_p), whole(qln), whole(w_uq_p), whole(kvln),
                  whole(w_ukv_p), whole(gq), whole(gk),
                  rows(LANES), rows(LANES), rows(LANES)],
        out_specs=[rows(s.shape[-1]) for s in out_shapes],
        out_shape=out_shapes,
        compiler_params=pltpu.CompilerParams(
            dimension_semantics=("arbitrary", "arbitrary"), vmem_limit_bytes=VMEM_LIMIT),
        name="inproj",
    )(x, mod, norm_w, w_in_p, qln, w_uq_p, kvln, w_ukv_p, gq, gk, tc, ts1, ts2)


def _sb_kernel(q_ref, k_ref, v_ref, g_ref, o_ref, acc_ref, carry_ref):
    tq, tk = SB_TQ, SB_TK
    i = pl.program_id(2)
    lane = lax.broadcasted_iota(jnp.int32, (tq, LANES), 1)
    q = q_ref[0]
    zero = jnp.zeros_like(q)
    q_heads = [jnp.where((lane // SB_HEAD_DIM) == hh, q, zero) for hh in range(HEAD_PAIR)]

    row = lax.broadcasted_iota(jnp.int32, (tq, tk), 0)
    col = lax.broadcasted_iota(jnp.int32, (tq, tk), 1)
    strict = col < row
    upper = (lax.broadcasted_iota(jnp.int32, (tk, tk), 0)
             > lax.broadcasted_iota(jnp.int32, (tk, tk), 1)).astype(BF16)

    acc_ref[...] = jnp.zeros_like(acc_ref)
    carry_ref[...] = jnp.zeros_like(carry_ref)

    def visit(j, masked):
        start = pl.multiple_of(j * tk, tk)
        k_blk = k_ref[0, pl.ds(start, tk), :]
        v_blk = v_ref[0, pl.ds(start, tk), :]
        alive = None
        for hh in range(HEAD_PAIR):
            z = _dot_nt(q_heads[hh], k_blk)
            log_keep = jnp.minimum(-z, 0.0) - jnp.log(1.0 + jnp.exp(-jnp.abs(z)))
            if masked:
                log_keep = jnp.where(strict, log_keep, 0.0)
            hi = log_keep.astype(BF16)
            lo = (log_keep - hi.astype(F32)).astype(BF16)
            after = _dot(hi, upper) + _dot(lo, upper)
            carry = carry_ref[hh]
            w = jnp.exp((z + log_keep) + (after + carry))
            if masked:
                w = jnp.where(strict, w, 0.0)
            acc_ref[hh] += _dot(w.astype(BF16), v_blk)
            carry = carry + jnp.sum(log_keep, axis=-1, keepdims=True)
            carry_ref[hh] = carry
            head_alive = jnp.max(carry) > SB_DEAD_LOG_WEIGHT
            alive = head_alive if alive is None else jnp.logical_or(alive, head_alive)
        return alive

    alive0 = visit(i, True)

    def cond(state):
        j, alive = state
        return jnp.logical_and(j >= 0, alive)

    def body(state):
        j, _ = state
        return j - 1, visit(j, False)

    lax.while_loop(cond, body, (i - 1, alive0))

    o = jnp.where(lane < SB_HEAD_DIM, acc_ref[0], acc_ref[1])
    o_ref[0] = (o * _silu(g_ref[0])).astype(o_ref.dtype)


def _sb_attention(q, k, v, g):
    B, S, W = q.shape
    assert S % SB_TQ == 0 and SB_TQ == SB_TK
    grid = (B, W // LANES, S // SB_TQ)
    q_spec = pl.BlockSpec((1, SB_TQ, LANES), lambda b, p, i: (b, i, p))
    kv_spec = pl.BlockSpec((1, S, LANES), lambda b, p, i: (b, 0, p))
    return pl.pallas_call(
        _sb_kernel,
        grid=grid,
        in_specs=[q_spec, kv_spec, kv_spec, q_spec],
        out_specs=q_spec,
        out_shape=jax.ShapeDtypeStruct((B, S, W), BF16),
        scratch_shapes=[pltpu.VMEM((HEAD_PAIR, SB_TQ, LANES), F32),
                        pltpu.VMEM((HEAD_PAIR, SB_TQ, 1), F32)],
        compiler_params=pltpu.CompilerParams(
            dimension_semantics=("arbitrary", "arbitrary", "arbitrary"),
            vmem_limit_bytes=VMEM_LIMIT),
        name="sb_attn",
    )(q, k, v, g)


def _mla_kernel(q_ref, k_ref, v_ref, g_ref, o_ref, acc_ref, m_ref, l_ref, s_ref):
    tq, tk = MLA_TQ, MLA_TK
    diag_tiles = tq // tk
    i = pl.program_id(2)
    key_idx = lax.broadcasted_iota(jnp.int32, (tk, tq), 0)
    query_idx = lax.broadcasted_iota(jnp.int32, (tk, tq), 1)
    neg = jnp.finfo(F32).min

    acc_ref[...] = jnp.zeros_like(acc_ref)
    l_ref[...] = jnp.zeros_like(l_ref)
    m_ref[...] = jnp.full_like(m_ref, neg)

    def score(j, slot):
        start = pl.multiple_of(j * tk, tk)
        for hh in range(HEAD_PAIR):
            q = q_ref[0, :, hh * LANES:(hh + 1) * LANES]
            k_blk = k_ref[0, pl.ds(start, tk), hh * LANES:(hh + 1) * LANES]
            s_ref[slot, hh] = _dot_nt(k_blk, q)

    def consume(j, slot, diag):
        start = pl.multiple_of(j * tk, tk)
        v_blk = v_ref[0, pl.ds(start, tk), :]
        for hh in range(HEAD_PAIR):
            s = s_ref[slot, hh]
            if diag is not None:
                s = jnp.where(key_idx + diag * tk <= query_idx, s, neg)
            m_old = m_ref[hh]
            m_new = jnp.maximum(m_old, jnp.max(s, axis=0, keepdims=True))
            alpha = jnp.exp2(m_old - m_new)
            p = jnp.exp2(s - m_new)
            l_ref[hh] = alpha * l_ref[hh] + jnp.sum(p, axis=0, keepdims=True)
            pv = _dot_tn(v_blk, p.astype(BF16))
            acc_ref[hh] = alpha * acc_ref[hh] + pv[hh * MLA_V_DIM:(hh + 1) * MLA_V_DIM, :]
            m_ref[hh] = m_new

    base = i * diag_tiles
    score(0, 0)

    def body(jj, carry):
        t = 2 * jj
        score(t + 1, 1)
        consume(t, 0, None)
        score(t + 2, 0)
        consume(t + 1, 1, None)
        return carry

    lax.fori_loop(0, base // 2, body, 0)
    for d in range(diag_tiles):
        if d + 1 < diag_tiles:
            score(base + d + 1, (d + 1) % 2)
        consume(base + d, d % 2, d)

    o_t = jnp.concatenate([acc_ref[hh] / l_ref[hh] for hh in range(HEAD_PAIR)], axis=0)
    o_ref[0] = (o_t.T * _silu(g_ref[0])).astype(o_ref.dtype)


def _mla_attention(q, k, v, g):
    B, S, W = v.shape
    tq = MLA_TQ
    assert S % tq == 0 and MLA_TQ % (2 * MLA_TK) == 0
    grid = (B, W // LANES, S // tq)
    return pl.pallas_call(
        _mla_kernel,
        grid=grid,
        in_specs=[pl.BlockSpec((1, tq, HEAD_PAIR * LANES), lambda b, p, i: (b, i, p)),
                  pl.BlockSpec((1, S, HEAD_PAIR * LANES), lambda b, p, i: (b, 0, p)),
                  pl.BlockSpec((1, S, LANES), lambda b, p, i: (b, 0, p)),
                  pl.BlockSpec((1, tq, LANES), lambda b, p, i: (b, i, p))],
        out_specs=pl.BlockSpec((1, tq, LANES), lambda b, p, i: (b, i, p)),
        out_shape=jax.ShapeDtypeStruct((B, S, W), BF16),
        scratch_shapes=[pltpu.VMEM((HEAD_PAIR, MLA_V_DIM, tq), F32),
                        pltpu.VMEM((HEAD_PAIR, 1, tq), F32),
                        pltpu.VMEM((HEAD_PAIR, 1, tq), F32),
                        pltpu.VMEM((2, HEAD_PAIR, MLA_TK, tq), F32)],
        compiler_params=pltpu.CompilerParams(
            dimension_semantics=("arbitrary", "arbitrary", "arbitrary"),
            vmem_limit_bytes=VMEM_LIMIT),
        name="mla_attn",
    )(q, k, v, g)


def _outproj_kernel(msb_ref, mmla_ref, x_ref, mod_ref, w_ref, o_ref):
    y = _dot(msb_ref[0], w_ref[:SB_WIDTH, :]) + _dot(mmla_ref[0], w_ref[SB_WIDTH:, :])
    o_ref[0] = x_ref[0] + mod_ref[0, 2:3, :] * y


def _outproj(mixed_sb, mixed_mla, x, mod, w_out):
    B, S, D = x.shape
    ts = min(ROW_TILE, S)

    def rows(width):
        return pl.BlockSpec((1, ts, width), lambda b, i: (b, i, 0))

    return pl.pallas_call(
        _outproj_kernel,
        grid=(B, S // ts),
        in_specs=[rows(SB_WIDTH), rows(MLA_WIDTH), rows(D),
                  pl.BlockSpec((1, 3, D), lambda b, i: (b, 0, 0)),
                  pl.BlockSpec(w_out.shape, lambda b, i: (0, 0))],
        out_specs=rows(D),
        out_shape=jax.ShapeDtypeStruct((B, S, D), x.dtype),
        compiler_params=pltpu.CompilerParams(
            dimension_semantics=("arbitrary", "arbitrary"), vmem_limit_bytes=VMEM_LIMIT),
        name="outproj",
    )(mixed_sb, mixed_mla, x, mod, w_out)


def _pad_last(a, width):
    return jnp.pad(a, [(0, 0)] * (a.ndim - 1) + [(0, width - a.shape[-1])])


def _layer(x, c, tables, w_ada, b_ada, norm_w, w_in, q_lora_norm, w_uq, kv_lora_norm, w_ukv,
           q_head_norm, k_head_norm, w_out):
    B, S, D = x.shape
    c_pad = jnp.pad(c, ((0, 8 - B), (0, 0)))
    ada = _adaln(c_pad, w_ada, b_ada[None, :])[:B]
    mod = ada.reshape(B, 3, D)

    zeros = lambda n: jnp.zeros((D, n), w_in.dtype)
    kr_lo = _C_KR
    w_in_p = jnp.concatenate(
        [w_in[:, :kr_lo], zeros(MLA_NOPE_DIM), w_in[:, kr_lo:kr_lo + MLA_ROPE_DIM],
         zeros(LANES - MLA_QK_DIM), w_in[:, kr_lo + MLA_ROPE_DIM:]], axis=1).astype(BF16)
    w_uq_p = _pad_last(w_uq.reshape(Q_LORA_RANK, MLA_HEADS, MLA_QK_DIM), LANES)
    w_uq_p = w_uq_p.reshape(Q_LORA_RANK, MLA_PAD_WIDTH).astype(BF16)
    w_ukv_h = w_ukv.reshape(KV_LORA_RANK, MLA_HEADS, MLA_NOPE_DIM + MLA_V_DIM)
    w_uk_p = _pad_last(w_ukv_h[:, :, :MLA_NOPE_DIM], LANES).reshape(KV_LORA_RANK, MLA_PAD_WIDTH)
    w_uv = w_ukv_h[:, :, MLA_NOPE_DIM:].reshape(KV_LORA_RANK, MLA_WIDTH)
    w_ukv_p = jnp.concatenate([w_uk_p, w_uv], axis=1).astype(BF16)
    gq = _pad_last(q_head_norm[None, :], LANES)
    gk = _pad_last(k_head_norm[None, :], LANES)

    (q_sb, k_sb, v_sb, g_sb, q_m, k_m, v_m, g_m) = _inproj(
        x, mod, norm_w[None, :], w_in_p, q_lora_norm[None, :], w_uq_p, kv_lora_norm[None, :],
        w_ukv_p, gq, gk, *tables)

    mixed_sb = _sb_attention(q_sb, k_sb, v_sb, g_sb)
    mixed_mla = _mla_attention(q_m, k_m, v_m, g_m)
    return _outproj(mixed_sb, mixed_mla, x, mod, w_out.astype(BF16))


def _rope_tables(positions, dtype):
    half = MLA_ROPE_DIM // 2
    inv_freq = ROPE_THETA ** (-jnp.arange(0, MLA_ROPE_DIM, 2, dtype=F32) / MLA_ROPE_DIM)
    ang = positions.astype(F32)[..., None] * inv_freq
    cos = jnp.cos(ang).astype(dtype)
    sin = jnp.sin(ang).astype(dtype)
    B, S = positions.shape
    z = lambda n: jnp.zeros((B, S, n), dtype)
    tail = LANES - MLA_QK_DIM
    tc = jnp.concatenate([jnp.ones((B, S, MLA_NOPE_DIM), dtype), cos, cos, z(tail)], axis=-1)
    ts1 = jnp.concatenate([z(MLA_NOPE_DIM), -sin, z(half), z(tail)], axis=-1)
    ts2 = jnp.concatenate([z(MLA_NOPE_DIM), z(half), sin, z(tail)], axis=-1)
    return tc, ts1, ts2


def kernel(x, c, positions, w_ada, b_ada, norm_w, w_in, q_lora_norm, w_uq, kv_lora_norm, w_ukv,
           q_head_norm, k_head_norm, w_out):
    tables = _rope_tables(positions, x.dtype)
    for l in range(w_ada.shape[0]):
        x = _layer(x, c, tables, w_ada[l], b_ada[l], norm_w[l], w_in[l], q_lora_norm[l], w_uq[l],
                   kv_lora_norm[l], w_ukv[l], q_head_norm[l], k_head_norm[l], w_out[l])
    return x
```

```python
import functools
import math

import jax
import jax.numpy as jnp
from jax import lax
from jax.experimental import pallas as pl
from jax.experimental.pallas import tpu as pltpu

F32 = jnp.float32
BF16 = jnp.bfloat16

D_MODEL = 1024
SB_HEADS = 8
SB_HEAD_DIM = 64
SB_WIDTH = SB_HEADS * SB_HEAD_DIM
MLA_HEADS = 8
MLA_NOPE_DIM = 64
MLA_ROPE_DIM = 32
MLA_QK_DIM = MLA_NOPE_DIM + MLA_ROPE_DIM
MLA_V_DIM = 64
MLA_WIDTH = MLA_HEADS * MLA_V_DIM
Q_LORA_RANK = 384
KV_LORA_RANK = 256
ROPE_THETA = 10000.0
EPS = 1e-6

LANES = 128
HEAD_PAIR = 2
MLA_PAD_WIDTH = MLA_HEADS * LANES

_C_QSB = 0
_C_KSB = _C_QSB + SB_WIDTH
_C_VSB = _C_KSB + SB_WIDTH
_C_GSB = _C_VSB + SB_WIDTH
_C_CQ = _C_GSB + SB_WIDTH
_C_CKV = _C_CQ + Q_LORA_RANK
_C_KR = _C_CKV + KV_LORA_RANK
_C_GMLA = _C_KR + LANES
_C_END = _C_GMLA + MLA_WIDTH

ROW_TILE = 512
SB_TQ = 256
SB_TK = 256
SB_PAIRS = 4
MLA_TQ = 512
MLA_TK = 256
VMEM_LIMIT = 56 * 1024 * 1024

SB_DEAD_LOG2_DROP = 160.0
SB_Q_SCALE = math.log2(math.e) / math.sqrt(SB_HEAD_DIM)
MLA_Q_SCALE = math.log2(math.e) / math.sqrt(MLA_QK_DIM)


def _silu(g):
    return g * (1.0 / (1.0 + jnp.exp(-g)))


def _dot(a, b):
    return jnp.dot(a, b, preferred_element_type=F32)


def _dot_nt(a, b):
    return lax.dot_general(a, b, (((1,), (1,)), ((), ())), preferred_element_type=F32)


def _dot_tn(a, b):
    return lax.dot_general(a, b, (((0,), (0,)), ((), ())), preferred_element_type=F32)


def _adaln_kernel(c_ref, w_ref, b_ref, o_ref):
    o_ref[...] = jnp.dot(_silu(c_ref[...]), w_ref[...], preferred_element_type=F32,
                         precision=lax.Precision.HIGHEST) + b_ref[...]


def _adaln(c_pad, w_ada, b_ada):
    rows, d = c_pad.shape
    n = w_ada.shape[1]
    bn = 512
    return pl.pallas_call(
        _adaln_kernel,
        grid=(n // bn,),
        in_specs=[pl.BlockSpec((rows, d), lambda j: (0, 0)),
                  pl.BlockSpec((d, bn), lambda j: (0, j)),
                  pl.BlockSpec((1, bn), lambda j: (0, j))],
        out_specs=pl.BlockSpec((rows, bn), lambda j: (0, j)),
        out_shape=jax.ShapeDtypeStruct((rows, n), F32),
        name="adaln",
    )(c_pad, w_ada, b_ada)


def _rope_t(x1, x2, cos, sin):
    return x1 * cos - x2 * sin, x2 * cos + x1 * sin


def _inproj_kernel(x_ref, mod_ref, nw_ref, win_ref, qln_ref, wuqt_ref, kvln_ref, wukt_ref, wuv_ref,
                   gq_ref, gk_ref, cos_ref, sin_ref,
                   qsb_ref, ksb_ref, vsb_ref, gsb_ref, qmt_ref, km_ref, vm_ref, gm_ref):
    half = MLA_ROPE_DIM // 2
    n0, n1, n2 = MLA_NOPE_DIM, MLA_NOPE_DIM + half, MLA_QK_DIM
    x = x_ref[0]
    ts = x.shape[0]
    y = x * lax.rsqrt(jnp.mean(x * x, axis=-1, keepdims=True) + EPS) * nw_ref[...]
    h = (y * (1.0 + mod_ref[0, 1:2, :]) + mod_ref[0, 0:1, :]).astype(BF16)

    def proj(lo, hi):
        return _dot(h, win_ref[:, lo:hi])

    qsb_ref[0] = (proj(_C_QSB, _C_KSB) * SB_Q_SCALE).astype(BF16)
    ksb_ref[0] = proj(_C_KSB, _C_VSB).astype(BF16)
    vsb_ref[0] = proj(_C_VSB, _C_GSB).astype(BF16)
    gsb_ref[0] = proj(_C_GSB, _C_CQ)
    gm_ref[0] = proj(_C_GMLA, _C_END)

    cos, sin = cos_ref[0], sin_ref[0]
    pad_rows = jnp.zeros((LANES - MLA_QK_DIM, ts), F32)

    cq = proj(_C_CQ, _C_CKV)
    cq = cq * lax.rsqrt(jnp.mean(cq * cq, axis=-1, keepdims=True) + EPS) * qln_ref[...]
    q_t = _dot_nt(wuqt_ref[...], cq.astype(BF16))
    gq = gq_ref[...]
    for hd in range(MLA_HEADS):
        blk = q_t[hd * LANES:(hd + 1) * LANES]
        inv = lax.rsqrt(jnp.sum(blk * blk, axis=0, keepdims=True) * (1.0 / MLA_QK_DIM) + EPS)
        g = blk * gq
        r1, r2 = _rope_t(g[n0:n1], g[n1:n2], cos, sin)
        out = jnp.concatenate([g[:n0], r1, r2, pad_rows], axis=0) * (inv * MLA_Q_SCALE)
        qmt_ref[0, hd * LANES:(hd + 1) * LANES, :] = out.astype(BF16)

    ckv = proj(_C_CKV, _C_KR)
    ckv = (ckv * lax.rsqrt(jnp.mean(ckv * ckv, axis=-1, keepdims=True) + EPS) * kvln_ref[...]).astype(BF16)
    vm_ref[0] = _dot(ckv, wuv_ref[...]).astype(BF16)
    k_t = _dot_nt(wukt_ref[...], ckv)
    gk = gk_ref[...]
    kr_t = proj(_C_KR, _C_GMLA).T
    x1, x2 = kr_t[:half], kr_t[half:MLA_ROPE_DIM]
    kr_ssq = jnp.sum(x1 * x1 + x2 * x2, axis=0, keepdims=True)
    r1, r2 = _rope_t(x1 * gk[n0:n1], x2 * gk[n1:n2], cos, sin)
    for hd in range(MLA_HEADS):
        kn = k_t[hd * LANES:hd * LANES + n0]
        ssq = jnp.sum(kn * kn, axis=0, keepdims=True) + kr_ssq
        inv = lax.rsqrt(ssq * (1.0 / MLA_QK_DIM) + EPS)
        out = jnp.concatenate([kn * gk[:n0], r1, r2, pad_rows], axis=0) * inv
        km_ref[0, :, hd * LANES:(hd + 1) * LANES] = out.astype(BF16).T


def _inproj(x, mod, norm_w, w_in_p, qln, w_uq_t, kvln, w_uk_t, w_uv, gq, gk, cos_t, sin_t):
    B, S, D = x.shape
    ts = min(ROW_TILE, S)
    grid = (B, S // ts)

    def whole(a):
        return pl.BlockSpec(a.shape, lambda b, i: (0,) * a.ndim)

    def rows(width):
        return pl.BlockSpec((1, ts, width), lambda b, i: (b, i, 0))

    def cols(height):
        return pl.BlockSpec((1, height, ts), lambda b, i: (b, 0, i))

    out_shapes = (
        jax.ShapeDtypeStruct((B, S, SB_WIDTH), BF16),
        jax.ShapeDtypeStruct((B, S, SB_WIDTH), BF16),
        jax.ShapeDtypeStruct((B, S, SB_WIDTH), BF16),
        jax.ShapeDtypeStruct((B, S, SB_WIDTH), F32),
        jax.ShapeDtypeStruct((B, MLA_PAD_WIDTH, S), BF16),
        jax.ShapeDtypeStruct((B, S, MLA_PAD_WIDTH), BF16),
        jax.ShapeDtypeStruct((B, S, MLA_WIDTH), BF16),
        jax.ShapeDtypeStruct((B, S, MLA_WIDTH), F32),
    )
    out_specs = [rows(s.shape[-1]) for s in out_shapes]
    out_specs[4] = cols(MLA_PAD_WIDTH)
    half = MLA_ROPE_DIM // 2
    return pl.pallas_call(
        _inproj_kernel,
        grid=grid,
        in_specs=[rows(D),
                  pl.BlockSpec((1, 3, D), lambda b, i: (b, 0, 0)),
                  whole(norm_w), whole(w_in_p), whole(qln), whole(w_uq_t), whole(kvln),
                  whole(w_uk_t), whole(w_uv), whole(gq), whole(gk),
                  cols(half), cols(half)],
        out_specs=out_specs,
        out_shape=out_shapes,
        compiler_params=pltpu.CompilerParams(
            dimension_semantics=("arbitrary", "arbitrary"), vmem_limit_bytes=VMEM_LIMIT),
        name="inproj",
    )(x, mod, norm_w, w_in_p, qln, w_uq_t, kvln, w_uk_t, w_uv, gq, gk, cos_t, sin_t)


def _sb_kernel(q_ref, k_ref, v_ref, g_ref, o_ref, acc_ref, carry_ref):
    tq, tk = SB_TQ, SB_TK
    n_heads = SB_PAIRS * HEAD_PAIR
    i = pl.program_id(2)
    lane = lax.broadcasted_iota(jnp.int32, (tq, LANES), 1)
    q_heads = []
    for p in range(SB_PAIRS):
        q_pair = q_ref[0, :, p * LANES:(p + 1) * LANES]
        for hh in range(HEAD_PAIR):
            q_heads.append(jnp.where((lane // SB_HEAD_DIM) == hh, q_pair, jnp.zeros_like(q_pair)))

    strict = (lax.broadcasted_iota(jnp.int32, (tk, tq), 0)
              < lax.broadcasted_iota(jnp.int32, (tk, tq), 1))
    later = (lax.broadcasted_iota(jnp.int32, (tk, tk), 1)
             > lax.broadcasted_iota(jnp.int32, (tk, tk), 0)).astype(BF16)

    acc_ref[...] = jnp.zeros_like(acc_ref)
    carry_ref[...] = jnp.zeros_like(carry_ref)

    def visit(j, masked):
        start = pl.multiple_of(j * tk, tk)

        def pair_block(ref, h):
            p = h // HEAD_PAIR
            return ref[0, pl.ds(start, tk), p * LANES:(p + 1) * LANES]

        zs = [_dot_nt(pair_block(k_ref, h), q_heads[h]) for h in range(n_heads)]
        log_betas, afters, col_sums = [], [], []
        for h in range(n_heads):
            z = zs[h]
            relu = jnp.maximum(z, 0.0)
            low = z - relu
            log1p = jnp.log2(1.0 + jnp.exp2(low - relu))
            drop = relu + log1p
            if masked:
                drop = jnp.where(strict, drop, 0.0)
            hi = drop.astype(BF16)
            lo = (drop - hi.astype(F32)).astype(BF16)
            afters.append(_dot(later, hi) + _dot(later, lo))
            log_betas.append(low - log1p)
            col_sums.append(jnp.sum(drop, axis=0, keepdims=True))
        dead = None
        for h in range(n_heads):
            carry = carry_ref[h]
            w = jnp.exp2(log_betas[h] - (afters[h] + carry))
            if masked:
                w = jnp.where(strict, w, 0.0)
            pv = _dot_tn(pair_block(v_ref, h), w.astype(BF16))
            lo_row = (h % HEAD_PAIR) * SB_HEAD_DIM
            acc_ref[h] += pv[lo_row:lo_row + SB_HEAD_DIM, :]
            carry = carry + col_sums[h]
            carry_ref[h] = carry
            dead = carry if dead is None else jnp.minimum(dead, carry)
        return jnp.min(dead) < SB_DEAD_LOG2_DROP

    alive0 = visit(i, True)

    def cond(state):
        j, alive = state
        return jnp.logical_and(j >= 0, alive)

    def body(state):
        j, _ = state
        return j - 1, visit(j, False)

    lax.while_loop(cond, body, (i - 1, alive0))

    o_t = acc_ref[...].reshape(n_heads * SB_HEAD_DIM, tq)
    o_ref[0] = (o_t.T * _silu(g_ref[0])).astype(o_ref.dtype)


def _sb_attention(q, k, v, g):
    B, S, W = q.shape
    assert S % SB_TQ == 0 and SB_TQ == SB_TK
    wb = SB_PAIRS * LANES
    grid = (B, W // wb, S // SB_TQ)
    q_spec = pl.BlockSpec((1, SB_TQ, wb), lambda b, p, i: (b, i, p))
    kv_spec = pl.BlockSpec((1, S, wb), lambda b, p, i: (b, 0, p))
    return pl.pallas_call(
        _sb_kernel,
        grid=grid,
        in_specs=[q_spec, kv_spec, kv_spec, q_spec],
        out_specs=q_spec,
        out_shape=jax.ShapeDtypeStruct((B, S, W), BF16),
        scratch_shapes=[pltpu.VMEM((SB_PAIRS * HEAD_PAIR, SB_HEAD_DIM, SB_TQ), F32),
                        pltpu.VMEM((SB_PAIRS * HEAD_PAIR, 1, SB_TQ), F32)],
        compiler_params=pltpu.CompilerParams(
            dimension_semantics=("arbitrary", "arbitrary", "arbitrary"),
            vmem_limit_bytes=VMEM_LIMIT),
        name="sb_attn",
    )(q, k, v, g)


def _mla_kernel(q_ref, k_ref, v_ref, g_ref, o_ref, acc_ref, m_ref, l_ref, s_ref):
    tq, tk = MLA_TQ, MLA_TK
    diag_tiles = tq // tk
    i = pl.program_id(2)
    key_idx = lax.broadcasted_iota(jnp.int32, (tk, tq), 0)
    query_idx = lax.broadcasted_iota(jnp.int32, (tk, tq), 1)
    neg = jnp.finfo(F32).min

    acc_ref[...] = jnp.zeros_like(acc_ref)
    l_ref[...] = jnp.zeros_like(l_ref)
    m_ref[...] = jnp.full_like(m_ref, neg)

    def score(j, slot):
        start = pl.multiple_of(j * tk, tk)
        for hh in range(HEAD_PAIR):
            q_t = q_ref[0, hh * LANES:(hh + 1) * LANES, :]
            k_blk = k_ref[0, pl.ds(start, tk), hh * LANES:(hh + 1) * LANES]
            s_ref[slot, hh] = _dot(k_blk, q_t)

    def consume(j, slot, diag):
        start = pl.multiple_of(j * tk, tk)
        v_blk = v_ref[0, pl.ds(start, tk), :]
        for hh in range(HEAD_PAIR):
            s = s_ref[slot, hh]
            if diag is not None:
                s = jnp.where(key_idx + diag * tk <= query_idx, s, neg)
            m_old = m_ref[hh]
            m_new = jnp.maximum(m_old, jnp.max(s, axis=0, keepdims=True))
            alpha = jnp.exp2(m_old - m_new)
            p = jnp.exp2(s - m_new)
            l_ref[hh] = alpha * l_ref[hh] + jnp.sum(p, axis=0, keepdims=True)
            pv = _dot_tn(v_blk, p.astype(BF16))
            acc_ref[hh] = alpha * acc_ref[hh] + pv[hh * MLA_V_DIM:(hh + 1) * MLA_V_DIM, :]
            m_ref[hh] = m_new

    base = i * diag_tiles
    score(0, 0)

    def body(jj, carry):
        t = 2 * jj
        score(t + 1, 1)
        consume(t, 0, None)
        score(t + 2, 0)
        consume(t + 1, 1, None)
        return carry

    lax.fori_loop(0, base // 2, body, 0)
    for d in range(diag_tiles):
        if d + 1 < diag_tiles:
            score(base + d + 1, (d + 1) % 2)
        consume(base + d, d % 2, d)

    o_t = jnp.concatenate([acc_ref[hh] / l_ref[hh] for hh in range(HEAD_PAIR)], axis=0)
    o_ref[0] = (o_t.T * _silu(g_ref[0])).astype(o_ref.dtype)


def _mla_attention(q, k, v, g):
    B, S, W = v.shape
    tq = MLA_TQ
    assert S % tq == 0 and MLA_TQ % (2 * MLA_TK) == 0
    grid = (B, W // LANES, S // tq)
    return pl.pallas_call(
        _mla_kernel,
        grid=grid,
        in_specs=[pl.BlockSpec((1, HEAD_PAIR * LANES, tq), lambda b, p, i: (b, p, i)),
                  pl.BlockSpec((1, S, HEAD_PAIR * LANES), lambda b, p, i: (b, 0, p)),
                  pl.BlockSpec((1, S, LANES), lambda b, p, i: (b, 0, p)),
                  pl.BlockSpec((1, tq, LANES), lambda b, p, i: (b, i, p))],
        out_specs=pl.BlockSpec((1, tq, LANES), lambda b, p, i: (b, i, p)),
        out_shape=jax.ShapeDtypeStruct((B, S, W), BF16),
        scratch_shapes=[pltpu.VMEM((HEAD_PAIR, MLA_V_DIM, tq), F32),
                        pltpu.VMEM((HEAD_PAIR, 1, tq), F32),
                        pltpu.VMEM((HEAD_PAIR, 1, tq), F32),
                        pltpu.VMEM((2, HEAD_PAIR, MLA_TK, tq), F32)],
        compiler_params=pltpu.CompilerParams(
            dimension_semantics=("arbitrary", "arbitrary", "arbitrary"),
            vmem_limit_bytes=VMEM_LIMIT),
        name="mla_attn",
    )(q, k, v, g)


def _outproj_kernel(msb_ref, mmla_ref, x_ref, mod_ref, w_ref, o_ref):
    y = _dot(msb_ref[0], w_ref[:SB_WIDTH, :]) + _dot(mmla_ref[0], w_ref[SB_WIDTH:, :])
    o_ref[0] = x_ref[0] + mod_ref[0, 2:3, :] * y


def _outproj(mixed_sb, mixed_mla, x, mod, w_out):
    B, S, D = x.shape
    ts = min(ROW_TILE, S)

    def rows(width):
        return pl.BlockSpec((1, ts, width), lambda b, i: (b, i, 0))

    return pl.pallas_call(
        _outproj_kernel,
        grid=(B, S // ts),
        in_specs=[rows(SB_WIDTH), rows(MLA_WIDTH), rows(D),
                  pl.BlockSpec((1, 3, D), lambda b, i: (b, 0, 0)),
                  pl.BlockSpec(w_out.shape, lambda b, i: (0, 0))],
        out_specs=rows(D),
        out_shape=jax.ShapeDtypeStruct((B, S, D), x.dtype),
        compiler_params=pltpu.CompilerParams(
            dimension_semantics=("arbitrary", "arbitrary"), vmem_limit_bytes=VMEM_LIMIT),
        name="outproj",
    )(mixed_sb, mixed_mla, x, mod, w_out)


def _pad_last(a, width):
    return jnp.pad(a, [(0, 0)] * (a.ndim - 1) + [(0, width - a.shape[-1])])


def _layer(x, c, tables, w_ada, b_ada, norm_w, w_in, q_lora_norm, w_uq, kv_lora_norm, w_ukv,
           q_head_norm, k_head_norm, w_out):
    B, S, D = x.shape
    c_pad = jnp.pad(c, ((0, 8 - B), (0, 0)))
    ada = _adaln(c_pad, w_ada, b_ada[None, :])[:B]
    mod = ada.reshape(B, 3, D)

    ts = min(ROW_TILE, S)
    kr_lo = _C_KR
    w_in_p = jnp.concatenate(
        [w_in[:, :kr_lo + MLA_ROPE_DIM], jnp.zeros((D, LANES - MLA_ROPE_DIM), w_in.dtype),
         w_in[:, kr_lo + MLA_ROPE_DIM:]], axis=1).astype(BF16)
    w_uq_t = _pad_last(w_uq.reshape(Q_LORA_RANK, MLA_HEADS, MLA_QK_DIM), LANES)
    w_uq_t = w_uq_t.reshape(Q_LORA_RANK, MLA_PAD_WIDTH).T.astype(BF16)
    w_ukv_h = w_ukv.reshape(KV_LORA_RANK, MLA_HEADS, MLA_NOPE_DIM + MLA_V_DIM)
    w_uk_t = _pad_last(w_ukv_h[:, :, :MLA_NOPE_DIM], LANES)
    w_uk_t = w_uk_t.reshape(KV_LORA_RANK, MLA_PAD_WIDTH).T.astype(BF16)
    w_uv = w_ukv_h[:, :, MLA_NOPE_DIM:].reshape(KV_LORA_RANK, MLA_WIDTH).astype(BF16)
    gq = jnp.broadcast_to(_pad_last(q_head_norm, LANES)[:, None], (LANES, ts))
    gk = jnp.broadcast_to(_pad_last(k_head_norm, LANES)[:, None], (LANES, ts))

    (q_sb, k_sb, v_sb, g_sb, q_m, k_m, v_m, g_m) = _inproj(
        x, mod, norm_w[None, :], w_in_p, q_lora_norm[None, :], w_uq_t, kv_lora_norm[None, :],
        w_uk_t, w_uv, gq, gk, *tables)

    mixed_sb = _sb_attention(q_sb, k_sb, v_sb, g_sb)
    mixed_mla = _mla_attention(q_m, k_m, v_m, g_m)
    return _outproj(mixed_sb, mixed_mla, x, mod, w_out.astype(BF16))


def _rope_tables(positions, dtype):
    inv_freq = ROPE_THETA ** (-jnp.arange(0, MLA_ROPE_DIM, 2, dtype=F32) / MLA_ROPE_DIM)
    ang = inv_freq[None, :, None] * positions.astype(F32)[:, None, :]
    return jnp.cos(ang).astype(dtype), jnp.sin(ang).astype(dtype)


def kernel(x, c, positions, w_ada, b_ada, norm_w, w_in, q_lora_norm, w_uq, kv_lora_norm, w_ukv,
           q_head_norm, k_head_norm, w_out):
    tables = _rope_tables(positions, x.dtype)
    for l in range(w_ada.shape[0]):
        x = _layer(x, c, tables, w_ada[l], b_ada[l], norm_w[l], w_in[l], q_lora_norm[l], w_uq[l],
                   kv_lora_norm[l], w_ukv[l], q_head_norm[l], k_head_norm[l], w_out[l])
    return x
```

```python
import functools
import math

import jax
import jax.numpy as jnp
from jax import lax
from jax.experimental import pallas as pl
from jax.experimental.pallas import tpu as pltpu

F32 = jnp.float32
BF16 = jnp.bfloat16

D_MODEL = 1024
SB_HEADS = 8
SB_HEAD_DIM = 64
SB_WIDTH = SB_HEADS * SB_HEAD_DIM
MLA_HEADS = 8
MLA_NOPE_DIM = 64
MLA_ROPE_DIM = 32
MLA_QK_DIM = MLA_NOPE_DIM + MLA_ROPE_DIM
MLA_V_DIM = 64
MLA_WIDTH = MLA_HEADS * MLA_V_DIM
MLA_V_ONES = 16
MLA_V_ROWS = MLA_V_DIM + MLA_V_ONES
Q_LORA_RANK = 384
KV_LORA_RANK = 256
ROPE_THETA = 10000.0
EPS = 1e-6

LANES = 128
HEAD_PAIR = 2
MLA_PAD_WIDTH = MLA_HEADS * LANES

_C_QSB = 0
_C_KSB = _C_QSB + SB_WIDTH
_C_VSB = _C_KSB + SB_WIDTH
_C_GSB = _C_VSB + SB_WIDTH
_C_CQ = _C_GSB + SB_WIDTH
_C_CKV = _C_CQ + Q_LORA_RANK
_C_KR = _C_CKV + KV_LORA_RANK
_C_GMLA = _C_KR + LANES
_C_END = _C_GMLA + MLA_WIDTH

ROW_TILE = 512
SB_TQ = 256
SB_TK = 256
SB_PAIRS = 4
MLA_TQ = 512
MLA_TK = 256
VMEM_LIMIT = 56 * 1024 * 1024

SB_DEAD_LOG2_DROP = 160.0
SB_Q_SCALE = math.log2(math.e) / math.sqrt(SB_HEAD_DIM)
MLA_Q_SCALE = math.log2(math.e) / math.sqrt(MLA_QK_DIM)


def _silu(g):
    return g * (1.0 / (1.0 + jnp.exp(-g)))


def _dot(a, b):
    return jnp.dot(a, b, preferred_element_type=F32)


def _dot_nt(a, b):
    return lax.dot_general(a, b, (((1,), (1,)), ((), ())), preferred_element_type=F32)


def _dot_tn(a, b):
    return lax.dot_general(a, b, (((0,), (0,)), ((), ())), preferred_element_type=F32)


def _adaln_kernel(c_ref, w_ref, b_ref, o_ref):
    o_ref[...] = jnp.dot(_silu(c_ref[...]), w_ref[...], preferred_element_type=F32,
                         precision=lax.Precision.HIGHEST) + b_ref[...]


def _adaln(c_pad, w_ada, b_ada):
    rows, d = c_pad.shape
    n = w_ada.shape[1]
    bn = 512
    return pl.pallas_call(
        _adaln_kernel,
        grid=(n // bn,),
        in_specs=[pl.BlockSpec((rows, d), lambda j: (0, 0)),
                  pl.BlockSpec((d, bn), lambda j: (0, j)),
                  pl.BlockSpec((1, bn), lambda j: (0, j))],
        out_specs=pl.BlockSpec((rows, bn), lambda j: (0, j)),
        out_shape=jax.ShapeDtypeStruct((rows, n), F32),
        name="adaln",
    )(c_pad, w_ada, b_ada)


def _rope_t(x1, x2, cos, sin):
    return x1 * cos - x2 * sin, x2 * cos + x1 * sin


def _inproj_kernel(x_ref, mod_ref, nw_ref, win_ref, qln_ref, wuqt_ref, kvln_ref, wukt_ref, wuvt_ref,
                   gq_ref, gk_ref, cos_ref, sin_ref,
                   qsb_ref, ksb_ref, vsb_ref, gsb_ref, qmt_ref, km_ref, vmt_ref, gm_ref):
    half = MLA_ROPE_DIM // 2
    n0, n1, n2 = MLA_NOPE_DIM, MLA_NOPE_DIM + half, MLA_QK_DIM
    x = x_ref[0]
    ts = x.shape[0]
    y = x * lax.rsqrt(jnp.mean(x * x, axis=-1, keepdims=True) + EPS) * nw_ref[...]
    h = (y * (1.0 + mod_ref[0, 1:2, :]) + mod_ref[0, 0:1, :]).astype(BF16)

    def proj(lo, hi):
        return _dot(h, win_ref[:, lo:hi])

    qsb_ref[0] = (proj(_C_QSB, _C_KSB) * SB_Q_SCALE).astype(BF16)
    ksb_ref[0] = proj(_C_KSB, _C_VSB).astype(BF16)
    vsb_ref[0] = proj(_C_VSB, _C_GSB).astype(BF16)
    gsb_ref[0] = proj(_C_GSB, _C_CQ)
    gm_ref[0] = proj(_C_GMLA, _C_END)

    cos, sin = cos_ref[0], sin_ref[0]
    pad_rows = jnp.zeros((LANES - MLA_QK_DIM, ts), F32)

    cq = proj(_C_CQ, _C_CKV)
    cq = cq * lax.rsqrt(jnp.mean(cq * cq, axis=-1, keepdims=True) + EPS) * qln_ref[...]
    q_t = _dot_nt(wuqt_ref[...], cq.astype(BF16))
    gq = gq_ref[...]
    for hd in range(MLA_HEADS):
        blk = q_t[hd * LANES:(hd + 1) * LANES]
        inv = lax.rsqrt(jnp.sum(blk * blk, axis=0, keepdims=True) * (1.0 / MLA_QK_DIM) + EPS)
        g = blk * gq
        r1, r2 = _rope_t(g[n0:n1], g[n1:n2], cos, sin)
        out = jnp.concatenate([g[:n0], r1, r2, pad_rows], axis=0) * (inv * MLA_Q_SCALE)
        qmt_ref[0, hd * LANES:(hd + 1) * LANES, :] = out.astype(BF16)

    ckv = proj(_C_CKV, _C_KR)
    ckv = (ckv * lax.rsqrt(jnp.mean(ckv * ckv, axis=-1, keepdims=True) + EPS) * kvln_ref[...]).astype(BF16)
    v_t = _dot_nt(wuvt_ref[...], ckv)
    ones_rows = jnp.ones((MLA_V_ONES, ts), F32)
    v_rows = []
    for hd in range(MLA_HEADS):
        v_rows += [v_t[hd * MLA_V_DIM:(hd + 1) * MLA_V_DIM], ones_rows]
    vmt_ref[0] = jnp.concatenate(v_rows, axis=0).astype(BF16)
    k_t = _dot_nt(wukt_ref[...], ckv)
    gk = gk_ref[...]
    kr_t = proj(_C_KR, _C_GMLA).T
    x1, x2 = kr_t[:half], kr_t[half:MLA_ROPE_DIM]
    kr_ssq = jnp.sum(x1 * x1 + x2 * x2, axis=0, keepdims=True)
    r1, r2 = _rope_t(x1 * gk[n0:n1], x2 * gk[n1:n2], cos, sin)
    for hd in range(MLA_HEADS):
        kn = k_t[hd * LANES:hd * LANES + n0]
        ssq = jnp.sum(kn * kn, axis=0, keepdims=True) + kr_ssq
        inv = lax.rsqrt(ssq * (1.0 / MLA_QK_DIM) + EPS)
        out = jnp.concatenate([kn * gk[:n0], r1, r2, pad_rows], axis=0) * inv
        km_ref[0, :, hd * LANES:(hd + 1) * LANES] = out.astype(BF16).T


def _inproj(x, mod, norm_w, w_in_p, qln, w_uq_t, kvln, w_uk_t, w_uv, gq, gk, cos_t, sin_t):
    B, S, D = x.shape
    ts = min(ROW_TILE, S)
    grid = (B, S // ts)

    def whole(a):
        return pl.BlockSpec(a.shape, lambda b, i: (0,) * a.ndim)

    def rows(width):
        return pl.BlockSpec((1, ts, width), lambda b, i: (b, i, 0))

    def cols(height):
        return pl.BlockSpec((1, height, ts), lambda b, i: (b, 0, i))

    out_shapes = (
        jax.ShapeDtypeStruct((B, S, SB_WIDTH), BF16),
        jax.ShapeDtypeStruct((B, S, SB_WIDTH), BF16),
        jax.ShapeDtypeStruct((B, S, SB_WIDTH), BF16),
        jax.ShapeDtypeStruct((B, S, SB_WIDTH), F32),
        jax.ShapeDtypeStruct((B, MLA_PAD_WIDTH, S), BF16),
        jax.ShapeDtypeStruct((B, S, MLA_PAD_WIDTH), BF16),
        jax.ShapeDtypeStruct((B, MLA_HEADS * MLA_V_ROWS, S), BF16),
        jax.ShapeDtypeStruct((B, S, MLA_WIDTH), F32),
    )
    out_specs = [rows(s.shape[-1]) for s in out_shapes]
    out_specs[4] = cols(MLA_PAD_WIDTH)
    out_specs[6] = cols(MLA_HEADS * MLA_V_ROWS)
    half = MLA_ROPE_DIM // 2
    return pl.pallas_call(
        _inproj_kernel,
        grid=grid,
        in_specs=[rows(D),
                  pl.BlockSpec((1, 3, D), lambda b, i: (b, 0, 0)),
                  whole(norm_w), whole(w_in_p), whole(qln), whole(w_uq_t), whole(kvln),
                  whole(w_uk_t), whole(w_uv), whole(gq), whole(gk),
                  cols(half), cols(half)],
        out_specs=out_specs,
        out_shape=out_shapes,
        compiler_params=pltpu.CompilerParams(
            dimension_semantics=("arbitrary", "arbitrary"), vmem_limit_bytes=VMEM_LIMIT),
        name="inproj",
    )(x, mod, norm_w, w_in_p, qln, w_uq_t, kvln, w_uk_t, w_uv, gq, gk, cos_t, sin_t)


def _sb_kernel(q_ref, k_ref, v_ref, g_ref, o_ref, acc_ref, carry_ref):
    tq, tk = SB_TQ, SB_TK
    n_heads = SB_PAIRS * HEAD_PAIR
    i = pl.program_id(2)
    lane = lax.broadcasted_iota(jnp.int32, (tq, LANES), 1)
    q_heads = []
    for p in range(SB_PAIRS):
        q_pair = q_ref[0, :, p * LANES:(p + 1) * LANES]
        for hh in range(HEAD_PAIR):
            q_heads.append(jnp.where((lane // SB_HEAD_DIM) == hh, q_pair, jnp.zeros_like(q_pair)))

    strict = (lax.broadcasted_iota(jnp.int32, (tk, tq), 0)
              < lax.broadcasted_iota(jnp.int32, (tk, tq), 1))
    later = (lax.broadcasted_iota(jnp.int32, (tk, tk), 1)
             > lax.broadcasted_iota(jnp.int32, (tk, tk), 0)).astype(BF16)

    acc_ref[...] = jnp.zeros_like(acc_ref)
    carry_ref[...] = jnp.zeros_like(carry_ref)

    def visit(j, masked):
        start = pl.multiple_of(j * tk, tk)

        def pair_block(ref, h):
            p = h // HEAD_PAIR
            return ref[0, pl.ds(start, tk), p * LANES:(p + 1) * LANES]

        zs = [_dot_nt(pair_block(k_ref, h), q_heads[h]) for h in range(n_heads)]
        log_betas, afters, col_sums = [], [], []
        for h in range(n_heads):
            z = zs[h]
            relu = jnp.maximum(z, 0.0)
            low = z - relu
            log1p = jnp.log2(1.0 + jnp.exp2(low - relu))
            drop = relu + log1p
            if masked:
                drop = jnp.where(strict, drop, 0.0)
            hi = drop.astype(BF16)
            lo = (drop - hi.astype(F32)).astype(BF16)
            afters.append(_dot(later, hi) + _dot(later, lo))
            log_betas.append(low - log1p)
            col_sums.append(jnp.sum(drop, axis=0, keepdims=True))
        dead = None
        for h in range(n_heads):
            carry = carry_ref[h]
            w = jnp.exp2(log_betas[h] - (afters[h] + carry))
            if masked:
                w = jnp.where(strict, w, 0.0)
            pv = _dot_tn(pair_block(v_ref, h), w.astype(BF16))
            lo_row = (h % HEAD_PAIR) * SB_HEAD_DIM
            acc_ref[h] += pv[lo_row:lo_row + SB_HEAD_DIM, :]
            carry = carry + col_sums[h]
            carry_ref[h] = carry
            dead = carry if dead is None else jnp.minimum(dead, carry)
        return jnp.min(dead) < SB_DEAD_LOG2_DROP

    alive0 = visit(i, True)

    def cond(state):
        j, alive = state
        return jnp.logical_and(j >= 0, alive)

    def body(state):
        j, _ = state
        return j - 1, visit(j, False)

    lax.while_loop(cond, body, (i - 1, alive0))

    o_t = acc_ref[...].reshape(n_heads * SB_HEAD_DIM, tq)
    o_ref[0] = (o_t.T * _silu(g_ref[0])).astype(o_ref.dtype)


def _sb_attention(q, k, v, g):
    B, S, W = q.shape
    assert S % SB_TQ == 0 and SB_TQ == SB_TK
    wb = SB_PAIRS * LANES
    grid = (B, W // wb, S // SB_TQ)
    q_spec = pl.BlockSpec((1, SB_TQ, wb), lambda b, p, i: (b, i, p))
    kv_spec = pl.BlockSpec((1, S, wb), lambda b, p, i: (b, 0, p))
    return pl.pallas_call(
        _sb_kernel,
        grid=grid,
        in_specs=[q_spec, kv_spec, kv_spec, q_spec],
        out_specs=q_spec,
        out_shape=jax.ShapeDtypeStruct((B, S, W), BF16),
        scratch_shapes=[pltpu.VMEM((SB_PAIRS * HEAD_PAIR, SB_HEAD_DIM, SB_TQ), F32),
                        pltpu.VMEM((SB_PAIRS * HEAD_PAIR, 1, SB_TQ), F32)],
        compiler_params=pltpu.CompilerParams(
            dimension_semantics=("arbitrary", "arbitrary", "arbitrary"),
            vmem_limit_bytes=VMEM_LIMIT),
        name="sb_attn",
    )(q, k, v, g)


def _mla_kernel(q_ref, k_ref, v_ref, g_ref, o_ref, acc_ref, m_ref, s_ref):
    tq, tk = MLA_TQ, MLA_TK
    diag_tiles = tq // tk
    i = pl.program_id(2)
    neg = jnp.finfo(F32).min

    acc_ref[...] = jnp.zeros_like(acc_ref)
    m_ref[...] = jnp.full_like(m_ref, neg)

    def score(j, slot, diag=None):
        start = pl.multiple_of(j * tk, tk)
        lo = 0 if diag is None else diag * tk
        for hh in range(HEAD_PAIR):
            q_t = q_ref[0, hh * LANES:(hh + 1) * LANES, lo:]
            k_blk = k_ref[0, pl.ds(start, tk), hh * LANES:(hh + 1) * LANES]
            s_ref[slot, hh, :, lo:] = _dot(k_blk, q_t)

    def consume(j, slot, diag=None):
        start = pl.multiple_of(j * tk, tk)
        lo = 0 if diag is None else diag * tk
        for hh in range(HEAD_PAIR):
            v_t = v_ref[0, hh * MLA_V_ROWS:(hh + 1) * MLA_V_ROWS, pl.ds(start, tk)]
            s = s_ref[slot, hh, :, lo:]
            if diag is not None:
                causal = (lax.broadcasted_iota(jnp.int32, s.shape, 0)
                          <= lax.broadcasted_iota(jnp.int32, s.shape, 1))
                s = jnp.where(causal, s, neg)
            m_old = m_ref[hh, :, lo:]
            m_new = jnp.maximum(m_old, jnp.max(s, axis=0, keepdims=True))
            alpha = jnp.exp2(m_old[:1] - m_new[:1])
            p = jnp.exp2(s - m_new[:1])
            acc_ref[hh, :, lo:] = alpha * acc_ref[hh, :, lo:] + _dot(v_t, p.astype(BF16))
            m_ref[hh, :, lo:] = m_new

    base = i * diag_tiles
    score(0, 0)

    def two_tiles(t):
        score(t + 1, 1)
        consume(t, 0)
        score(t + 2, 0)
        consume(t + 1, 1)

    def four_tiles(jj, carry):
        two_tiles(4 * jj)
        two_tiles(4 * jj + 2)
        return carry

    n_four = base // 4
    lax.fori_loop(0, n_four, four_tiles, 0)

    def leftover(_, carry):
        two_tiles(4 * n_four)
        return carry

    lax.fori_loop(0, (base - 4 * n_four) // 2, leftover, 0)
    for d in range(diag_tiles):
        if d + 1 < diag_tiles:
            score(base + d + 1, (d + 1) % 2, d + 1)
        consume(base + d, d % 2, d)

    o_t = jnp.concatenate(
        [acc_ref[hh, :MLA_V_DIM] / acc_ref[hh, MLA_V_DIM:MLA_V_DIM + 1] for hh in range(HEAD_PAIR)],
        axis=0)
    o_ref[0] = (o_t.T * _silu(g_ref[0])).astype(o_ref.dtype)


def _mla_attention(q, k, v, g):
    B, S, W = g.shape
    tq = MLA_TQ
    pair_rows = HEAD_PAIR * MLA_V_ROWS
    assert S % tq == 0 and MLA_TQ % (2 * MLA_TK) == 0
    grid = (B, W // LANES, S // tq)
    return pl.pallas_call(
        _mla_kernel,
        grid=grid,
        in_specs=[pl.BlockSpec((1, HEAD_PAIR * LANES, tq), lambda b, p, i: (b, p, i)),
                  pl.BlockSpec((1, S, HEAD_PAIR * LANES), lambda b, p, i: (b, 0, p)),
                  pl.BlockSpec((1, pair_rows, S), lambda b, p, i: (b, p, 0)),
                  pl.BlockSpec((1, tq, LANES), lambda b, p, i: (b, i, p))],
        out_specs=pl.BlockSpec((1, tq, LANES), lambda b, p, i: (b, i, p)),
        out_shape=jax.ShapeDtypeStruct((B, S, W), BF16),
        scratch_shapes=[pltpu.VMEM((HEAD_PAIR, MLA_V_ROWS, tq), F32),
                        pltpu.VMEM((HEAD_PAIR, 8, tq), F32),
                        pltpu.VMEM((2, HEAD_PAIR, MLA_TK, tq), F32)],
        compiler_params=pltpu.CompilerParams(
            dimension_semantics=("arbitrary", "arbitrary", "arbitrary"),
            vmem_limit_bytes=VMEM_LIMIT),
        name="mla_attn",
    )(q, k, v, g)


def _outproj_kernel(msb_ref, mmla_ref, x_ref, mod_ref, w_ref, o_ref):
    y = _dot(msb_ref[0], w_ref[:SB_WIDTH, :]) + _dot(mmla_ref[0], w_ref[SB_WIDTH:, :])
    o_ref[0] = x_ref[0] + mod_ref[0, 2:3, :] * y


def _outproj(mixed_sb, mixed_mla, x, mod, w_out):
    B, S, D = x.shape
    ts = min(ROW_TILE, S)

    def rows(width):
        return pl.BlockSpec((1, ts, width), lambda b, i: (b, i, 0))

    return pl.pallas_call(
        _outproj_kernel,
        grid=(B, S // ts),
        in_specs=[rows(SB_WIDTH), rows(MLA_WIDTH), rows(D),
                  pl.BlockSpec((1, 3, D), lambda b, i: (b, 0, 0)),
                  pl.BlockSpec(w_out.shape, lambda b, i: (0, 0))],
        out_specs=rows(D),
        out_shape=jax.ShapeDtypeStruct((B, S, D), x.dtype),
        compiler_params=pltpu.CompilerParams(
            dimension_semantics=("arbitrary", "arbitrary"), vmem_limit_bytes=VMEM_LIMIT),
        name="outproj",
    )(mixed_sb, mixed_mla, x, mod, w_out)


def _pad_last(a, width):
    return jnp.pad(a, [(0, 0)] * (a.ndim - 1) + [(0, width - a.shape[-1])])


def _layer(x, c, tables, w_ada, b_ada, norm_w, w_in, q_lora_norm, w_uq, kv_lora_norm, w_ukv,
           q_head_norm, k_head_norm, w_out):
    B, S, D = x.shape
    c_pad = jnp.pad(c, ((0, 8 - B), (0, 0)))
    ada = _adaln(c_pad, w_ada, b_ada[None, :])[:B]
    mod = ada.reshape(B, 3, D)

    ts = min(ROW_TILE, S)
    kr_lo = _C_KR
    w_in_p = jnp.concatenate(
        [w_in[:, :kr_lo + MLA_ROPE_DIM], jnp.zeros((D, LANES - MLA_ROPE_DIM), w_in.dtype),
         w_in[:, kr_lo + MLA_ROPE_DIM:]], axis=1).astype(BF16)
    w_uq_t = _pad_last(w_uq.reshape(Q_LORA_RANK, MLA_HEADS, MLA_QK_DIM), LANES)
    w_uq_t = w_uq_t.reshape(Q_LORA_RANK, MLA_PAD_WIDTH).T.astype(BF16)
    w_ukv_h = w_ukv.reshape(KV_LORA_RANK, MLA_HEADS, MLA_NOPE_DIM + MLA_V_DIM)
    w_uk_t = _pad_last(w_ukv_h[:, :, :MLA_NOPE_DIM], LANES)
    w_uk_t = w_uk_t.reshape(KV_LORA_RANK, MLA_PAD_WIDTH).T.astype(BF16)
    w_uv = w_ukv_h[:, :, MLA_NOPE_DIM:].reshape(KV_LORA_RANK, MLA_WIDTH).T.astype(BF16)
    gq = jnp.broadcast_to(_pad_last(q_head_norm, LANES)[:, None], (LANES, ts))
    gk = jnp.broadcast_to(_pad_last(k_head_norm, LANES)[:, None], (LANES, ts))

    (q_sb, k_sb, v_sb, g_sb, q_m, k_m, v_m, g_m) = _inproj(
        x, mod, norm_w[None, :], w_in_p, q_lora_norm[None, :], w_uq_t, kv_lora_norm[None, :],
        w_uk_t, w_uv, gq, gk, *tables)

    mixed_sb = _sb_attention(q_sb, k_sb, v_sb, g_sb)
    mixed_mla = _mla_attention(q_m, k_m, v_m, g_m)
    return _outproj(mixed_sb, mixed_mla, x, mod, w_out.astype(BF16))


def _rope_tables(positions, dtype):
    inv_freq = ROPE_THETA ** (-jnp.arange(0, MLA_ROPE_DIM, 2, dtype=F32) / MLA_ROPE_DIM)
    ang = inv_freq[None, :, None] * positions.astype(F32)[:, None, :]
    return jnp.cos(ang).astype(dtype), jnp.sin(ang).astype(dtype)


def kernel(x, c, positions, w_ada, b_ada, norm_w, w_in, q_lora_norm, w_uq, kv_lora_norm, w_ukv,
           q_head_norm, k_head_norm, w_out):
    tables = _rope_tables(positions, x.dtype)
    for l in range(w_ada.shape[0]):
        x = _layer(x, c, tables, w_ada[l], b_ada[l], norm_w[l], w_in[l], q_lora_norm[l], w_uq[l],
                   kv_lora_norm[l], w_ukv[l], q_head_norm[l], k_head_norm[l], w_out[l])
    return x
```

```python
import functools
import math

import jax
import jax.numpy as jnp
from jax import lax
from jax.experimental import pallas as pl
from jax.experimental.pallas import tpu as pltpu

F32 = jnp.float32
BF16 = jnp.bfloat16

D_MODEL = 1024
SB_HEADS = 8
SB_HEAD_DIM = 64
SB_WIDTH = SB_HEADS * SB_HEAD_DIM
MLA_HEADS = 8
MLA_NOPE_DIM = 64
MLA_ROPE_DIM = 32
MLA_QK_DIM = MLA_NOPE_DIM + MLA_ROPE_DIM
MLA_V_DIM = 64
MLA_WIDTH = MLA_HEADS * MLA_V_DIM
MLA_V_ONES = 16
MLA_V_ROWS = MLA_V_DIM + MLA_V_ONES
Q_LORA_RANK = 384
KV_LORA_RANK = 256
ROPE_THETA = 10000.0
EPS = 1e-6

LANES = 128
HEAD_PAIR = 2
MLA_PAD_WIDTH = MLA_HEADS * LANES

_C_QSB = 0
_C_KSB = _C_QSB + SB_WIDTH
_C_VSB = _C_KSB + SB_WIDTH
_C_GSB = _C_VSB + SB_WIDTH
_C_CQ = _C_GSB + SB_WIDTH
_C_CKV = _C_CQ + Q_LORA_RANK
_C_KR = _C_CKV + KV_LORA_RANK
_C_GMLA = _C_KR + MLA_ROPE_DIM

ROW_TILE = 512
OUT_ROW_TILE = 1024
SB_TQ = 256
SB_TK = 256
SB_PAIRS = 4
MLA_TQ = 512
MLA_TK = 256
VMEM_LIMIT = 56 * 1024 * 1024

SB_DEAD_LOG2_DROP = 160.0
SB_Q_SCALE = math.log2(math.e) / math.sqrt(SB_HEAD_DIM)
MLA_Q_SCALE = math.log2(math.e) / math.sqrt(MLA_QK_DIM)


def _silu(g):
    return g * (1.0 / (1.0 + jnp.exp(-g)))


def _dot(a, b):
    return jnp.dot(a, b, preferred_element_type=F32)


def _dot_nt(a, b):
    return lax.dot_general(a, b, (((1,), (1,)), ((), ())), preferred_element_type=F32)


def _dot_tn(a, b):
    return lax.dot_general(a, b, (((0,), (0,)), ((), ())), preferred_element_type=F32)


def _adaln_kernel(c_ref, w_ref, b_ref, o_ref):
    o_ref[...] = jnp.dot(_silu(c_ref[...]), w_ref[...], preferred_element_type=F32,
                         precision=lax.Precision.HIGHEST) + b_ref[...]


def _adaln(c_pad, w_ada, b_ada):
    rows, d = c_pad.shape
    n = w_ada.shape[1]
    bn = 512
    return pl.pallas_call(
        _adaln_kernel,
        grid=(n // bn,),
        in_specs=[pl.BlockSpec((rows, d), lambda j: (0, 0)),
                  pl.BlockSpec((d, bn), lambda j: (0, j)),
                  pl.BlockSpec((1, bn), lambda j: (0, j))],
        out_specs=pl.BlockSpec((rows, bn), lambda j: (0, j)),
        out_shape=jax.ShapeDtypeStruct((rows, n), F32),
        name="adaln",
    )(c_pad, w_ada, b_ada)


def _rope_t(x1, x2, cos, sin):
    return x1 * cos - x2 * sin, x2 * cos + x1 * sin


def _inproj_kernel(x_ref, mod_ref, nw_ref, win_ref, wgm_ref, qln_ref, wuqt_ref, kvln_ref, wukt_ref, wuvt_ref,
                   gq_ref, gk_ref, cos_ref, sin_ref,
                   qsb_ref, ksb_ref, vsb_ref, gsb_ref, qmt_ref, km_ref, vmt_ref, gm_ref):
    half = MLA_ROPE_DIM // 2
    n0, n1, n2 = MLA_NOPE_DIM, MLA_NOPE_DIM + half, MLA_QK_DIM
    x = x_ref[0]
    ts = x.shape[0]
    y = x * lax.rsqrt(jnp.mean(x * x, axis=-1, keepdims=True) + EPS) * nw_ref[...]
    h = (y * (1.0 + mod_ref[0, 1:2, :]) + mod_ref[0, 0:1, :]).astype(BF16)

    def proj(lo, hi):
        return _dot(h, win_ref[:, lo:hi])

    def rms(t, w_ref):
        return t * lax.rsqrt(jnp.mean(t * t, axis=-1, keepdims=True) + EPS) * w_ref[...]

    cq = proj(_C_CQ, _C_CKV)
    ckv = proj(_C_CKV, _C_KR)
    kr = proj(_C_KR, _C_KR + LANES)
    qsb_ref[0] = (proj(_C_QSB, _C_KSB) * SB_Q_SCALE).astype(BF16)
    ksb_ref[0] = proj(_C_KSB, _C_VSB).astype(BF16)

    cqn = rms(cq, qln_ref).astype(BF16)
    ckvn = rms(ckv, kvln_ref).astype(BF16)
    q_t = _dot_nt(wuqt_ref[...], cqn)
    k_t = _dot_nt(wukt_ref[...], ckvn)
    v_t = _dot_nt(wuvt_ref[...], ckvn)
    vsb_ref[0] = proj(_C_VSB, _C_GSB).astype(BF16)
    gsb_ref[0] = proj(_C_GSB, _C_CQ)
    gm_ref[0] = _dot(h, wgm_ref[...])

    cos, sin = cos_ref[0], sin_ref[0]
    pad_rows = jnp.zeros((LANES - MLA_QK_DIM, ts), F32)

    gq = gq_ref[...]
    for hd in range(MLA_HEADS):
        blk = q_t[hd * LANES:(hd + 1) * LANES]
        inv = lax.rsqrt(jnp.sum(blk * blk, axis=0, keepdims=True) * (1.0 / MLA_QK_DIM) + EPS)
        g = blk * gq
        r1, r2 = _rope_t(g[n0:n1], g[n1:n2], cos, sin)
        out = jnp.concatenate([g[:n0], r1, r2, pad_rows], axis=0) * (inv * MLA_Q_SCALE)
        qmt_ref[0, hd * LANES:(hd + 1) * LANES, :] = out.astype(BF16)

    ones_rows = jnp.ones((MLA_V_ONES, ts), F32)
    v_rows = []
    for hd in range(MLA_HEADS):
        v_rows += [v_t[hd * MLA_V_DIM:(hd + 1) * MLA_V_DIM], ones_rows]
    vmt_ref[0] = jnp.concatenate(v_rows, axis=0).astype(BF16)

    gk = gk_ref[...]
    kr_t = kr.T
    x1, x2 = kr_t[:half], kr_t[half:MLA_ROPE_DIM]
    kr_ssq = jnp.sum(x1 * x1 + x2 * x2, axis=0, keepdims=True)
    r1, r2 = _rope_t(x1 * gk[n0:n1], x2 * gk[n1:n2], cos, sin)
    for hd in range(MLA_HEADS):
        kn = k_t[hd * LANES:hd * LANES + n0]
        ssq = jnp.sum(kn * kn, axis=0, keepdims=True) + kr_ssq
        inv = lax.rsqrt(ssq * (1.0 / MLA_QK_DIM) + EPS)
        out = jnp.concatenate([kn * gk[:n0], r1, r2, pad_rows], axis=0) * inv
        km_ref[0, :, hd * LANES:(hd + 1) * LANES] = out.astype(BF16).T


def _inproj(x, mod, norm_w, w_in_b, w_gm, qln, w_uq_t, kvln, w_uk_t, w_uv, gq, gk, cos_t, sin_t):
    B, S, D = x.shape
    ts = min(ROW_TILE, S)
    grid = (B, S // ts)

    def whole(a):
        return pl.BlockSpec(a.shape, lambda b, i: (0,) * a.ndim)

    def rows(width):
        return pl.BlockSpec((1, ts, width), lambda b, i: (b, i, 0))

    def cols(height):
        return pl.BlockSpec((1, height, ts), lambda b, i: (b, 0, i))

    out_shapes = (
        jax.ShapeDtypeStruct((B, S, SB_WIDTH), BF16),
        jax.ShapeDtypeStruct((B, S, SB_WIDTH), BF16),
        jax.ShapeDtypeStruct((B, S, SB_WIDTH), BF16),
        jax.ShapeDtypeStruct((B, S, SB_WIDTH), F32),
        jax.ShapeDtypeStruct((B, MLA_PAD_WIDTH, S), BF16),
        jax.ShapeDtypeStruct((B, S, MLA_PAD_WIDTH), BF16),
        jax.ShapeDtypeStruct((B, MLA_HEADS * MLA_V_ROWS, S), BF16),
        jax.ShapeDtypeStruct((B, S, MLA_WIDTH), F32),
    )
    out_specs = [rows(s.shape[-1]) for s in out_shapes]
    out_specs[4] = cols(MLA_PAD_WIDTH)
    out_specs[6] = cols(MLA_HEADS * MLA_V_ROWS)
    half = MLA_ROPE_DIM // 2
    return pl.pallas_call(
        _inproj_kernel,
        grid=grid,
        in_specs=[rows(D),
                  pl.BlockSpec((1, 3, D), lambda b, i: (b, 0, 0)),
                  whole(norm_w), whole(w_in_b), whole(w_gm), whole(qln), whole(w_uq_t), whole(kvln),
                  whole(w_uk_t), whole(w_uv), whole(gq), whole(gk),
                  cols(half), cols(half)],
        out_specs=out_specs,
        out_shape=out_shapes,
        compiler_params=pltpu.CompilerParams(
            dimension_semantics=("arbitrary", "arbitrary"), vmem_limit_bytes=VMEM_LIMIT),
        name="inproj",
    )(x, mod, norm_w, w_in_b, w_gm, qln, w_uq_t, kvln, w_uk_t, w_uv, gq, gk, cos_t, sin_t)


def _sb_kernel(q_ref, k_ref, v_ref, g_ref, o_ref, acc_ref, carry_ref):
    tq, tk = SB_TQ, SB_TK
    n_heads = SB_PAIRS * HEAD_PAIR
    i = pl.program_id(2)
    lane = lax.broadcasted_iota(jnp.int32, (tq, LANES), 1)
    q_heads = []
    for p in range(SB_PAIRS):
        q_pair = q_ref[0, :, p * LANES:(p + 1) * LANES]
        for hh in range(HEAD_PAIR):
            q_heads.append(jnp.where((lane // SB_HEAD_DIM) == hh, q_pair, jnp.zeros_like(q_pair)))

    strict = (lax.broadcasted_iota(jnp.int32, (tk, tq), 0)
              < lax.broadcasted_iota(jnp.int32, (tk, tq), 1))
    l_row = lax.broadcasted_iota(jnp.int32, (tk + 8, tk), 0)
    l_col = lax.broadcasted_iota(jnp.int32, (tk + 8, tk), 1)
    later = jnp.where(jnp.logical_or(l_col > l_row, l_row >= tk), 1.0, 0.0).astype(F32)
    sign_bit = jnp.uint32(0x80000000)
    bf16_bits = jnp.uint32(0xFFFF0000)

    acc_ref[...] = jnp.zeros_like(acc_ref)
    carry_ref[...] = jnp.zeros_like(carry_ref)

    def visit(j, masked):
        start = pl.multiple_of(j * tk, tk)

        def pair_block(ref, h):
            p = h // HEAD_PAIR
            return ref[0, pl.ds(start, tk), p * LANES:(p + 1) * LANES]

        zs = [_dot_nt(pair_block(k_ref, h), q_heads[h]) for h in range(n_heads)]
        log_betas, afters, col_sums = [], [], []
        for h in range(n_heads):
            z = zs[h]
            neg_abs = lax.bitcast_convert_type(lax.bitcast_convert_type(z, jnp.uint32) | sign_bit, F32)
            drop = jnp.maximum(z, 0.0) + jnp.log2(1.0 + jnp.exp2(neg_abs))
            if masked:
                drop = jnp.where(strict, drop, 0.0)
            hi = lax.bitcast_convert_type(lax.bitcast_convert_type(drop, jnp.uint32) & bf16_bits, F32)
            sums = _dot(later, hi) + _dot(later, drop - hi)
            afters.append(sums[:tk])
            col_sums.append(sums[tk:tk + 1])
            log_betas.append(z - drop)
        dead = None
        for h in range(n_heads):
            carry = carry_ref[h]
            w = jnp.exp2(log_betas[h] - (afters[h] + carry))
            if masked:
                w = jnp.where(strict, w, 0.0)
            pv = _dot_tn(pair_block(v_ref, h), w.astype(BF16))
            lo_row = (h % HEAD_PAIR) * SB_HEAD_DIM
            acc_ref[h] += pv[lo_row:lo_row + SB_HEAD_DIM, :]
            carry = carry + col_sums[h]
            carry_ref[h] = carry
            dead = carry if dead is None else jnp.minimum(dead, carry)
        return jnp.min(dead) < SB_DEAD_LOG2_DROP

    alive0 = visit(i, True)

    def cond(state):
        j, alive = state
        return jnp.logical_and(j >= 0, alive)

    def body(state):
        j, _ = state
        return j - 1, visit(j, False)

    lax.while_loop(cond, body, (i - 1, alive0))

    o_t = acc_ref[...].reshape(n_heads * SB_HEAD_DIM, tq)
    o_ref[0] = (o_t.T * _silu(g_ref[0])).astype(o_ref.dtype)


def _sb_attention(q, k, v, g):
    B, S, W = q.shape
    assert S % SB_TQ == 0 and SB_TQ == SB_TK
    wb = SB_PAIRS * LANES
    grid = (B, W // wb, S // SB_TQ)
    q_spec = pl.BlockSpec((1, SB_TQ, wb), lambda b, p, i: (b, i, p))
    kv_spec = pl.BlockSpec((1, S, wb), lambda b, p, i: (b, 0, p))
    return pl.pallas_call(
        _sb_kernel,
        grid=grid,
        in_specs=[q_spec, kv_spec, kv_spec, q_spec],
        out_specs=q_spec,
        out_shape=jax.ShapeDtypeStruct((B, S, W), BF16),
        scratch_shapes=[pltpu.VMEM((SB_PAIRS * HEAD_PAIR, SB_HEAD_DIM, SB_TQ), F32),
                        pltpu.VMEM((SB_PAIRS * HEAD_PAIR, 1, SB_TQ), F32)],
        compiler_params=pltpu.CompilerParams(
            dimension_semantics=("arbitrary", "arbitrary", "arbitrary"),
            vmem_limit_bytes=VMEM_LIMIT),
        name="sb_attn",
    )(q, k, v, g)


def _mla_kernel(q_ref, k_ref, v_ref, g_ref, o_ref, acc_ref, m_ref, s_ref):
    tq, tk = MLA_TQ, MLA_TK
    diag_tiles = tq // tk
    i = pl.program_id(2)
    neg = jnp.finfo(F32).min

    acc_ref[...] = jnp.zeros_like(acc_ref)
    m_ref[...] = jnp.full_like(m_ref, neg)

    def score(j, slot, diag=None):
        start = pl.multiple_of(j * tk, tk)
        lo = 0 if diag is None else diag * tk
        for hh in range(HEAD_PAIR):
            q_t = q_ref[0, hh * LANES:(hh + 1) * LANES, lo:]
            k_blk = k_ref[0, pl.ds(start, tk), hh * LANES:(hh + 1) * LANES]
            s_ref[slot, hh, :, lo:] = _dot(k_blk, q_t)

    def consume(j, slot, diag=None):
        start = pl.multiple_of(j * tk, tk)
        lo = 0 if diag is None else diag * tk
        for hh in range(HEAD_PAIR):
            v_t = v_ref[0, hh * MLA_V_ROWS:(hh + 1) * MLA_V_ROWS, pl.ds(start, tk)]
            s = s_ref[slot, hh, :, lo:]
            if diag is not None:
                causal = (lax.broadcasted_iota(jnp.int32, s.shape, 0)
                          <= lax.broadcasted_iota(jnp.int32, s.shape, 1))
                s = jnp.where(causal, s, neg)
            m_old = m_ref[hh, :, lo:]
            m_new = jnp.maximum(m_old, jnp.max(s, axis=0, keepdims=True))
            alpha = jnp.exp2(m_old[:1] - m_new[:1])
            p = jnp.exp2(s - m_new[:1])
            acc_ref[hh, :, lo:] = alpha * acc_ref[hh, :, lo:] + _dot(v_t, p.astype(BF16))
            m_ref[hh, :, lo:] = m_new

    base = i * diag_tiles
    score(0, 0)

    def two_tiles(t):
        score(t + 1, 1)
        consume(t, 0)
        score(t + 2, 0)
        consume(t + 1, 1)

    def four_tiles(jj, carry):
        two_tiles(4 * jj)
        two_tiles(4 * jj + 2)
        return carry

    n_four = base // 4
    lax.fori_loop(0, n_four, four_tiles, 0)

    def leftover(_, carry):
        two_tiles(4 * n_four)
        return carry

    lax.fori_loop(0, (base - 4 * n_four) // 2, leftover, 0)
    for d in range(diag_tiles):
        if d + 1 < diag_tiles:
            score(base + d + 1, (d + 1) % 2, d + 1)
        consume(base + d, d % 2, d)

    o_t = jnp.concatenate(
        [acc_ref[hh, :MLA_V_DIM] / acc_ref[hh, MLA_V_DIM:MLA_V_DIM + 1] for hh in range(HEAD_PAIR)],
        axis=0)
    o_ref[0] = (o_t.T * _silu(g_ref[0])).astype(o_ref.dtype)


def _mla_attention(q, k, v, g):
    B, S, W = g.shape
    tq = MLA_TQ
    pair_rows = HEAD_PAIR * MLA_V_ROWS
    assert S % tq == 0 and MLA_TQ % (2 * MLA_TK) == 0
    grid = (B, W // LANES, S // tq)
    return pl.pallas_call(
        _mla_kernel,
        grid=grid,
        in_specs=[pl.BlockSpec((1, HEAD_PAIR * LANES, tq), lambda b, p, i: (b, p, i)),
                  pl.BlockSpec((1, S, HEAD_PAIR * LANES), lambda b, p, i: (b, 0, p)),
                  pl.BlockSpec((1, pair_rows, S), lambda b, p, i: (b, p, 0)),
                  pl.BlockSpec((1, tq, LANES), lambda b, p, i: (b, i, p))],
        out_specs=pl.BlockSpec((1, tq, LANES), lambda b, p, i: (b, i, p)),
        out_shape=jax.ShapeDtypeStruct((B, S, W), BF16),
        scratch_shapes=[pltpu.VMEM((HEAD_PAIR, MLA_V_ROWS, tq), F32),
                        pltpu.VMEM((HEAD_PAIR, 8, tq), F32),
                        pltpu.VMEM((2, HEAD_PAIR, MLA_TK, tq), F32)],
        compiler_params=pltpu.CompilerParams(
            dimension_semantics=("arbitrary", "arbitrary", "arbitrary"),
            vmem_limit_bytes=VMEM_LIMIT),
        name="mla_attn",
    )(q, k, v, g)


def _outproj_kernel(msb_ref, mmla_ref, x_ref, mod_ref, w_ref, o_ref):
    y = _dot(msb_ref[0], w_ref[:SB_WIDTH, :]) + _dot(mmla_ref[0], w_ref[SB_WIDTH:, :])
    o_ref[0] = x_ref[0] + mod_ref[0, 2:3, :] * y


def _outproj(mixed_sb, mixed_mla, x, mod, w_out):
    B, S, D = x.shape
    ts = min(OUT_ROW_TILE, S)

    def rows(width):
        return pl.BlockSpec((1, ts, width), lambda b, i: (b, i, 0))

    return pl.pallas_call(
        _outproj_kernel,
        grid=(B, S // ts),
        in_specs=[rows(SB_WIDTH), rows(MLA_WIDTH), rows(D),
                  pl.BlockSpec((1, 3, D), lambda b, i: (b, 0, 0)),
                  pl.BlockSpec(w_out.shape, lambda b, i: (0, 0))],
        out_specs=rows(D),
        out_shape=jax.ShapeDtypeStruct((B, S, D), x.dtype),
        compiler_params=pltpu.CompilerParams(
            dimension_semantics=("arbitrary", "arbitrary"), vmem_limit_bytes=VMEM_LIMIT),
        name="outproj",
    )(mixed_sb, mixed_mla, x, mod, w_out)


def _pad_last(a, width):
    return jnp.pad(a, [(0, 0)] * (a.ndim - 1) + [(0, width - a.shape[-1])])


def _layer(x, c, tables, w_ada, b_ada, norm_w, w_in, q_lora_norm, w_uq, kv_lora_norm, w_ukv,
           q_head_norm, k_head_norm, w_out):
    B, S, D = x.shape
    c_pad = jnp.pad(c, ((0, 8 - B), (0, 0)))
    ada = _adaln(c_pad, w_ada, b_ada[None, :])[:B]
    mod = ada.reshape(B, 3, D)

    ts = min(ROW_TILE, S)
    w_in_b = w_in.astype(BF16)
    w_gm = w_in[:, _C_GMLA:].astype(BF16)
    w_uq_t = _pad_last(w_uq.reshape(Q_LORA_RANK, MLA_HEADS, MLA_QK_DIM), LANES)
    w_uq_t = w_uq_t.reshape(Q_LORA_RANK, MLA_PAD_WIDTH).T.astype(BF16)
    w_ukv_h = w_ukv.reshape(KV_LORA_RANK, MLA_HEADS, MLA_NOPE_DIM + MLA_V_DIM)
    w_uk_t = _pad_last(w_ukv_h[:, :, :MLA_NOPE_DIM], LANES)
    w_uk_t = w_uk_t.reshape(KV_LORA_RANK, MLA_PAD_WIDTH).T.astype(BF16)
    w_uv = w_ukv_h[:, :, MLA_NOPE_DIM:].reshape(KV_LORA_RANK, MLA_WIDTH).T.astype(BF16)
    gq = jnp.broadcast_to(_pad_last(q_head_norm, LANES)[:, None], (LANES, ts))
    gk = jnp.broadcast_to(_pad_last(k_head_norm, LANES)[:, None], (LANES, ts))

    (q_sb, k_sb, v_sb, g_sb, q_m, k_m, v_m, g_m) = _inproj(
        x, mod, norm_w[None, :], w_in_b, w_gm, q_lora_norm[None, :], w_uq_t, kv_lora_norm[None, :],
        w_uk_t, w_uv, gq, gk, *tables)

    mixed_sb = _sb_attention(q_sb, k_sb, v_sb, g_sb)
    mixed_mla = _mla_attention(q_m, k_m, v_m, g_m)
    return _outproj(mixed_sb, mixed_mla, x, mod, w_out.astype(BF16))


def _rope_tables(positions, dtype):
    inv_freq = ROPE_THETA ** (-jnp.arange(0, MLA_ROPE_DIM, 2, dtype=F32) / MLA_ROPE_DIM)
    ang = inv_freq[None, :, None] * positions.astype(F32)[:, None, :]
    return jnp.cos(ang).astype(dtype), jnp.sin(ang).astype(dtype)


def kernel(x, c, positions, w_ada, b_ada, norm_w, w_in, q_lora_norm, w_uq, kv_lora_norm, w_ukv,
           q_head_norm, k_head_norm, w_out):
    tables = _rope_tables(positions, x.dtype)
    for l in range(w_ada.shape[0]):
        x = _layer(x, c, tables, w_ada[l], b_ada[l], norm_w[l], w_in[l], q_lora_norm[l], w_uq[l],
                   kv_lora_norm[l], w_ukv[l], q_head_norm[l], k_head_norm[l], w_out[l])
    return x
```

```python
import functools
import math

import jax
import jax.numpy as jnp
from jax import lax
from jax.experimental import pallas as pl
from jax.experimental.pallas import tpu as pltpu

F32 = jnp.float32
BF16 = jnp.bfloat16

D_MODEL = 1024
SB_HEADS = 8
SB_HEAD_DIM = 64
SB_WIDTH = SB_HEADS * SB_HEAD_DIM
MLA_HEADS = 8
MLA_NOPE_DIM = 64
MLA_ROPE_DIM = 32
MLA_QK_DIM = MLA_NOPE_DIM + MLA_ROPE_DIM
MLA_V_DIM = 64
MLA_WIDTH = MLA_HEADS * MLA_V_DIM
MLA_V_ONES = 16
MLA_V_ROWS = MLA_V_DIM + MLA_V_ONES
Q_LORA_RANK = 384
KV_LORA_RANK = 256
ROPE_THETA = 10000.0
EPS = 1e-6

LANES = 128
HEAD_PAIR = 2
MLA_PAD_WIDTH = MLA_HEADS * LANES

_C_QSB = 0
_C_KSB = _C_QSB + SB_WIDTH
_C_VSB = _C_KSB + SB_WIDTH
_C_GSB = _C_VSB + SB_WIDTH
_C_CQ = _C_GSB + SB_WIDTH
_C_CKV = _C_CQ + Q_LORA_RANK
_C_KR = _C_CKV + KV_LORA_RANK
_C_GMLA = _C_KR + MLA_ROPE_DIM

ROW_TILE = 512
OUT_ROW_TILE = 1024
SB_TQ = 256
SB_TK = 256
SB_PAIRS = 4
SB_STAGE_LAG = (2, 4)
MLA_TQ = 512
MLA_TK = 256
VMEM_LIMIT = 56 * 1024 * 1024

SB_DEAD_LOG2_DROP = 160.0
SB_Q_SCALE = math.log2(math.e) / math.sqrt(SB_HEAD_DIM)
MLA_Q_SCALE = math.log2(math.e) / math.sqrt(MLA_QK_DIM)


def _silu(g):
    return g * (1.0 / (1.0 + jnp.exp(-g)))


def _dot(a, b):
    return jnp.dot(a, b, preferred_element_type=F32)


def _dot_nt(a, b):
    return lax.dot_general(a, b, (((1,), (1,)), ((), ())), preferred_element_type=F32)


def _dot_tn(a, b):
    return lax.dot_general(a, b, (((0,), (0,)), ((), ())), preferred_element_type=F32)


def _adaln_kernel(c_ref, w_ref, b_ref, o_ref):
    o_ref[...] = jnp.dot(_silu(c_ref[...]), w_ref[...], preferred_element_type=F32,
                         precision=lax.Precision.HIGHEST) + b_ref[...]


def _adaln(c_pad, w_ada, b_ada):
    rows, d = c_pad.shape
    n = w_ada.shape[1]
    bn = 512
    return pl.pallas_call(
        _adaln_kernel,
        grid=(n // bn,),
        in_specs=[pl.BlockSpec((rows, d), lambda j: (0, 0)),
                  pl.BlockSpec((d, bn), lambda j: (0, j)),
                  pl.BlockSpec((1, bn), lambda j: (0, j))],
        out_specs=pl.BlockSpec((rows, bn), lambda j: (0, j)),
        out_shape=jax.ShapeDtypeStruct((rows, n), F32),
        name="adaln",
    )(c_pad, w_ada, b_ada)


def _rope_t(x1, x2, cos, sin):
    return x1 * cos - x2 * sin, x2 * cos + x1 * sin


def _inproj_kernel(x_ref, mod_ref, nw_ref, win_ref, wgm_ref, qln_ref, wuqt_ref, kvln_ref, wukt_ref, wuvt_ref,
                   gq_ref, gk_ref, cos_ref, sin_ref,
                   qsb_ref, ksb_ref, vsb_ref, gsb_ref, qmt_ref, km_ref, vmt_ref, gm_ref):
    half = MLA_ROPE_DIM // 2
    n0, n1, n2 = MLA_NOPE_DIM, MLA_NOPE_DIM + half, MLA_QK_DIM
    x = x_ref[0]
    ts = x.shape[0]
    y = x * lax.rsqrt(jnp.mean(x * x, axis=-1, keepdims=True) + EPS) * nw_ref[...]
    h = (y * (1.0 + mod_ref[0, 1:2, :]) + mod_ref[0, 0:1, :]).astype(BF16)

    def proj(lo, hi):
        return _dot(h, win_ref[:, lo:hi])

    def rms(t, w_ref):
        return t * lax.rsqrt(jnp.mean(t * t, axis=-1, keepdims=True) + EPS) * w_ref[...]

    cq = proj(_C_CQ, _C_CKV)
    ckv = proj(_C_CKV, _C_KR)
    kr = proj(_C_KR, _C_KR + LANES)
    qsb_ref[0] = (proj(_C_QSB, _C_KSB) * SB_Q_SCALE).astype(BF16)
    ksb_ref[0] = proj(_C_KSB, _C_VSB).astype(BF16)

    cqn = rms(cq, qln_ref).astype(BF16)
    ckvn = rms(ckv, kvln_ref).astype(BF16)
    q_t = _dot_nt(wuqt_ref[...], cqn)
    k_t = _dot_nt(wukt_ref[...], ckvn)
    v_t = _dot_nt(wuvt_ref[...], ckvn)
    vsb_ref[0] = proj(_C_VSB, _C_GSB).astype(BF16)
    gsb_ref[0] = proj(_C_GSB, _C_CQ)
    gm_ref[0] = _dot(h, wgm_ref[...])

    cos, sin = cos_ref[0], sin_ref[0]
    pad_rows = jnp.zeros((LANES - MLA_QK_DIM, ts), F32)

    gq = gq_ref[...]
    for hd in range(MLA_HEADS):
        blk = q_t[hd * LANES:(hd + 1) * LANES]
        inv = lax.rsqrt(jnp.sum(blk * blk, axis=0, keepdims=True) * (1.0 / MLA_QK_DIM) + EPS)
        g = blk * gq
        r1, r2 = _rope_t(g[n0:n1], g[n1:n2], cos, sin)
        out = jnp.concatenate([g[:n0], r1, r2, pad_rows], axis=0) * (inv * MLA_Q_SCALE)
        qmt_ref[0, hd * LANES:(hd + 1) * LANES, :] = out.astype(BF16)

    ones_rows = jnp.ones((MLA_V_ONES, ts), F32)
    v_rows = []
    for hd in range(MLA_HEADS):
        v_rows += [v_t[hd * MLA_V_DIM:(hd + 1) * MLA_V_DIM], ones_rows]
    vmt_ref[0] = jnp.concatenate(v_rows, axis=0).astype(BF16)

    gk = gk_ref[...]
    kr_t = kr.T
    x1, x2 = kr_t[:half], kr_t[half:MLA_ROPE_DIM]
    kr_ssq = jnp.sum(x1 * x1 + x2 * x2, axis=0, keepdims=True)
    r1, r2 = _rope_t(x1 * gk[n0:n1], x2 * gk[n1:n2], cos, sin)
    for hd in range(MLA_HEADS):
        kn = k_t[hd * LANES:hd * LANES + n0]
        ssq = jnp.sum(kn * kn, axis=0, keepdims=True) + kr_ssq
        inv = lax.rsqrt(ssq * (1.0 / MLA_QK_DIM) + EPS)
        out = jnp.concatenate([kn * gk[:n0], r1, r2, pad_rows], axis=0) * inv
        km_ref[0, :, hd * LANES:(hd + 1) * LANES] = out.astype(BF16).T


def _inproj(x, mod, norm_w, w_in_b, w_gm, qln, w_uq_t, kvln, w_uk_t, w_uv, gq, gk, cos_t, sin_t):
    B, S, D = x.shape
    ts = min(ROW_TILE, S)
    grid = (B, S // ts)

    def whole(a):
        return pl.BlockSpec(a.shape, lambda b, i: (0,) * a.ndim)

    def rows(width):
        return pl.BlockSpec((1, ts, width), lambda b, i: (b, i, 0))

    def cols(height):
        return pl.BlockSpec((1, height, ts), lambda b, i: (b, 0, i))

    out_shapes = (
        jax.ShapeDtypeStruct((B, S, SB_WIDTH), BF16),
        jax.ShapeDtypeStruct((B, S, SB_WIDTH), BF16),
        jax.ShapeDtypeStruct((B, S, SB_WIDTH), BF16),
        jax.ShapeDtypeStruct((B, S, SB_WIDTH), F32),
        jax.ShapeDtypeStruct((B, MLA_PAD_WIDTH, S), BF16),
        jax.ShapeDtypeStruct((B, S, MLA_PAD_WIDTH), BF16),
        jax.ShapeDtypeStruct((B, MLA_HEADS * MLA_V_ROWS, S), BF16),
        jax.ShapeDtypeStruct((B, S, MLA_WIDTH), F32),
    )
    out_specs = [rows(s.shape[-1]) for s in out_shapes]
    out_specs[4] = cols(MLA_PAD_WIDTH)
    out_specs[6] = cols(MLA_HEADS * MLA_V_ROWS)
    half = MLA_ROPE_DIM // 2
    return pl.pallas_call(
        _inproj_kernel,
        grid=grid,
        in_specs=[rows(D),
                  pl.BlockSpec((1, 3, D), lambda b, i: (b, 0, 0)),
                  whole(norm_w), whole(w_in_b), whole(w_gm), whole(qln), whole(w_uq_t), whole(kvln),
                  whole(w_uk_t), whole(w_uv), whole(gq), whole(gk),
                  cols(half), cols(half)],
        out_specs=out_specs,
        out_shape=out_shapes,
        compiler_params=pltpu.CompilerParams(
            dimension_semantics=("arbitrary", "arbitrary"), vmem_limit_bytes=VMEM_LIMIT),
        name="inproj",
    )(x, mod, norm_w, w_in_b, w_gm, qln, w_uq_t, kvln, w_uk_t, w_uv, gq, gk, cos_t, sin_t)


def _sb_kernel(q_ref, k_ref, v_ref, g_ref, o_ref, acc_ref, carry_ref):
    tq, tk = SB_TQ, SB_TK
    n_heads = SB_PAIRS * HEAD_PAIR
    i = pl.program_id(2)
    lane = lax.broadcasted_iota(jnp.int32, (tq, LANES), 1)
    q_heads = []
    for p in range(SB_PAIRS):
        q_pair = q_ref[0, :, p * LANES:(p + 1) * LANES]
        for hh in range(HEAD_PAIR):
            q_heads.append(jnp.where((lane // SB_HEAD_DIM) == hh, q_pair, jnp.zeros_like(q_pair)))

    strict = (lax.broadcasted_iota(jnp.int32, (tk, tq), 0)
              < lax.broadcasted_iota(jnp.int32, (tk, tq), 1))
    l_row = lax.broadcasted_iota(jnp.int32, (tk + 8, tk), 0)
    l_col = lax.broadcasted_iota(jnp.int32, (tk + 8, tk), 1)
    later = jnp.where(jnp.logical_or(l_col > l_row, l_row >= tk), 1.0, 0.0).astype(F32)

    acc_ref[...] = jnp.zeros_like(acc_ref)
    carry_ref[...] = jnp.zeros_like(carry_ref)

    def visit(tiles, diag_first):
        items = [(t, j, h) for t, j in enumerate(tiles) for h in range(n_heads)]

        def pair_block(ref, j, h):
            p = h // HEAD_PAIR
            return ref[0, pl.ds(pl.multiple_of(j * tk, tk), tk), p * LANES:(p + 1) * LANES]

        n = len(items)
        zs, log_betas, afters, col_sums = [None] * n, [None] * n, [None] * n, [None] * n
        carries = [carry_ref[h] for h in range(n_heads)]
        outs = [None] * n_heads

        def scores(idx):
            _, j, h = items[idx]
            zs[idx] = _dot_nt(pair_block(k_ref, j, h), q_heads[h])

        def drops(idx):
            t, _, _ = items[idx]
            z = zs[idx]
            drop = jnp.maximum(z, 0.0) + jnp.log2(1.0 + jnp.exp2(-jnp.abs(z)))
            if diag_first and t == 0:
                drop = jnp.where(strict, drop, 0.0)
            hi = drop.astype(BF16).astype(F32)
            sums = _dot(later, hi) + _dot(later, drop - hi)
            afters[idx] = sums[:tk]
            col_sums[idx] = sums[tk:tk + 1]
            log_betas[idx] = z - drop

        def weights(idx):
            t, j, h = items[idx]
            w = jnp.exp2(log_betas[idx] - (afters[idx] + carries[h]))
            if diag_first and t == 0:
                w = jnp.where(strict, w, 0.0)
            pv = _dot_tn(pair_block(v_ref, j, h), w.astype(BF16))
            lo_row = (h % HEAD_PAIR) * SB_HEAD_DIM
            pv = pv[lo_row:lo_row + SB_HEAD_DIM, :]
            outs[h] = pv if outs[h] is None else outs[h] + pv
            carries[h] = carries[h] + col_sums[idx]

        for step in range(n + SB_STAGE_LAG[1]):
            if step < n:
                scores(step)
            if 0 <= step - SB_STAGE_LAG[0] < n:
                drops(step - SB_STAGE_LAG[0])
            if 0 <= step - SB_STAGE_LAG[1] < n:
                weights(step - SB_STAGE_LAG[1])
        dead = None
        for h in range(n_heads):
            acc_ref[h] += outs[h]
            carry_ref[h] = carries[h]
            dead = carries[h] if dead is None else jnp.minimum(dead, carries[h])
        return jnp.min(dead) < SB_DEAD_LOG2_DROP

    alive0 = lax.cond(i > 0, lambda: visit([i, i - 1], True), lambda: visit([i], True))

    def cond(state):
        j, alive = state
        return jnp.logical_and(j >= 0, alive)

    def body(state):
        j, _ = state
        return j - 1, visit([j], False)

    lax.while_loop(cond, body, (i - 2, alive0))

    o_t = acc_ref[...].reshape(n_heads * SB_HEAD_DIM, tq)
    o_ref[0] = (o_t.T * _silu(g_ref[0])).astype(o_ref.dtype)


def _sb_attention(q, k, v, g):
    B, S, W = q.shape
    assert S % SB_TQ == 0 and SB_TQ == SB_TK
    wb = SB_PAIRS * LANES
    grid = (B, W // wb, S // SB_TQ)
    q_spec = pl.BlockSpec((1, SB_TQ, wb), lambda b, p, i: (b, i, p))
    kv_spec = pl.BlockSpec((1, S, wb), lambda b, p, i: (b, 0, p))
    return pl.pallas_call(
        _sb_kernel,
        grid=grid,
        in_specs=[q_spec, kv_spec, kv_spec, q_spec],
        out_specs=q_spec,
        out_shape=jax.ShapeDtypeStruct((B, S, W), BF16),
        scratch_shapes=[pltpu.VMEM((SB_PAIRS * HEAD_PAIR, SB_HEAD_DIM, SB_TQ), F32),
                        pltpu.VMEM((SB_PAIRS * HEAD_PAIR, 1, SB_TQ), F32)],
        compiler_params=pltpu.CompilerParams(
            dimension_semantics=("arbitrary", "arbitrary", "arbitrary"),
            vmem_limit_bytes=VMEM_LIMIT),
        name="sb_attn",
    )(q, k, v, g)


def _mla_kernel(q_ref, k_ref, v_ref, g_ref, o_ref, acc_ref, m_ref, s_ref):
    tq, tk = MLA_TQ, MLA_TK
    diag_tiles = tq // tk
    i = pl.program_id(2)
    neg = jnp.finfo(F32).min

    acc_ref[...] = jnp.zeros_like(acc_ref)
    m_ref[...] = jnp.full_like(m_ref, neg)

    def score(j, slot, diag=None):
        start = pl.multiple_of(j * tk, tk)
        lo = 0 if diag is None else diag * tk
        for hh in range(HEAD_PAIR):
            q_t = q_ref[0, hh * LANES:(hh + 1) * LANES, lo:]
            k_blk = k_ref[0, pl.ds(start, tk), hh * LANES:(hh + 1) * LANES]
            s_ref[slot, hh, :, lo:] = _dot(k_blk, q_t)

    def consume(j, slot, diag=None):
        start = pl.multiple_of(j * tk, tk)
        lo = 0 if diag is None else diag * tk
        for hh in range(HEAD_PAIR):
            v_t = v_ref[0, hh * MLA_V_ROWS:(hh + 1) * MLA_V_ROWS, pl.ds(start, tk)]
            s = s_ref[slot, hh, :, lo:]
            if diag is not None:
                causal = (lax.broadcasted_iota(jnp.int32, s.shape, 0)
                          <= lax.broadcasted_iota(jnp.int32, s.shape, 1))
                s = jnp.where(causal, s, neg)
            m_old = m_ref[hh, :, lo:]
            m_new = jnp.maximum(m_old, jnp.max(s, axis=0, keepdims=True))
            alpha = jnp.exp2(m_old[:1] - m_new[:1])
            p = jnp.exp2(s - m_new[:1])
            acc_ref[hh, :, lo:] = alpha * acc_ref[hh, :, lo:] + _dot(v_t, p.astype(BF16))
            m_ref[hh, :, lo:] = m_new

    base = i * diag_tiles
    score(0, 0)

    def two_tiles(t):
        score(t + 1, 1)
        consume(t, 0)
        score(t + 2, 0)
        consume(t + 1, 1)

    def four_tiles(jj, carry):
        two_tiles(4 * jj)
        two_tiles(4 * jj + 2)
        return carry

    n_four = base // 4
    lax.fori_loop(0, n_four, four_tiles, 0)

    def leftover(_, carry):
        two_tiles(4 * n_four)
        return carry

    lax.fori_loop(0, (base - 4 * n_four) // 2, leftover, 0)
    for d in range(diag_tiles):
        if d + 1 < diag_tiles:
            score(base + d + 1, (d + 1) % 2, d + 1)
        consume(base + d, d % 2, d)

    o_t = jnp.concatenate(
        [acc_ref[hh, :MLA_V_DIM] / acc_ref[hh, MLA_V_DIM:MLA_V_DIM + 1] for hh in range(HEAD_PAIR)],
        axis=0)
    o_ref[0] = (o_t.T * _silu(g_ref[0])).astype(o_ref.dtype)


def _mla_attention(q, k, v, g):
    B, S, W = g.shape
    tq = MLA_TQ
    pair_rows = HEAD_PAIR * MLA_V_ROWS
    assert S % tq == 0 and MLA_TQ % (2 * MLA_TK) == 0
    grid = (B, W // LANES, S // tq)
    return pl.pallas_call(
        _mla_kernel,
        grid=grid,
        in_specs=[pl.BlockSpec((1, HEAD_PAIR * LANES, tq), lambda b, p, i: (b, p, i)),
                  pl.BlockSpec((1, S, HEAD_PAIR * LANES), lambda b, p, i: (b, 0, p)),
                  pl.BlockSpec((1, pair_rows, S), lambda b, p, i: (b, p, 0)),
                  pl.BlockSpec((1, tq, LANES), lambda b, p, i: (b, i, p))],
        out_specs=pl.BlockSpec((1, tq, LANES), lambda b, p, i: (b, i, p)),
        out_shape=jax.ShapeDtypeStruct((B, S, W), BF16),
        scratch_shapes=[pltpu.VMEM((HEAD_PAIR, MLA_V_ROWS, tq), F32),
                        pltpu.VMEM((HEAD_PAIR, 8, tq), F32),
                        pltpu.VMEM((2, HEAD_PAIR, MLA_TK, tq), F32)],
        compiler_params=pltpu.CompilerParams(
            dimension_semantics=("arbitrary", "arbitrary", "arbitrary"),
            vmem_limit_bytes=VMEM_LIMIT),
        name="mla_attn",
    )(q, k, v, g)


def _outproj_kernel(msb_ref, mmla_ref, x_ref, mod_ref, w_ref, o_ref):
    y = _dot(msb_ref[0], w_ref[:SB_WIDTH, :]) + _dot(mmla_ref[0], w_ref[SB_WIDTH:, :])
    o_ref[0] = x_ref[0] + mod_ref[0, 2:3, :] * y


def _outproj(mixed_sb, mixed_mla, x, mod, w_out):
    B, S, D = x.shape
    ts = min(OUT_ROW_TILE, S)

    def rows(width):
        return pl.BlockSpec((1, ts, width), lambda b, i: (b, i, 0))

    return pl.pallas_call(
        _outproj_kernel,
        grid=(B, S // ts),
        in_specs=[rows(SB_WIDTH), rows(MLA_WIDTH), rows(D),
                  pl.BlockSpec((1, 3, D), lambda b, i: (b, 0, 0)),
                  pl.BlockSpec(w_out.shape, lambda b, i: (0, 0))],
        out_specs=rows(D),
        out_shape=jax.ShapeDtypeStruct((B, S, D), x.dtype),
        compiler_params=pltpu.CompilerParams(
            dimension_semantics=("arbitrary", "arbitrary"), vmem_limit_bytes=VMEM_LIMIT),
        name="outproj",
    )(mixed_sb, mixed_mla, x, mod, w_out)


def _pad_last(a, width):
    return jnp.pad(a, [(0, 0)] * (a.ndim - 1) + [(0, width - a.shape[-1])])


def _layer(x, c, tables, w_ada, b_ada, norm_w, w_in, q_lora_norm, w_uq, kv_lora_norm, w_ukv,
           q_head_norm, k_head_norm, w_out):
    B, S, D = x.shape
    c_pad = jnp.pad(c, ((0, 8 - B), (0, 0)))
    ada = _adaln(c_pad, w_ada, b_ada[None, :])[:B]
    mod = ada.reshape(B, 3, D)

    ts = min(ROW_TILE, S)
    w_in_b = w_in.astype(BF16)
    w_gm = w_in[:, _C_GMLA:].astype(BF16)
    w_uq_t = _pad_last(w_uq.reshape(Q_LORA_RANK, MLA_HEADS, MLA_QK_DIM), LANES)
    w_uq_t = w_uq_t.reshape(Q_LORA_RANK, MLA_PAD_WIDTH).T.astype(BF16)
    w_ukv_h = w_ukv.reshape(KV_LORA_RANK, MLA_HEADS, MLA_NOPE_DIM + MLA_V_DIM)
    w_uk_t = _pad_last(w_ukv_h[:, :, :MLA_NOPE_DIM], LANES)
    w_uk_t = w_uk_t.reshape(KV_LORA_RANK, MLA_PAD_WIDTH).T.astype(BF16)
    w_uv = w_ukv_h[:, :, MLA_NOPE_DIM:].reshape(KV_LORA_RANK, MLA_WIDTH).T.astype(BF16)
    gq = jnp.broadcast_to(_pad_last(q_head_norm, LANES)[:, None], (LANES, ts))
    gk = jnp.broadcast_to(_pad_last(k_head_norm, LANES)[:, None], (LANES, ts))

    (q_sb, k_sb, v_sb, g_sb, q_m, k_m, v_m, g_m) = _inproj(
        x, mod, norm_w[None, :], w_in_b, w_gm, q_lora_norm[None, :], w_uq_t, kv_lora_norm[None, :],
        w_uk_t, w_uv, gq, gk, *tables)

    mixed_sb = _sb_attention(q_sb, k_sb, v_sb, g_sb)
    mixed_mla = _mla_attention(q_m, k_m, v_m, g_m)
    return _outproj(mixed_sb, mixed_mla, x, mod, w_out.astype(BF16))


def _rope_tables(positions, dtype):
    inv_freq = ROPE_THETA ** (-jnp.arange(0, MLA_ROPE_DIM, 2, dtype=F32) / MLA_ROPE_DIM)
    ang = inv_freq[None, :, None] * positions.astype(F32)[:, None, :]
    return jnp.cos(ang).astype(dtype), jnp.sin(ang).astype(dtype)


def kernel(x, c, positions, w_ada, b_ada, norm_w, w_in, q_lora_norm, w_uq, kv_lora_norm, w_ukv,
           q_head_norm, k_head_norm, w_out):
    tables = _rope_tables(positions, x.dtype)
    for l in range(w_ada.shape[0]):
        x = _layer(x, c, tables, w_ada[l], b_ada[l], norm_w[l], w_in[l], q_lora_norm[l], w_uq[l],
                   kv_lora_norm[l], w_ukv[l], q_head_norm[l], k_head_norm[l], w_out[l])
    return x
```

```python
import functools
import math

import jax
import jax.numpy as jnp
from jax import lax
from jax.experimental import pallas as pl
from jax.experimental.pallas import tpu as pltpu

F32 = jnp.float32
BF16 = jnp.bfloat16

D_MODEL = 1024
SB_HEADS = 8
SB_HEAD_DIM = 64
SB_WIDTH = SB_HEADS * SB_HEAD_DIM
MLA_HEADS = 8
MLA_NOPE_DIM = 64
MLA_ROPE_DIM = 32
MLA_QK_DIM = MLA_NOPE_DIM + MLA_ROPE_DIM
MLA_V_DIM = 64
MLA_WIDTH = MLA_HEADS * MLA_V_DIM
MLA_V_ONES = 16
MLA_V_ROWS = MLA_V_DIM + MLA_V_ONES
Q_LORA_RANK = 384
KV_LORA_RANK = 256
ROPE_THETA = 10000.0
EPS = 1e-6

LANES = 128
HEAD_PAIR = 2
MLA_PAD_WIDTH = MLA_HEADS * LANES

_C_QSB = 0
_C_KSB = _C_QSB + SB_WIDTH
_C_VSB = _C_KSB + SB_WIDTH
_C_GSB = _C_VSB + SB_WIDTH
_C_CQ = _C_GSB + SB_WIDTH
_C_CKV = _C_CQ + Q_LORA_RANK
_C_KR = _C_CKV + KV_LORA_RANK
_C_GMLA = _C_KR + MLA_ROPE_DIM

ROW_TILE = 512
OUT_ROW_TILE = 1024
SB_TQ = 256
SB_TK = 256
SB_PAIRS = 4
SB_STAGE_LAG = (2, 4)
MLA_GROUP = 2
MLA_UNROLL = 4
MLA_TQ = 512
MLA_TK = 256
VMEM_LIMIT = 56 * 1024 * 1024

SB_DEAD_LOG2_DROP = 160.0
SB_Q_SCALE = math.log2(math.e) / math.sqrt(SB_HEAD_DIM)
MLA_Q_SCALE = math.log2(math.e) / math.sqrt(MLA_QK_DIM)


def _silu(g):
    return g * (1.0 / (1.0 + jnp.exp(-g)))


def _dot(a, b):
    return jnp.dot(a, b, preferred_element_type=F32)


def _dot_nt(a, b):
    return lax.dot_general(a, b, (((1,), (1,)), ((), ())), preferred_element_type=F32)


def _dot_tn(a, b):
    return lax.dot_general(a, b, (((0,), (0,)), ((), ())), preferred_element_type=F32)


def _adaln_kernel(c_ref, w_ref, b_ref, o_ref):
    o_ref[...] = jnp.dot(_silu(c_ref[...]), w_ref[...], preferred_element_type=F32,
                         precision=lax.Precision.HIGHEST) + b_ref[...]


def _adaln(c_pad, w_ada, b_ada):
    rows, d = c_pad.shape
    n = w_ada.shape[1]
    bn = 512
    return pl.pallas_call(
        _adaln_kernel,
        grid=(n // bn,),
        in_specs=[pl.BlockSpec((rows, d), lambda j: (0, 0)),
                  pl.BlockSpec((d, bn), lambda j: (0, j)),
                  pl.BlockSpec((1, bn), lambda j: (0, j))],
        out_specs=pl.BlockSpec((rows, bn), lambda j: (0, j)),
        out_shape=jax.ShapeDtypeStruct((rows, n), F32),
        name="adaln",
    )(c_pad, w_ada, b_ada)


def _rope_t(x1, x2, cos, sin):
    return x1 * cos - x2 * sin, x2 * cos + x1 * sin


def _inproj_kernel(x_ref, mod_ref, nw_ref, win_ref, wgm_ref, qln_ref, wuqt_ref, kvln_ref, wukt_ref, wuvt_ref,
                   gq_ref, gk_ref, cos_ref, sin_ref,
                   qsb_ref, ksb_ref, vsb_ref, gsb_ref, qmt_ref, km_ref, vmt_ref, gm_ref):
    half = MLA_ROPE_DIM // 2
    n0, n1, n2 = MLA_NOPE_DIM, MLA_NOPE_DIM + half, MLA_QK_DIM
    x = x_ref[0]
    ts = x.shape[0]
    y = x * lax.rsqrt(jnp.mean(x * x, axis=-1, keepdims=True) + EPS) * nw_ref[...]
    h = (y * (1.0 + mod_ref[0, 1:2, :]) + mod_ref[0, 0:1, :]).astype(BF16)

    def proj(lo, hi):
        return _dot(h, win_ref[:, lo:hi])

    def rms(t, w_ref):
        return t * lax.rsqrt(jnp.mean(t * t, axis=-1, keepdims=True) + EPS) * w_ref[...]

    cq = proj(_C_CQ, _C_CKV)
    ckv = proj(_C_CKV, _C_KR)
    kr = proj(_C_KR, _C_KR + LANES)
    qsb_ref[0] = (proj(_C_QSB, _C_KSB) * SB_Q_SCALE).astype(BF16)
    ksb_ref[0] = proj(_C_KSB, _C_VSB).astype(BF16)

    cqn = rms(cq, qln_ref).astype(BF16)
    ckvn = rms(ckv, kvln_ref).astype(BF16)
    q_t = _dot_nt(wuqt_ref[...], cqn)
    k_t = _dot_nt(wukt_ref[...], ckvn)
    v_t = _dot_nt(wuvt_ref[...], ckvn)
    vsb_ref[0] = proj(_C_VSB, _C_GSB).astype(BF16)
    gsb_ref[0] = proj(_C_GSB, _C_CQ)
    gm_ref[0] = _dot(h, wgm_ref[...])

    cos, sin = cos_ref[0], sin_ref[0]
    pad_rows = jnp.zeros((LANES - MLA_QK_DIM, ts), F32)

    gq = gq_ref[...]
    for hd in range(MLA_HEADS):
        blk = q_t[hd * LANES:(hd + 1) * LANES]
        inv = lax.rsqrt(jnp.sum(blk * blk, axis=0, keepdims=True) * (1.0 / MLA_QK_DIM) + EPS)
        g = blk * gq
        r1, r2 = _rope_t(g[n0:n1], g[n1:n2], cos, sin)
        out = jnp.concatenate([g[:n0], r1, r2, pad_rows], axis=0) * (inv * MLA_Q_SCALE)
        qmt_ref[0, hd * LANES:(hd + 1) * LANES, :] = out.astype(BF16)

    ones_rows = jnp.ones((MLA_V_ONES, ts), F32)
    v_rows = []
    for hd in range(MLA_HEADS):
        v_rows += [v_t[hd * MLA_V_DIM:(hd + 1) * MLA_V_DIM], ones_rows]
    vmt_ref[0] = jnp.concatenate(v_rows, axis=0).astype(BF16)

    gk = gk_ref[...]
    kr_t = kr.T
    x1, x2 = kr_t[:half], kr_t[half:MLA_ROPE_DIM]
    kr_ssq = jnp.sum(x1 * x1 + x2 * x2, axis=0, keepdims=True)
    r1, r2 = _rope_t(x1 * gk[n0:n1], x2 * gk[n1:n2], cos, sin)
    for hd in range(MLA_HEADS):
        kn = k_t[hd * LANES:hd * LANES + n0]
        ssq = jnp.sum(kn * kn, axis=0, keepdims=True) + kr_ssq
        inv = lax.rsqrt(ssq * (1.0 / MLA_QK_DIM) + EPS)
        out = jnp.concatenate([kn * gk[:n0], r1, r2, pad_rows], axis=0) * inv
        km_ref[0, :, hd * LANES:(hd + 1) * LANES] = out.astype(BF16).T


def _inproj(x, mod, norm_w, w_in_b, w_gm, qln, w_uq_t, kvln, w_uk_t, w_uv, gq, gk, cos_t, sin_t):
    B, S, D = x.shape
    ts = min(ROW_TILE, S)
    grid = (B, S // ts)

    def whole(a):
        return pl.BlockSpec(a.shape, lambda b, i: (0,) * a.ndim)

    def rows(width):
        return pl.BlockSpec((1, ts, width), lambda b, i: (b, i, 0))

    def cols(height):
        return pl.BlockSpec((1, height, ts), lambda b, i: (b, 0, i))

    out_shapes = (
        jax.ShapeDtypeStruct((B, S, SB_WIDTH), BF16),
        jax.ShapeDtypeStruct((B, S, SB_WIDTH), BF16),
        jax.ShapeDtypeStruct((B, S, SB_WIDTH), BF16),
        jax.ShapeDtypeStruct((B, S, SB_WIDTH), F32),
        jax.ShapeDtypeStruct((B, MLA_PAD_WIDTH, S), BF16),
        jax.ShapeDtypeStruct((B, S, MLA_PAD_WIDTH), BF16),
        jax.ShapeDtypeStruct((B, MLA_HEADS * MLA_V_ROWS, S), BF16),
        jax.ShapeDtypeStruct((B, S, MLA_WIDTH), F32),
    )
    out_specs = [rows(s.shape[-1]) for s in out_shapes]
    out_specs[4] = cols(MLA_PAD_WIDTH)
    out_specs[6] = cols(MLA_HEADS * MLA_V_ROWS)
    half = MLA_ROPE_DIM // 2
    return pl.pallas_call(
        _inproj_kernel,
        grid=grid,
        in_specs=[rows(D),
                  pl.BlockSpec((1, 3, D), lambda b, i: (b, 0, 0)),
                  whole(norm_w), whole(w_in_b), whole(w_gm), whole(qln), whole(w_uq_t), whole(kvln),
                  whole(w_uk_t), whole(w_uv), whole(gq), whole(gk),
                  cols(half), cols(half)],
        out_specs=out_specs,
        out_shape=out_shapes,
        compiler_params=pltpu.CompilerParams(
            dimension_semantics=("arbitrary", "arbitrary"), vmem_limit_bytes=VMEM_LIMIT),
        name="inproj",
    )(x, mod, norm_w, w_in_b, w_gm, qln, w_uq_t, kvln, w_uk_t, w_uv, gq, gk, cos_t, sin_t)


def _sb_kernel(q_ref, k_ref, v_ref, g_ref, o_ref, acc_ref, carry_ref):
    tq, tk = SB_TQ, SB_TK
    n_heads = SB_PAIRS * HEAD_PAIR
    i = pl.program_id(2)
    lane = lax.broadcasted_iota(jnp.int32, (tq, LANES), 1)
    q_heads = []
    for p in range(SB_PAIRS):
        q_pair = q_ref[0, :, p * LANES:(p + 1) * LANES]
        for hh in range(HEAD_PAIR):
            q_heads.append(jnp.where((lane // SB_HEAD_DIM) == hh, q_pair, jnp.zeros_like(q_pair)))

    strict = (lax.broadcasted_iota(jnp.int32, (tk, tq), 0)
              < lax.broadcasted_iota(jnp.int32, (tk, tq), 1))
    l_row = lax.broadcasted_iota(jnp.int32, (tk + 8, tk), 0)
    l_col = lax.broadcasted_iota(jnp.int32, (tk + 8, tk), 1)
    later = jnp.where(jnp.logical_or(l_col > l_row, l_row >= tk), 1.0, 0.0).astype(F32)

    acc_ref[...] = jnp.zeros_like(acc_ref)
    carry_ref[...] = jnp.zeros_like(carry_ref)

    def visit(tiles, diag_first):
        items = [(t, j, h) for t, j in enumerate(tiles) for h in range(n_heads)]

        def pair_block(ref, j, h):
            p = h // HEAD_PAIR
            return ref[0, pl.ds(pl.multiple_of(j * tk, tk), tk), p * LANES:(p + 1) * LANES]

        n = len(items)
        zs, log_betas, afters, col_sums = [None] * n, [None] * n, [None] * n, [None] * n
        carries = [carry_ref[h] for h in range(n_heads)]
        outs = [None] * n_heads

        def scores(idx):
            _, j, h = items[idx]
            zs[idx] = _dot_nt(pair_block(k_ref, j, h), q_heads[h])

        def drops(idx):
            t, _, _ = items[idx]
            z = zs[idx]
            drop = jnp.maximum(z, 0.0) + jnp.log2(1.0 + jnp.exp2(-jnp.abs(z)))
            if diag_first and t == 0:
                drop = jnp.where(strict, drop, 0.0)
            hi = drop.astype(BF16).astype(F32)
            sums = _dot(later, hi) + _dot(later, drop - hi)
            afters[idx] = sums[:tk]
            col_sums[idx] = sums[tk:tk + 1]
            log_betas[idx] = z - drop

        def weights(idx):
            t, j, h = items[idx]
            w = jnp.exp2(log_betas[idx] - (afters[idx] + carries[h]))
            if diag_first and t == 0:
                w = jnp.where(strict, w, 0.0)
            pv = _dot_tn(pair_block(v_ref, j, h), w.astype(BF16))
            lo_row = (h % HEAD_PAIR) * SB_HEAD_DIM
            pv = pv[lo_row:lo_row + SB_HEAD_DIM, :]
            outs[h] = pv if outs[h] is None else outs[h] + pv
            carries[h] = carries[h] + col_sums[idx]

        for step in range(n + SB_STAGE_LAG[1]):
            if step < n:
                scores(step)
            if 0 <= step - SB_STAGE_LAG[0] < n:
                drops(step - SB_STAGE_LAG[0])
            if 0 <= step - SB_STAGE_LAG[1] < n:
                weights(step - SB_STAGE_LAG[1])
        dead = None
        for h in range(n_heads):
            acc_ref[h] += outs[h]
            carry_ref[h] = carries[h]
            dead = carries[h] if dead is None else jnp.minimum(dead, carries[h])
        return jnp.min(dead) < SB_DEAD_LOG2_DROP

    alive0 = lax.cond(i > 0, lambda: visit([i, i - 1], True), lambda: visit([i], True))

    def cond(state):
        j, alive = state
        return jnp.logical_and(j >= 0, alive)

    def body(state):
        j, _ = state
        return j - 1, visit([j], False)

    lax.while_loop(cond, body, (i - 2, alive0))

    o_t = acc_ref[...].reshape(n_heads * SB_HEAD_DIM, tq)
    o_ref[0] = (o_t.T * _silu(g_ref[0])).astype(o_ref.dtype)


def _sb_attention(q, k, v, g):
    B, S, W = q.shape
    assert S % SB_TQ == 0 and SB_TQ == SB_TK
    wb = SB_PAIRS * LANES
    grid = (B, W // wb, S // SB_TQ)
    q_spec = pl.BlockSpec((1, SB_TQ, wb), lambda b, p, i: (b, i, p))
    kv_spec = pl.BlockSpec((1, S, wb), lambda b, p, i: (b, 0, p))
    return pl.pallas_call(
        _sb_kernel,
        grid=grid,
        in_specs=[q_spec, kv_spec, kv_spec, q_spec],
        out_specs=q_spec,
        out_shape=jax.ShapeDtypeStruct((B, S, W), BF16),
        scratch_shapes=[pltpu.VMEM((SB_PAIRS * HEAD_PAIR, SB_HEAD_DIM, SB_TQ), F32),
                        pltpu.VMEM((SB_PAIRS * HEAD_PAIR, 1, SB_TQ), F32)],
        compiler_params=pltpu.CompilerParams(
            dimension_semantics=("arbitrary", "arbitrary", "arbitrary"),
            vmem_limit_bytes=VMEM_LIMIT),
        name="sb_attn",
    )(q, k, v, g)


def _mla_kernel(q_ref, k_ref, v_ref, g_ref, o_ref, acc_ref, m_ref, s_ref):
    tq, tk = MLA_TQ, MLA_TK
    diag_tiles = tq // tk
    i = pl.program_id(2)
    neg = jnp.finfo(F32).min

    acc_ref[...] = jnp.zeros_like(acc_ref)
    m_ref[...] = jnp.full_like(m_ref, neg)

    def score(j, slot, diag=None):
        start = pl.multiple_of(j * tk, tk)
        lo = 0 if diag is None else diag * tk
        for hh in range(MLA_GROUP):
            q_t = q_ref[0, hh * LANES:(hh + 1) * LANES, lo:]
            k_blk = k_ref[0, pl.ds(start, tk), hh * LANES:(hh + 1) * LANES]
            s_ref[slot, hh, :, lo:] = _dot(k_blk, q_t)

    def consume(j, slot, diag=None):
        start = pl.multiple_of(j * tk, tk)
        lo = 0 if diag is None else diag * tk
        for hh in range(MLA_GROUP):
            v_t = v_ref[0, hh * MLA_V_ROWS:(hh + 1) * MLA_V_ROWS, pl.ds(start, tk)]
            s = s_ref[slot, hh, :, lo:]
            if diag is not None:
                causal = (lax.broadcasted_iota(jnp.int32, s.shape, 0)
                          <= lax.broadcasted_iota(jnp.int32, s.shape, 1))
                s = jnp.where(causal, s, neg)
            m_old = m_ref[hh, :, lo:]
            m_new = jnp.maximum(m_old, jnp.max(s, axis=0, keepdims=True))
            alpha = jnp.exp2(m_old[:1] - m_new[:1])
            p = jnp.exp2(s - m_new[:1])
            acc_ref[hh, :, lo:] = alpha * acc_ref[hh, :, lo:] + _dot(v_t, p.astype(BF16))
            m_ref[hh, :, lo:] = m_new

    base = i * diag_tiles
    score(0, 0)

    def two_tiles(t):
        score(t + 1, 1)
        consume(t, 0)
        score(t + 2, 0)
        consume(t + 1, 1)

    done = 0
    pairs = MLA_UNROLL
    while pairs >= 1:
        def group(jj, carry, pairs=pairs, done=done):
            for r in range(pairs):
                two_tiles(done + 2 * pairs * jj + 2 * r)
            return carry

        n_groups = (base - done) // (2 * pairs)
        lax.fori_loop(0, n_groups, group, 0)
        done = done + n_groups * 2 * pairs
        pairs //= 2
    for d in range(diag_tiles):
        if d + 1 < diag_tiles:
            score(base + d + 1, (d + 1) % 2, d + 1)
        consume(base + d, d % 2, d)

    o_t = jnp.concatenate(
        [acc_ref[hh, :MLA_V_DIM] / acc_ref[hh, MLA_V_DIM:MLA_V_DIM + 1] for hh in range(MLA_GROUP)],
        axis=0)
    o_ref[0] = (o_t.T * _silu(g_ref[0])).astype(o_ref.dtype)


def _mla_attention(q, k, v, g):
    B, S, W = g.shape
    tq = MLA_TQ
    group_rows = MLA_GROUP * MLA_V_ROWS
    group_out = MLA_GROUP * MLA_V_DIM
    assert S % tq == 0 and MLA_TQ % (2 * MLA_TK) == 0 and group_out % LANES == 0
    grid = (B, W // group_out, S // tq)
    return pl.pallas_call(
        _mla_kernel,
        grid=grid,
        in_specs=[pl.BlockSpec((1, MLA_GROUP * LANES, tq), lambda b, p, i: (b, p, i)),
                  pl.BlockSpec((1, S, MLA_GROUP * LANES), lambda b, p, i: (b, 0, p)),
                  pl.BlockSpec((1, group_rows, S), lambda b, p, i: (b, p, 0)),
                  pl.BlockSpec((1, tq, group_out), lambda b, p, i: (b, i, p))],
        out_specs=pl.BlockSpec((1, tq, group_out), lambda b, p, i: (b, i, p)),
        out_shape=jax.ShapeDtypeStruct((B, S, W), BF16),
        scratch_shapes=[pltpu.VMEM((MLA_GROUP, MLA_V_ROWS, tq), F32),
                        pltpu.VMEM((MLA_GROUP, 8, tq), F32),
                        pltpu.VMEM((2, MLA_GROUP, MLA_TK, tq), F32)],
        compiler_params=pltpu.CompilerParams(
            dimension_semantics=("arbitrary", "arbitrary", "arbitrary"),
            vmem_limit_bytes=VMEM_LIMIT),
        name="mla_attn",
    )(q, k, v, g)


def _outproj_kernel(msb_ref, mmla_ref, x_ref, mod_ref, w_ref, o_ref):
    y = _dot(msb_ref[0], w_ref[:SB_WIDTH, :]) + _dot(mmla_ref[0], w_ref[SB_WIDTH:, :])
    o_ref[0] = x_ref[0] + mod_ref[0, 2:3, :] * y


def _outproj(mixed_sb, mixed_mla, x, mod, w_out):
    B, S, D = x.shape
    ts = min(OUT_ROW_TILE, S)

    def rows(width):
        return pl.BlockSpec((1, ts, width), lambda b, i: (b, i, 0))

    return pl.pallas_call(
        _outproj_kernel,
        grid=(B, S // ts),
        in_specs=[rows(SB_WIDTH), rows(MLA_WIDTH), rows(D),
                  pl.BlockSpec((1, 3, D), lambda b, i: (b, 0, 0)),
                  pl.BlockSpec(w_out.shape, lambda b, i: (0, 0))],
        out_specs=rows(D),
        out_shape=jax.ShapeDtypeStruct((B, S, D), x.dtype),
        compiler_params=pltpu.CompilerParams(
            dimension_semantics=("arbitrary", "arbitrary"), vmem_limit_bytes=VMEM_LIMIT),
        name="outproj",
    )(mixed_sb, mixed_mla, x, mod, w_out)


def _pad_last(a, width):
    return jnp.pad(a, [(0, 0)] * (a.ndim - 1) + [(0, width - a.shape[-1])])


def _layer(x, c, tables, w_ada, b_ada, norm_w, w_in, q_lora_norm, w_uq, kv_lora_norm, w_ukv,
           q_head_norm, k_head_norm, w_out):
    B, S, D = x.shape
    c_pad = jnp.pad(c, ((0, 8 - B), (0, 0)))
    ada = _adaln(c_pad, w_ada, b_ada[None, :])[:B]
    mod = ada.reshape(B, 3, D)

    ts = min(ROW_TILE, S)
    w_in_b = w_in.astype(BF16)
    w_gm = w_in[:, _C_GMLA:].astype(BF16)
    w_uq_t = _pad_last(w_uq.reshape(Q_LORA_RANK, MLA_HEADS, MLA_QK_DIM), LANES)
    w_uq_t = w_uq_t.reshape(Q_LORA_RANK, MLA_PAD_WIDTH).T.astype(BF16)
    w_ukv_h = w_ukv.reshape(KV_LORA_RANK, MLA_HEADS, MLA_NOPE_DIM + MLA_V_DIM)
    w_uk_t = _pad_last(w_ukv_h[:, :, :MLA_NOPE_DIM], LANES)
    w_uk_t = w_uk_t.reshape(KV_LORA_RANK, MLA_PAD_WIDTH).T.astype(BF16)
    w_uv = w_ukv_h[:, :, MLA_NOPE_DIM:].reshape(KV_LORA_RANK, MLA_WIDTH).T.astype(BF16)
    gq = jnp.broadcast_to(_pad_last(q_head_norm, LANES)[:, None], (LANES, ts))
    gk = jnp.broadcast_to(_pad_last(k_head_norm, LANES)[:, None], (LANES, ts))

    (q_sb, k_sb, v_sb, g_sb, q_m, k_m, v_m, g_m) = _inproj(
        x, mod, norm_w[None, :], w_in_b, w_gm, q_lora_norm[None, :], w_uq_t, kv_lora_norm[None, :],
        w_uk_t, w_uv, gq, gk, *tables)

    mixed_sb = _sb_attention(q_sb, k_sb, v_sb, g_sb)
    mixed_mla = _mla_attention(q_m, k_m, v_m, g_m)
    return _outproj(mixed_sb, mixed_mla, x, mod, w_out.astype(BF16))


def _rope_tables(positions, dtype):
    inv_freq = ROPE_THETA ** (-jnp.arange(0, MLA_ROPE_DIM, 2, dtype=F32) / MLA_ROPE_DIM)
    ang = inv_freq[None, :, None] * positions.astype(F32)[:, None, :]
    return jnp.cos(ang).astype(dtype), jnp.sin(ang).astype(dtype)


def kernel(x, c, positions, w_ada, b_ada, norm_w, w_in, q_lora_norm, w_uq, kv_lora_norm, w_ukv,
           q_head_norm, k_head_norm, w_out):
    tables = _rope_tables(positions, x.dtype)
    for l in range(w_ada.shape[0]):
        x = _layer(x, c, tables, w_ada[l], b_ada[l], norm_w[l], w_in[l], q_lora_norm[l], w_uq[l],
                   kv_lora_norm[l], w_ukv[l], q_head_norm[l], k_head_norm[l], w_out[l])
    return x
```

```python
import functools
import math

import jax
import jax.numpy as jnp
from jax import lax
from jax.experimental import pallas as pl
from jax.experimental.pallas import tpu as pltpu

F32 = jnp.float32
BF16 = jnp.bfloat16

D_MODEL = 1024
SB_HEADS = 8
SB_HEAD_DIM = 64
SB_WIDTH = SB_HEADS * SB_HEAD_DIM
MLA_HEADS = 8
MLA_NOPE_DIM = 64
MLA_ROPE_DIM = 32
MLA_QK_DIM = MLA_NOPE_DIM + MLA_ROPE_DIM
MLA_V_DIM = 64
MLA_WIDTH = MLA_HEADS * MLA_V_DIM
MLA_V_ONES = 16
MLA_V_ROWS = MLA_V_DIM + MLA_V_ONES
Q_LORA_RANK = 384
KV_LORA_RANK = 256
ROPE_THETA = 10000.0
EPS = 1e-6

LANES = 128
HEAD_PAIR = 2
MLA_PAD_WIDTH = MLA_HEADS * LANES

_C_QSB = 0
_C_KSB = _C_QSB + SB_WIDTH
_C_VSB = _C_KSB + SB_WIDTH
_C_GSB = _C_VSB + SB_WIDTH
_C_CQ = _C_GSB + SB_WIDTH
_C_CKV = _C_CQ + Q_LORA_RANK
_C_KR = _C_CKV + KV_LORA_RANK
_C_GMLA = _C_KR + MLA_ROPE_DIM

ROW_TILE = 512
OUT_ROW_TILE = 1024
SB_TQ = 256
SB_TK = 256
SB_PAIRS = 4
SB_STAGE_LAG = (2, 4)
MLA_GROUP = 2
MLA_UNROLL = 4
MLA_TQ = 512
MLA_TK = 256
VMEM_LIMIT = 56 * 1024 * 1024

SB_DEAD_LOG2_DROP = 160.0
SB_Q_SCALE = math.log2(math.e) / math.sqrt(SB_HEAD_DIM)
MLA_Q_SCALE = math.log2(math.e) / math.sqrt(MLA_QK_DIM)


def _silu(g):
    return g * (1.0 / (1.0 + jnp.exp(-g)))


def _dot(a, b):
    return jnp.dot(a, b, preferred_element_type=F32)


def _dot_nt(a, b):
    return lax.dot_general(a, b, (((1,), (1,)), ((), ())), preferred_element_type=F32)


def _dot_tn(a, b):
    return lax.dot_general(a, b, (((0,), (0,)), ((), ())), preferred_element_type=F32)


def _adaln_kernel(c_ref, w_ref, b_ref, o_ref):
    o_ref[...] = jnp.dot(_silu(c_ref[...]), w_ref[...], preferred_element_type=F32,
                         precision=lax.Precision.HIGHEST) + b_ref[...]


def _adaln(c_pad, w_ada, b_ada):
    rows, d = c_pad.shape
    n = w_ada.shape[1]
    bn = 512
    return pl.pallas_call(
        _adaln_kernel,
        grid=(n // bn,),
        in_specs=[pl.BlockSpec((rows, d), lambda j: (0, 0)),
                  pl.BlockSpec((d, bn), lambda j: (0, j)),
                  pl.BlockSpec((1, bn), lambda j: (0, j))],
        out_specs=pl.BlockSpec((rows, bn), lambda j: (0, j)),
        out_shape=jax.ShapeDtypeStruct((rows, n), F32),
        name="adaln",
    )(c_pad, w_ada, b_ada)


def _rope_t(x1, x2, cos, sin):
    return x1 * cos - x2 * sin, x2 * cos + x1 * sin


def _inproj_kernel(x_ref, mod_ref, nw_ref, win_ref, wgm_ref, qln_ref, wuqt_ref, kvln_ref, wukt_ref, wuvt_ref,
                   gq_ref, gk_ref, cos_ref, sin_ref,
                   qsb_ref, ksb_ref, vsb_ref, gsb_ref, qmt_ref, km_ref, vmt_ref, gm_ref):
    half = MLA_ROPE_DIM // 2
    n0, n1, n2 = MLA_NOPE_DIM, MLA_NOPE_DIM + half, MLA_QK_DIM
    x = x_ref[0]
    ts = x.shape[0]
    y = x * lax.rsqrt(jnp.mean(x * x, axis=-1, keepdims=True) + EPS) * nw_ref[...]
    h = (y * (1.0 + mod_ref[0, 1:2, :]) + mod_ref[0, 0:1, :]).astype(BF16)

    def proj(lo, hi):
        return _dot(h, win_ref[:, lo:hi])

    def rms(t, w_ref):
        return t * lax.rsqrt(jnp.mean(t * t, axis=-1, keepdims=True) + EPS) * w_ref[...]

    cq = proj(_C_CQ, _C_CKV)
    ckv = proj(_C_CKV, _C_KR)
    kr = proj(_C_KR, _C_KR + LANES)
    qsb_ref[0] = (proj(_C_QSB, _C_KSB) * SB_Q_SCALE).astype(BF16)
    ksb_ref[0] = proj(_C_KSB, _C_VSB).astype(BF16)

    cqn = rms(cq, qln_ref).astype(BF16)
    ckvn = rms(ckv, kvln_ref).astype(BF16)
    q_t = _dot_nt(wuqt_ref[...], cqn)
    k_t = _dot_nt(wukt_ref[...], ckvn)
    v_t = _dot_nt(wuvt_ref[...], ckvn)
    vsb_ref[0] = proj(_C_VSB, _C_GSB).astype(BF16)
    gsb_ref[0] = proj(_C_GSB, _C_CQ)
    gm_ref[0] = _dot(h, wgm_ref[...])

    cos, sin = cos_ref[0], sin_ref[0]
    pad_rows = jnp.zeros((LANES - MLA_QK_DIM, ts), F32)

    gq = gq_ref[...]
    for hd in range(MLA_HEADS):
        blk = q_t[hd * LANES:(hd + 1) * LANES]
        inv = lax.rsqrt(jnp.sum(blk * blk, axis=0, keepdims=True) * (1.0 / MLA_QK_DIM) + EPS)
        g = blk * gq
        r1, r2 = _rope_t(g[n0:n1], g[n1:n2], cos, sin)
        out = jnp.concatenate([g[:n0], r1, r2, pad_rows], axis=0) * (inv * MLA_Q_SCALE)
        qmt_ref[0, hd * LANES:(hd + 1) * LANES, :] = out.astype(BF16)

    ones_rows = jnp.ones((MLA_V_ONES, ts), F32)
    v_rows = []
    for hd in range(MLA_HEADS):
        v_rows += [v_t[hd * MLA_V_DIM:(hd + 1) * MLA_V_DIM], ones_rows]
    vmt_ref[0] = jnp.concatenate(v_rows, axis=0).astype(BF16)

    gk = gk_ref[...]
    kr_t = kr.T
    x1, x2 = kr_t[:half], kr_t[half:MLA_ROPE_DIM]
    kr_ssq = jnp.sum(x1 * x1 + x2 * x2, axis=0, keepdims=True)
    r1, r2 = _rope_t(x1 * gk[n0:n1], x2 * gk[n1:n2], cos, sin)
    for hd in range(MLA_HEADS):
        kn = k_t[hd * LANES:hd * LANES + n0]
        ssq = jnp.sum(kn * kn, axis=0, keepdims=True) + kr_ssq
        inv = lax.rsqrt(ssq * (1.0 / MLA_QK_DIM) + EPS)
        out = jnp.concatenate([kn * gk[:n0], r1, r2, pad_rows], axis=0) * inv
        km_ref[0, :, hd * LANES:(hd + 1) * LANES] = out.astype(BF16).T


def _inproj(x, mod, norm_w, w_in_b, w_gm, qln, w_uq_t, kvln, w_uk_t, w_uv, gq, gk, cos_t, sin_t):
    B, S, D = x.shape
    ts = min(ROW_TILE, S)
    grid = (B, S // ts)

    def whole(a):
        return pl.BlockSpec(a.shape, lambda b, i: (0,) * a.ndim)

    def rows(width):
        return pl.BlockSpec((1, ts, width), lambda b, i: (b, i, 0))

    def cols(height):
        return pl.BlockSpec((1, height, ts), lambda b, i: (b, 0, i))

    out_shapes = (
        jax.ShapeDtypeStruct((B, S, SB_WIDTH), BF16),
        jax.ShapeDtypeStruct((B, S, SB_WIDTH), BF16),
        jax.ShapeDtypeStruct((B, S, SB_WIDTH), BF16),
        jax.ShapeDtypeStruct((B, S, SB_WIDTH), F32),
        jax.ShapeDtypeStruct((B, MLA_PAD_WIDTH, S), BF16),
        jax.ShapeDtypeStruct((B, S, MLA_PAD_WIDTH), BF16),
        jax.ShapeDtypeStruct((B, MLA_HEADS * MLA_V_ROWS, S), BF16),
        jax.ShapeDtypeStruct((B, S, MLA_WIDTH), F32),
    )
    out_specs = [rows(s.shape[-1]) for s in out_shapes]
    out_specs[4] = cols(MLA_PAD_WIDTH)
    out_specs[6] = cols(MLA_HEADS * MLA_V_ROWS)
    half = MLA_ROPE_DIM // 2
    return pl.pallas_call(
        _inproj_kernel,
        grid=grid,
        in_specs=[rows(D),
                  pl.BlockSpec((1, 3, D), lambda b, i: (b, 0, 0)),
                  whole(norm_w), whole(w_in_b), whole(w_gm), whole(qln), whole(w_uq_t), whole(kvln),
                  whole(w_uk_t), whole(w_uv), whole(gq), whole(gk),
                  cols(half), cols(half)],
        out_specs=out_specs,
        out_shape=out_shapes,
        compiler_params=pltpu.CompilerParams(
            dimension_semantics=("arbitrary", "arbitrary"), vmem_limit_bytes=VMEM_LIMIT),
        name="inproj",
    )(x, mod, norm_w, w_in_b, w_gm, qln, w_uq_t, kvln, w_uk_t, w_uv, gq, gk, cos_t, sin_t)


def _sb_kernel(q_ref, k_ref, v_ref, g_ref, o_ref, acc_ref, carry_ref):
    tq, tk = SB_TQ, SB_TK
    n_heads = SB_PAIRS * HEAD_PAIR
    i = pl.program_id(2)
    lane = lax.broadcasted_iota(jnp.int32, (tq, LANES), 1)
    q_heads = []
    for p in range(SB_PAIRS):
        q_pair = q_ref[0, :, p * LANES:(p + 1) * LANES]
        for hh in range(HEAD_PAIR):
            q_heads.append(jnp.where((lane // SB_HEAD_DIM) == hh, q_pair, jnp.zeros_like(q_pair)))

    strict = (lax.broadcasted_iota(jnp.int32, (tk, tq), 0)
              < lax.broadcasted_iota(jnp.int32, (tk, tq), 1))
    l_row = lax.broadcasted_iota(jnp.int32, (tk + 8, tk), 0)
    l_col = lax.broadcasted_iota(jnp.int32, (tk + 8, tk), 1)
    later = jnp.where(jnp.logical_or(l_col > l_row, l_row >= tk), 1.0, 0.0).astype(F32)

    acc_ref[...] = jnp.zeros_like(acc_ref)
    carry_ref[...] = jnp.zeros_like(carry_ref)

    def visit(tiles, diag_first):
        items = [(t, j, h) for t, j in enumerate(tiles) for h in range(n_heads)]

        def pair_block(ref, j, h):
            p = h // HEAD_PAIR
            return ref[0, pl.ds(pl.multiple_of(j * tk, tk), tk), p * LANES:(p + 1) * LANES]

        n = len(items)
        zs, log_betas, afters, col_sums = [None] * n, [None] * n, [None] * n, [None] * n
        carries = [carry_ref[h] for h in range(n_heads)]
        outs = [None] * n_heads

        def scores(idx):
            _, j, h = items[idx]
            zs[idx] = _dot_nt(pair_block(k_ref, j, h), q_heads[h])

        def drops(idx):
            t, _, _ = items[idx]
            z = zs[idx]
            drop = jnp.maximum(z, 0.0) + jnp.log2(1.0 + jnp.exp2(-jnp.abs(z)))
            if diag_first and t == 0:
                drop = jnp.where(strict, drop, 0.0)
            hi = drop.astype(BF16).astype(F32)
            sums = _dot(later, hi) + _dot(later, drop - hi)
            afters[idx] = sums[:tk]
            col_sums[idx] = sums[tk:tk + 1]
            log_betas[idx] = z - drop

        def weights(idx):
            t, j, h = items[idx]
            w = jnp.exp2(log_betas[idx] - (afters[idx] + carries[h]))
            if diag_first and t == 0:
                w = jnp.where(strict, w, 0.0)
            pv = _dot_tn(pair_block(v_ref, j, h), w.astype(BF16))
            lo_row = (h % HEAD_PAIR) * SB_HEAD_DIM
            pv = pv[lo_row:lo_row + SB_HEAD_DIM, :]
            outs[h] = pv if outs[h] is None else outs[h] + pv
            carries[h] = carries[h] + col_sums[idx]

        for step in range(n + SB_STAGE_LAG[1]):
            if step < n:
                scores(step)
            if 0 <= step - SB_STAGE_LAG[0] < n:
                drops(step - SB_STAGE_LAG[0])
            if 0 <= step - SB_STAGE_LAG[1] < n:
                weights(step - SB_STAGE_LAG[1])
        dead = None
        for h in range(n_heads):
            acc_ref[h] += outs[h]
            carry_ref[h] = carries[h]
            dead = carries[h] if dead is None else jnp.minimum(dead, carries[h])
        return jnp.min(dead) < SB_DEAD_LOG2_DROP

    alive0 = lax.cond(i > 0, lambda: visit([i, i - 1], True), lambda: visit([i], True))

    def cond(state):
        j, alive = state
        return jnp.logical_and(j >= 0, alive)

    def body(state):
        j, _ = state
        return j - 1, visit([j], False)

    lax.while_loop(cond, body, (i - 2, alive0))

    o_t = acc_ref[...].reshape(n_heads * SB_HEAD_DIM, tq)
    o_ref[0] = (o_t.T * _silu(g_ref[0])).astype(o_ref.dtype)


def _sb_attention(q, k, v, g):
    B, S, W = q.shape
    assert S % SB_TQ == 0 and SB_TQ == SB_TK
    wb = SB_PAIRS * LANES
    grid = (B, W // wb, S // SB_TQ)
    q_spec = pl.BlockSpec((1, SB_TQ, wb), lambda b, p, i: (b, i, p))
    kv_spec = pl.BlockSpec((1, S, wb), lambda b, p, i: (b, 0, p))
    return pl.pallas_call(
        _sb_kernel,
        grid=grid,
        in_specs=[q_spec, kv_spec, kv_spec, q_spec],
        out_specs=q_spec,
        out_shape=jax.ShapeDtypeStruct((B, S, W), BF16),
        scratch_shapes=[pltpu.VMEM((SB_PAIRS * HEAD_PAIR, SB_HEAD_DIM, SB_TQ), F32),
                        pltpu.VMEM((SB_PAIRS * HEAD_PAIR, 1, SB_TQ), F32)],
        compiler_params=pltpu.CompilerParams(
            dimension_semantics=("arbitrary", "arbitrary", "arbitrary"),
            vmem_limit_bytes=VMEM_LIMIT),
        name="sb_attn",
    )(q, k, v, g)


def _mla_kernel(q_ref, qnext_ref, k_ref, v_ref, g_ref, o_ref, acc_ref, m_ref, s_ref):
    tq, tk = MLA_TQ, MLA_TK
    diag_tiles = tq // tk
    i = pl.program_id(2)
    neg = jnp.finfo(F32).min

    acc_ref[...] = jnp.zeros_like(acc_ref)
    m_ref[...] = jnp.full_like(m_ref, neg)

    def score(j, slot, diag=None, queries=q_ref):
        start = pl.multiple_of(j * tk, tk)
        lo = 0 if diag is None else diag * tk
        for hh in range(MLA_GROUP):
            q_t = queries[0, hh * LANES:(hh + 1) * LANES, lo:]
            k_blk = k_ref[0, pl.ds(start, tk), hh * LANES:(hh + 1) * LANES]
            s_ref[slot, hh, :, lo:] = _dot(k_blk, q_t)

    def consume(j, slot, diag=None):
        start = pl.multiple_of(j * tk, tk)
        lo = 0 if diag is None else diag * tk
        for hh in range(MLA_GROUP):
            v_t = v_ref[0, hh * MLA_V_ROWS:(hh + 1) * MLA_V_ROWS, pl.ds(start, tk)]
            s = s_ref[slot, hh, :, lo:]
            if diag is not None:
                causal = (lax.broadcasted_iota(jnp.int32, s.shape, 0)
                          <= lax.broadcasted_iota(jnp.int32, s.shape, 1))
                s = jnp.where(causal, s, neg)
            m_old = m_ref[hh, :, lo:]
            m_new = jnp.maximum(m_old, jnp.max(s, axis=0, keepdims=True))
            alpha = jnp.exp2(m_old[:1] - m_new[:1])
            p = jnp.exp2(s - m_new[:1])
            acc_ref[hh, :, lo:] = alpha * acc_ref[hh, :, lo:] + _dot(v_t, p.astype(BF16))
            m_ref[hh, :, lo:] = m_new

    base = i * diag_tiles

    @pl.when(i == 0)
    def _():
        score(0, 0)

    def two_tiles(t):
        score(t + 1, 1)
        consume(t, 0)
        score(t + 2, 0)
        consume(t + 1, 1)

    done = 0
    pairs = MLA_UNROLL
    while pairs >= 1:
        def group(jj, carry, pairs=pairs, done=done):
            for r in range(pairs):
                two_tiles(done + 2 * pairs * jj + 2 * r)
            return carry

        n_groups = (base - done) // (2 * pairs)
        lax.fori_loop(0, n_groups, group, 0)
        done = done + n_groups * 2 * pairs
        pairs //= 2
    for d in range(diag_tiles):
        if d + 1 < diag_tiles:
            score(base + d + 1, (d + 1) % 2, d + 1)
        consume(base + d, d % 2, d)
        if d == 0:
            score(0, 0, queries=qnext_ref)

    o_t = jnp.concatenate(
        [acc_ref[hh, :MLA_V_DIM] / acc_ref[hh, MLA_V_DIM:MLA_V_DIM + 1] for hh in range(MLA_GROUP)],
        axis=0)
    o_ref[0] = (o_t.T * _silu(g_ref[0])).astype(o_ref.dtype)


def _mla_attention(q, k, v, g):
    B, S, W = g.shape
    tq = MLA_TQ
    group_rows = MLA_GROUP * MLA_V_ROWS
    group_out = MLA_GROUP * MLA_V_DIM
    assert S % tq == 0 and MLA_TQ % (2 * MLA_TK) == 0 and group_out % LANES == 0
    grid = (B, W // group_out, S // tq)
    return pl.pallas_call(
        _mla_kernel,
        grid=grid,
        in_specs=[pl.BlockSpec((1, MLA_GROUP * LANES, tq), lambda b, p, i: (b, p, i)),
                  pl.BlockSpec((1, MLA_GROUP * LANES, tq),
                               lambda b, p, i: (b, p, jnp.minimum(i + 1, S // tq - 1))),
                  pl.BlockSpec((1, S, MLA_GROUP * LANES), lambda b, p, i: (b, 0, p)),
                  pl.BlockSpec((1, group_rows, S), lambda b, p, i: (b, p, 0)),
                  pl.BlockSpec((1, tq, group_out), lambda b, p, i: (b, i, p))],
        out_specs=pl.BlockSpec((1, tq, group_out), lambda b, p, i: (b, i, p)),
        out_shape=jax.ShapeDtypeStruct((B, S, W), BF16),
        scratch_shapes=[pltpu.VMEM((MLA_GROUP, MLA_V_ROWS, tq), F32),
                        pltpu.VMEM((MLA_GROUP, 8, tq), F32),
                        pltpu.VMEM((2, MLA_GROUP, MLA_TK, tq), F32)],
        compiler_params=pltpu.CompilerParams(
            dimension_semantics=("arbitrary", "arbitrary", "arbitrary"),
            vmem_limit_bytes=VMEM_LIMIT),
        name="mla_attn",
    )(q, q, k, v, g)


def _outproj_kernel(msb_ref, mmla_ref, x_ref, mod_ref, w_ref, o_ref):
    y = _dot(msb_ref[0], w_ref[:SB_WIDTH, :]) + _dot(mmla_ref[0], w_ref[SB_WIDTH:, :])
    o_ref[0] = x_ref[0] + mod_ref[0, 2:3, :] * y


def _outproj(mixed_sb, mixed_mla, x, mod, w_out):
    B, S, D = x.shape
    ts = min(OUT_ROW_TILE, S)

    def rows(width):
        return pl.BlockSpec((1, ts, width), lambda b, i: (b, i, 0))

    return pl.pallas_call(
        _outproj_kernel,
        grid=(B, S // ts),
        in_specs=[rows(SB_WIDTH), rows(MLA_WIDTH), rows(D),
                  pl.BlockSpec((1, 3, D), lambda b, i: (b, 0, 0)),
                  pl.BlockSpec(w_out.shape, lambda b, i: (0, 0))],
        out_specs=rows(D),
        out_shape=jax.ShapeDtypeStruct((B, S, D), x.dtype),
        compiler_params=pltpu.CompilerParams(
            dimension_semantics=("arbitrary", "arbitrary"), vmem_limit_bytes=VMEM_LIMIT),
        name="outproj",
    )(mixed_sb, mixed_mla, x, mod, w_out)


def _pad_last(a, width):
    return jnp.pad(a, [(0, 0)] * (a.ndim - 1) + [(0, width - a.shape[-1])])


def _layer(x, c, tables, w_ada, b_ada, norm_w, w_in, q_lora_norm, w_uq, kv_lora_norm, w_ukv,
           q_head_norm, k_head_norm, w_out):
    B, S, D = x.shape
    c_pad = jnp.pad(c, ((0, 8 - B), (0, 0)))
    ada = _adaln(c_pad, w_ada, b_ada[None, :])[:B]
    mod = ada.reshape(B, 3, D)

    ts = min(ROW_TILE, S)
    w_in_b = w_in.astype(BF16)
    w_gm = w_in[:, _C_GMLA:].astype(BF16)
    w_uq_t = _pad_last(w_uq.reshape(Q_LORA_RANK, MLA_HEADS, MLA_QK_DIM), LANES)
    w_uq_t = w_uq_t.reshape(Q_LORA_RANK, MLA_PAD_WIDTH).T.astype(BF16)
    w_ukv_h = w_ukv.reshape(KV_LORA_RANK, MLA_HEADS, MLA_NOPE_DIM + MLA_V_DIM)
    w_uk_t = _pad_last(w_ukv_h[:, :, :MLA_NOPE_DIM], LANES)
    w_uk_t = w_uk_t.reshape(KV_LORA_RANK, MLA_PAD_WIDTH).T.astype(BF16)
    w_uv = w_ukv_h[:, :, MLA_NOPE_DIM:].reshape(KV_LORA_RANK, MLA_WIDTH).T.astype(BF16)
    gq = jnp.broadcast_to(_pad_last(q_head_norm, LANES)[:, None], (LANES, ts))
    gk = jnp.broadcast_to(_pad_last(k_head_norm, LANES)[:, None], (LANES, ts))

    (q_sb, k_sb, v_sb, g_sb, q_m, k_m, v_m, g_m) = _inproj(
        x, mod, norm_w[None, :], w_in_b, w_gm, q_lora_norm[None, :], w_uq_t, kv_lora_norm[None, :],
        w_uk_t, w_uv, gq, gk, *tables)

    mixed_sb = _sb_attention(q_sb, k_sb, v_sb, g_sb)
    mixed_mla = _mla_attention(q_m, k_m, v_m, g_m)
    return _outproj(mixed_sb, mixed_mla, x, mod, w_out.astype(BF16))


def _rope_tables(positions, dtype):
    inv_freq = ROPE_THETA ** (-jnp.arange(0, MLA_ROPE_DIM, 2, dtype=F32) / MLA_ROPE_DIM)
    ang = inv_freq[None, :, None] * positions.astype(F32)[:, None, :]
    return jnp.cos(ang).astype(dtype), jnp.sin(ang).astype(dtype)


def kernel(x, c, positions, w_ada, b_ada, norm_w, w_in, q_lora_norm, w_uq, kv_lora_norm, w_ukv,
           q_head_norm, k_head_norm, w_out):
    tables = _rope_tables(positions, x.dtype)
    for l in range(w_ada.shape[0]):
        x = _layer(x, c, tables, w_ada[l], b_ada[l], norm_w[l], w_in[l], q_lora_norm[l], w_uq[l],
                   kv_lora_norm[l], w_ukv[l], q_head_norm[l], k_head_norm[l], w_out[l])
    return x
```

```python
import functools
import math

import jax
import jax.numpy as jnp
from jax import lax
from jax.experimental import pallas as pl
from jax.experimental.pallas import tpu as pltpu

F32 = jnp.float32
BF16 = jnp.bfloat16

D_MODEL = 1024
SB_HEADS = 8
SB_HEAD_DIM = 64
SB_WIDTH = SB_HEADS * SB_HEAD_DIM
MLA_HEADS = 8
MLA_NOPE_DIM = 64
MLA_ROPE_DIM = 32
MLA_QK_DIM = MLA_NOPE_DIM + MLA_ROPE_DIM
MLA_V_DIM = 64
MLA_WIDTH = MLA_HEADS * MLA_V_DIM
MLA_V_ONES = 16
MLA_V_ROWS = MLA_V_DIM + MLA_V_ONES
Q_LORA_RANK = 384
KV_LORA_RANK = 256
ROPE_THETA = 10000.0
EPS = 1e-6

LANES = 128
HEAD_PAIR = 2
MLA_PAD_WIDTH = MLA_HEADS * LANES

_C_QSB = 0
_C_KSB = _C_QSB + SB_WIDTH
_C_VSB = _C_KSB + SB_WIDTH
_C_GSB = _C_VSB + SB_WIDTH
_C_CQ = _C_GSB + SB_WIDTH
_C_CKV = _C_CQ + Q_LORA_RANK
_C_KR = _C_CKV + KV_LORA_RANK
_C_GMLA = _C_KR + MLA_ROPE_DIM

ROW_TILE = 512
OUT_ROW_TILE = 1024
SB_TQ = 256
SB_TK = 256
SB_PAIRS = 4
SB_SUBS = 2
SB_STAGE_LAG = (2, 4)
MLA_GROUP = 2
MLA_UNROLL = 4
MLA_TQ = 512
MLA_TK = 256
VMEM_LIMIT = 56 * 1024 * 1024

SB_DEAD_LOG2_DROP = 160.0
SB_Q_SCALE = math.log2(math.e) / math.sqrt(SB_HEAD_DIM)
MLA_Q_SCALE = math.log2(math.e) / math.sqrt(MLA_QK_DIM)


def _silu(g):
    return g * (1.0 / (1.0 + jnp.exp(-g)))


def _dot(a, b):
    return jnp.dot(a, b, preferred_element_type=F32)


def _dot_nt(a, b):
    return lax.dot_general(a, b, (((1,), (1,)), ((), ())), preferred_element_type=F32)


def _dot_tn(a, b):
    return lax.dot_general(a, b, (((0,), (0,)), ((), ())), preferred_element_type=F32)


def _adaln_kernel(c_ref, w_ref, b_ref, o_ref):
    o_ref[...] = jnp.dot(_silu(c_ref[...]), w_ref[...], preferred_element_type=F32,
                         precision=lax.Precision.HIGHEST) + b_ref[...]


def _adaln(c_pad, w_ada, b_ada):
    rows, d = c_pad.shape
    n = w_ada.shape[1]
    bn = 512
    return pl.pallas_call(
        _adaln_kernel,
        grid=(n // bn,),
        in_specs=[pl.BlockSpec((rows, d), lambda j: (0, 0)),
                  pl.BlockSpec((d, bn), lambda j: (0, j)),
                  pl.BlockSpec((1, bn), lambda j: (0, j))],
        out_specs=pl.BlockSpec((rows, bn), lambda j: (0, j)),
        out_shape=jax.ShapeDtypeStruct((rows, n), F32),
        name="adaln",
    )(c_pad, w_ada, b_ada)


def _rope_t(x1, x2, cos, sin):
    return x1 * cos - x2 * sin, x2 * cos + x1 * sin


def _inproj_kernel(x_ref, mod_ref, nw_ref, win_ref, wgm_ref, qln_ref, wuqt_ref, kvln_ref, wukt_ref, wuvt_ref,
                   gq_ref, gk_ref, cos_ref, sin_ref,
                   qsb_ref, ksb_ref, vsb_ref, gsb_ref, qmt_ref, km_ref, vmt_ref, gm_ref):
    half = MLA_ROPE_DIM // 2
    n0, n1, n2 = MLA_NOPE_DIM, MLA_NOPE_DIM + half, MLA_QK_DIM
    x = x_ref[0]
    ts = x.shape[0]
    y = x * lax.rsqrt(jnp.mean(x * x, axis=-1, keepdims=True) + EPS) * nw_ref[...]
    h = (y * (1.0 + mod_ref[0, 1:2, :]) + mod_ref[0, 0:1, :]).astype(BF16)

    def proj(lo, hi):
        return _dot(h, win_ref[:, lo:hi])

    def rms(t, w_ref):
        return t * lax.rsqrt(jnp.mean(t * t, axis=-1, keepdims=True) + EPS) * w_ref[...]

    cq = proj(_C_CQ, _C_CKV)
    ckv = proj(_C_CKV, _C_KR)
    kr = proj(_C_KR, _C_KR + LANES)
    qsb_ref[0] = (proj(_C_QSB, _C_KSB) * SB_Q_SCALE).astype(BF16)
    ksb_ref[0] = proj(_C_KSB, _C_VSB).astype(BF16)

    cqn = rms(cq, qln_ref).astype(BF16)
    ckvn = rms(ckv, kvln_ref).astype(BF16)
    q_t = _dot_nt(wuqt_ref[...], cqn)
    k_t = _dot_nt(wukt_ref[...], ckvn)
    v_t = _dot_nt(wuvt_ref[...], ckvn)
    vsb_ref[0] = proj(_C_VSB, _C_GSB).astype(BF16)
    gsb_ref[0] = proj(_C_GSB, _C_CQ)
    gm_ref[0] = _dot(h, wgm_ref[...])

    cos, sin = cos_ref[0], sin_ref[0]
    pad_rows = jnp.zeros((LANES - MLA_QK_DIM, ts), F32)

    gq = gq_ref[...]
    for hd in range(MLA_HEADS):
        blk = q_t[hd * LANES:(hd + 1) * LANES]
        inv = lax.rsqrt(jnp.sum(blk * blk, axis=0, keepdims=True) * (1.0 / MLA_QK_DIM) + EPS)
        g = blk * gq
        r1, r2 = _rope_t(g[n0:n1], g[n1:n2], cos, sin)
        out = jnp.concatenate([g[:n0], r1, r2, pad_rows], axis=0) * (inv * MLA_Q_SCALE)
        qmt_ref[0, hd * LANES:(hd + 1) * LANES, :] = out.astype(BF16)

    ones_rows = jnp.ones((MLA_V_ONES, ts), F32)
    v_rows = []
    for hd in range(MLA_HEADS):
        v_rows += [v_t[hd * MLA_V_DIM:(hd + 1) * MLA_V_DIM], ones_rows]
    vmt_ref[0] = jnp.concatenate(v_rows, axis=0).astype(BF16)

    gk = gk_ref[...]
    kr_t = kr.T
    x1, x2 = kr_t[:half], kr_t[half:MLA_ROPE_DIM]
    kr_ssq = jnp.sum(x1 * x1 + x2 * x2, axis=0, keepdims=True)
    r1, r2 = _rope_t(x1 * gk[n0:n1], x2 * gk[n1:n2], cos, sin)
    for hd in range(MLA_HEADS):
        kn = k_t[hd * LANES:hd * LANES + n0]
        ssq = jnp.sum(kn * kn, axis=0, keepdims=True) + kr_ssq
        inv = lax.rsqrt(ssq * (1.0 / MLA_QK_DIM) + EPS)
        out = jnp.concatenate([kn * gk[:n0], r1, r2, pad_rows], axis=0) * inv
        km_ref[0, :, hd * LANES:(hd + 1) * LANES] = out.astype(BF16).T


def _inproj(x, mod, norm_w, w_in_b, w_gm, qln, w_uq_t, kvln, w_uk_t, w_uv, gq, gk, cos_t, sin_t):
    B, S, D = x.shape
    ts = min(ROW_TILE, S)
    grid = (B, S // ts)

    def whole(a):
        return pl.BlockSpec(a.shape, lambda b, i: (0,) * a.ndim)

    def rows(width):
        return pl.BlockSpec((1, ts, width), lambda b, i: (b, i, 0))

    def cols(height):
        return pl.BlockSpec((1, height, ts), lambda b, i: (b, 0, i))

    out_shapes = (
        jax.ShapeDtypeStruct((B, S, SB_WIDTH), BF16),
        jax.ShapeDtypeStruct((B, S, SB_WIDTH), BF16),
        jax.ShapeDtypeStruct((B, S, SB_WIDTH), BF16),
        jax.ShapeDtypeStruct((B, S, SB_WIDTH), F32),
        jax.ShapeDtypeStruct((B, MLA_PAD_WIDTH, S), BF16),
        jax.ShapeDtypeStruct((B, S, MLA_PAD_WIDTH), BF16),
        jax.ShapeDtypeStruct((B, MLA_HEADS * MLA_V_ROWS, S), BF16),
        jax.ShapeDtypeStruct((B, S, MLA_WIDTH), F32),
    )
    out_specs = [rows(s.shape[-1]) for s in out_shapes]
    out_specs[4] = cols(MLA_PAD_WIDTH)
    out_specs[6] = cols(MLA_HEADS * MLA_V_ROWS)
    half = MLA_ROPE_DIM // 2
    return pl.pallas_call(
        _inproj_kernel,
        grid=grid,
        in_specs=[rows(D),
                  pl.BlockSpec((1, 3, D), lambda b, i: (b, 0, 0)),
                  whole(norm_w), whole(w_in_b), whole(w_gm), whole(qln), whole(w_uq_t), whole(kvln),
                  whole(w_uk_t), whole(w_uv), whole(gq), whole(gk),
                  cols(half), cols(half)],
        out_specs=out_specs,
        out_shape=out_shapes,
        compiler_params=pltpu.CompilerParams(
            dimension_semantics=("arbitrary", "arbitrary"), vmem_limit_bytes=VMEM_LIMIT),
        name="inproj",
    )(x, mod, norm_w, w_in_b, w_gm, qln, w_uq_t, kvln, w_uk_t, w_uv, gq, gk, cos_t, sin_t)


def _sb_kernel(q_ref, k_ref, v_ref, g_ref, o_ref, acc_ref, carry_ref):
    tq, tk = SB_TQ, SB_TK
    n_heads = SB_PAIRS * HEAD_PAIR
    i = pl.program_id(2)
    lane = lax.broadcasted_iota(jnp.int32, (tq, LANES), 1)
    q_heads = []
    for sub in range(SB_SUBS):
        heads = []
        for p in range(SB_PAIRS):
            q_pair = q_ref[0, sub * tq:(sub + 1) * tq, p * LANES:(p + 1) * LANES]
            for hh in range(HEAD_PAIR):
                heads.append(jnp.where((lane // SB_HEAD_DIM) == hh, q_pair, jnp.zeros_like(q_pair)))
        q_heads.append(heads)

    strict = (lax.broadcasted_iota(jnp.int32, (tk, tq), 0)
              < lax.broadcasted_iota(jnp.int32, (tk, tq), 1))
    l_row = lax.broadcasted_iota(jnp.int32, (tk + 8, tk), 0)
    l_col = lax.broadcasted_iota(jnp.int32, (tk + 8, tk), 1)
    later = jnp.where(jnp.logical_or(l_col > l_row, l_row >= tk), 1.0, 0.0).astype(F32)

    acc_ref[...] = jnp.zeros_like(acc_ref)
    carry_ref[...] = jnp.zeros_like(carry_ref)

    def visit(plan, diag_first):
        depth = max(len(tiles) for _, tiles in plan)
        items = [(sub, t, tiles[t], h) for t in range(depth) for sub, tiles in plan
                 if t < len(tiles) for h in range(n_heads)]

        def pair_block(ref, j, h):
            p = h // HEAD_PAIR
            return ref[0, pl.ds(pl.multiple_of(j * tk, tk), tk), p * LANES:(p + 1) * LANES]

        n = len(items)
        zs, log_betas, afters, col_sums = [None] * n, [None] * n, [None] * n, [None] * n
        carries = {(sub, h): carry_ref[sub, h] for sub, _ in plan for h in range(n_heads)}
        outs = {key: None for key in carries}

        def scores(idx):
            sub, _, j, h = items[idx]
            zs[idx] = _dot_nt(pair_block(k_ref, j, h), q_heads[sub][h])

        def drops(idx):
            _, t, _, _ = items[idx]
            z = zs[idx]
            drop = jnp.maximum(z, 0.0) + jnp.log2(1.0 + jnp.exp2(-jnp.abs(z)))
            if diag_first and t == 0:
                drop = jnp.where(strict, drop, 0.0)
            hi = drop.astype(BF16).astype(F32)
            sums = _dot(later, hi) + _dot(later, drop - hi)
            afters[idx] = sums[:tk]
            col_sums[idx] = sums[tk:tk + 1]
            log_betas[idx] = z - drop

        def weights(idx):
            sub, t, j, h = items[idx]
            w = jnp.exp2(log_betas[idx] - (afters[idx] + carries[sub, h]))
            if diag_first and t == 0:
                w = jnp.where(strict, w, 0.0)
            pv = _dot_tn(pair_block(v_ref, j, h), w.astype(BF16))
            lo_row = (h % HEAD_PAIR) * SB_HEAD_DIM
            pv = pv[lo_row:lo_row + SB_HEAD_DIM, :]
            outs[sub, h] = pv if outs[sub, h] is None else outs[sub, h] + pv
            carries[sub, h] = carries[sub, h] + col_sums[idx]

        for step in range(n + SB_STAGE_LAG[1]):
            if step < n:
                scores(step)
            if 0 <= step - SB_STAGE_LAG[0] < n:
                drops(step - SB_STAGE_LAG[0])
            if 0 <= step - SB_STAGE_LAG[1] < n:
                weights(step - SB_STAGE_LAG[1])
        alive = []
        for sub, _ in plan:
            dead = None
            for h in range(n_heads):
                acc_ref[sub, h] += outs[sub, h]
                carry_ref[sub, h] = carries[sub, h]
                dead = carries[sub, h] if dead is None else jnp.minimum(dead, carries[sub, h])
            alive.append(jnp.min(dead) < SB_DEAD_LOG2_DROP)
        return alive

    first = SB_SUBS * i

    def usual_step():
        return visit([(sub, [first + sub, first + sub - 1]) for sub in range(SB_SUBS)], True)

    def first_step():
        return visit([(0, [0])] + [(sub, [sub, sub - 1]) for sub in range(1, SB_SUBS)], True)

    alive = lax.cond(i > 0, usual_step, first_step)

    for sub in range(SB_SUBS):
        def cond(state):
            j, still_alive = state
            return jnp.logical_and(j >= 0, still_alive)

        def body(state, sub=sub):
            j, _ = state
            return j - 1, visit([(sub, [j])], False)[0]

        lax.while_loop(cond, body, (first + sub - 2, alive[sub]))

    for sub in range(SB_SUBS):
        rows = slice(sub * tq, (sub + 1) * tq)
        o_t = acc_ref[sub].reshape(n_heads * SB_HEAD_DIM, tq)
        o_ref[0, rows, :] = (o_t.T * _silu(g_ref[0, rows, :])).astype(o_ref.dtype)


def _sb_attention(q, k, v, g):
    B, S, W = q.shape
    rows = SB_SUBS * SB_TQ
    assert S % rows == 0 and SB_TQ == SB_TK
    wb = SB_PAIRS * LANES
    n_heads = SB_PAIRS * HEAD_PAIR
    grid = (B, W // wb, S // rows)
    q_spec = pl.BlockSpec((1, rows, wb), lambda b, p, i: (b, i, p))
    kv_spec = pl.BlockSpec((1, S, wb), lambda b, p, i: (b, 0, p))
    return pl.pallas_call(
        _sb_kernel,
        grid=grid,
        in_specs=[q_spec, kv_spec, kv_spec, q_spec],
        out_specs=q_spec,
        out_shape=jax.ShapeDtypeStruct((B, S, W), BF16),
        scratch_shapes=[pltpu.VMEM((SB_SUBS, n_heads, SB_HEAD_DIM, SB_TQ), F32),
                        pltpu.VMEM((SB_SUBS, n_heads, 1, SB_TQ), F32)],
        compiler_params=pltpu.CompilerParams(
            dimension_semantics=("arbitrary", "arbitrary", "arbitrary"),
            vmem_limit_bytes=VMEM_LIMIT),
        name="sb_attn",
    )(q, k, v, g)


def _mla_kernel(q_ref, qnext_ref, k_ref, v_ref, g_ref, o_ref, acc_ref, m_ref, s_ref):
    tq, tk = MLA_TQ, MLA_TK
    diag_tiles = tq // tk
    i = pl.program_id(2)
    neg = jnp.finfo(F32).min

    acc_ref[...] = jnp.zeros_like(acc_ref)
    m_ref[...] = jnp.full_like(m_ref, neg)

    def score(j, slot, diag=None, queries=q_ref):
        start = pl.multiple_of(j * tk, tk)
        lo = 0 if diag is None else diag * tk
        for hh in range(MLA_GROUP):
            q_t = queries[0, hh * LANES:(hh + 1) * LANES, lo:]
            k_blk = k_ref[0, pl.ds(start, tk), hh * LANES:(hh + 1) * LANES]
            s_ref[slot, hh, :, lo:] = _dot(k_blk, q_t)

    def consume(j, slot, diag=None):
        start = pl.multiple_of(j * tk, tk)
        lo = 0 if diag is None else diag * tk
        for hh in range(MLA_GROUP):
            v_t = v_ref[0, hh * MLA_V_ROWS:(hh + 1) * MLA_V_ROWS, pl.ds(start, tk)]
            s = s_ref[slot, hh, :, lo:]
            if diag is not None:
                causal = (lax.broadcasted_iota(jnp.int32, s.shape, 0)
                          <= lax.broadcasted_iota(jnp.int32, s.shape, 1))
                s = jnp.where(causal, s, neg)
            m_old = m_ref[hh, :, lo:]
            m_new = jnp.maximum(m_old, jnp.max(s, axis=0, keepdims=True))
            alpha = jnp.exp2(m_old[:1] - m_new[:1])
            p = jnp.exp2(s - m_new[:1])
            acc_ref[hh, :, lo:] = alpha * acc_ref[hh, :, lo:] + _dot(v_t, p.astype(BF16))
            m_ref[hh, :, lo:] = m_new

    base = i * diag_tiles

    @pl.when(i == 0)
    def _():
        score(0, 0)

    def two_tiles(t):
        score(t + 1, 1)
        consume(t, 0)
        score(t + 2, 0)
        consume(t + 1, 1)

    done = 0
    pairs = MLA_UNROLL
    while pairs >= 1:
        def group(jj, carry, pairs=pairs, done=done):
            for r in range(pairs):
                two_tiles(done + 2 * pairs * jj + 2 * r)
            return carry

        n_groups = (base - done) // (2 * pairs)
        lax.fori_loop(0, n_groups, group, 0)
        done = done + n_groups * 2 * pairs
        pairs //= 2
    for d in range(diag_tiles):
        if d + 1 < diag_tiles:
            score(base + d + 1, (d + 1) % 2, d + 1)
        consume(base + d, d % 2, d)
        if d == 0:
            score(0, 0, queries=qnext_ref)

    o_t = jnp.concatenate(
        [acc_ref[hh, :MLA_V_DIM] / acc_ref[hh, MLA_V_DIM:MLA_V_DIM + 1] for hh in range(MLA_GROUP)],
        axis=0)
    o_ref[0] = (o_t.T * _silu(g_ref[0])).astype(o_ref.dtype)


def _mla_attention(q, k, v, g):
    B, S, W = g.shape
    tq = MLA_TQ
    group_rows = MLA_GROUP * MLA_V_ROWS
    group_out = MLA_GROUP * MLA_V_DIM
    assert S % tq == 0 and MLA_TQ % (2 * MLA_TK) == 0 and group_out % LANES == 0
    grid = (B, W // group_out, S // tq)
    return pl.pallas_call(
        _mla_kernel,
        grid=grid,
        in_specs=[pl.BlockSpec((1, MLA_GROUP * LANES, tq), lambda b, p, i: (b, p, i)),
                  pl.BlockSpec((1, MLA_GROUP * LANES, tq),
                               lambda b, p, i: (b, p, jnp.minimum(i + 1, S // tq - 1))),
                  pl.BlockSpec((1, S, MLA_GROUP * LANES), lambda b, p, i: (b, 0, p)),
                  pl.BlockSpec((1, group_rows, S), lambda b, p, i: (b, p, 0)),
                  pl.BlockSpec((1, tq, group_out), lambda b, p, i: (b, i, p))],
        out_specs=pl.BlockSpec((1, tq, group_out), lambda b, p, i: (b, i, p)),
        out_shape=jax.ShapeDtypeStruct((B, S, W), BF16),
        scratch_shapes=[pltpu.VMEM((MLA_GROUP, MLA_V_ROWS, tq), F32),
                        pltpu.VMEM((MLA_GROUP, 8, tq), F32),
                        pltpu.VMEM((2, MLA_GROUP, MLA_TK, tq), F32)],
        compiler_params=pltpu.CompilerParams(
            dimension_semantics=("arbitrary", "arbitrary", "arbitrary"),
            vmem_limit_bytes=VMEM_LIMIT),
        name="mla_attn",
    )(q, q, k, v, g)


def _outproj_kernel(msb_ref, mmla_ref, x_ref, mod_ref, w_ref, o_ref):
    y = _dot(msb_ref[0], w_ref[:SB_WIDTH, :]) + _dot(mmla_ref[0], w_ref[SB_WIDTH:, :])
    o_ref[0] = x_ref[0] + mod_ref[0, 2:3, :] * y


def _outproj(mixed_sb, mixed_mla, x, mod, w_out):
    B, S, D = x.shape
    ts = min(OUT_ROW_TILE, S)

    def rows(width):
        return pl.BlockSpec((1, ts, width), lambda b, i: (b, i, 0))

    return pl.pallas_call(
        _outproj_kernel,
        grid=(B, S // ts),
        in_specs=[rows(SB_WIDTH), rows(MLA_WIDTH), rows(D),
                  pl.BlockSpec((1, 3, D), lambda b, i: (b, 0, 0)),
                  pl.BlockSpec(w_out.shape, lambda b, i: (0, 0))],
        out_specs=rows(D),
        out_shape=jax.ShapeDtypeStruct((B, S, D), x.dtype),
        compiler_params=pltpu.CompilerParams(
            dimension_semantics=("arbitrary", "arbitrary"), vmem_limit_bytes=VMEM_LIMIT),
        name="outproj",
    )(mixed_sb, mixed_mla, x, mod, w_out)


def _pad_last(a, width):
    return jnp.pad(a, [(0, 0)] * (a.ndim - 1) + [(0, width - a.shape[-1])])


def _layer(x, c, tables, w_ada, b_ada, norm_w, w_in, q_lora_norm, w_uq, kv_lora_norm, w_ukv,
           q_head_norm, k_head_norm, w_out):
    B, S, D = x.shape
    c_pad = jnp.pad(c, ((0, 8 - B), (0, 0)))
    ada = _adaln(c_pad, w_ada, b_ada[None, :])[:B]
    mod = ada.reshape(B, 3, D)

    ts = min(ROW_TILE, S)
    w_in_b = w_in.astype(BF16)
    w_gm = w_in[:, _C_GMLA:].astype(BF16)
    w_uq_t = _pad_last(w_uq.reshape(Q_LORA_RANK, MLA_HEADS, MLA_QK_DIM), LANES)
    w_uq_t = w_uq_t.reshape(Q_LORA_RANK, MLA_PAD_WIDTH).T.astype(BF16)
    w_ukv_h = w_ukv.reshape(KV_LORA_RANK, MLA_HEADS, MLA_NOPE_DIM + MLA_V_DIM)
    w_uk_t = _pad_last(w_ukv_h[:, :, :MLA_NOPE_DIM], LANES)
    w_uk_t = w_uk_t.reshape(KV_LORA_RANK, MLA_PAD_WIDTH).T.astype(BF16)
    w_uv = w_ukv_h[:, :, MLA_NOPE_DIM:].reshape(KV_LORA_RANK, MLA_WIDTH).T.astype(BF16)
    gq = jnp.broadcast_to(_pad_last(q_head_norm, LANES)[:, None], (LANES, ts))
    gk = jnp.broadcast_to(_pad_last(k_head_norm, LANES)[:, None], (LANES, ts))

    (q_sb, k_sb, v_sb, g_sb, q_m, k_m, v_m, g_m) = _inproj(
        x, mod, norm_w[None, :], w_in_b, w_gm, q_lora_norm[None, :], w_uq_t, kv_lora_norm[None, :],
        w_uk_t, w_uv, gq, gk, *tables)

    mixed_sb = _sb_attention(q_sb, k_sb, v_sb, g_sb)
    mixed_mla = _mla_attention(q_m, k_m, v_m, g_m)
    return _outproj(mixed_sb, mixed_mla, x, mod, w_out.astype(BF16))


def _rope_tables(positions, dtype):
    inv_freq = ROPE_THETA ** (-jnp.arange(0, MLA_ROPE_DIM, 2, dtype=F32) / MLA_ROPE_DIM)
    ang = inv_freq[None, :, None] * positions.astype(F32)[:, None, :]
    return jnp.cos(ang).astype(dtype), jnp.sin(ang).astype(dtype)


def kernel(x, c, positions, w_ada, b_ada, norm_w, w_in, q_lora_norm, w_uq, kv_lora_norm, w_ukv,
           q_head_norm, k_head_norm, w_out):
    tables = _rope_tables(positions, x.dtype)
    for l in range(w_ada.shape[0]):
        x = _layer(x, c, tables, w_ada[l], b_ada[l], norm_w[l], w_in[l], q_lora_norm[l], w_uq[l],
                   kv_lora_norm[l], w_ukv[l], q_head_norm[l], k_head_norm[l], w_out[l])
    return x
```

```python
import functools
import math

import jax
import jax.numpy as jnp
from jax import lax
from jax.experimental import pallas as pl
from jax.experimental.pallas import tpu as pltpu

F32 = jnp.float32
BF16 = jnp.bfloat16

D_MODEL = 1024
SB_HEADS = 8
SB_HEAD_DIM = 64
SB_WIDTH = SB_HEADS * SB_HEAD_DIM
MLA_HEADS = 8
MLA_NOPE_DIM = 64
MLA_ROPE_DIM = 32
MLA_QK_DIM = MLA_NOPE_DIM + MLA_ROPE_DIM
MLA_V_DIM = 64
MLA_WIDTH = MLA_HEADS * MLA_V_DIM
MLA_V_ONES = 16
MLA_V_ROWS = MLA_V_DIM + MLA_V_ONES
Q_LORA_RANK = 384
KV_LORA_RANK = 256
ROPE_THETA = 10000.0
EPS = 1e-6

LANES = 128
HEAD_PAIR = 2
MLA_PAD_WIDTH = MLA_HEADS * LANES

_C_QSB = 0
_C_KSB = _C_QSB + SB_WIDTH
_C_VSB = _C_KSB + SB_WIDTH
_C_GSB = _C_VSB + SB_WIDTH
_C_CQ = _C_GSB + SB_WIDTH
_C_CKV = _C_CQ + Q_LORA_RANK
_C_KR = _C_CKV + KV_LORA_RANK
_C_GMLA = _C_KR + MLA_ROPE_DIM

ROW_TILE = 512
OUT_ROW_TILE = 1024
SB_TQ = 256
SB_TK = 256
SB_SUBS = 2
SB_STAGE_LAG = (2, 4)
MLA_GROUP = 2
MLA_UNROLL = 4
MLA_TQ = 512
MLA_TK = 256
VMEM_LIMIT = 56 * 1024 * 1024

SB_DEAD_LOG2_DROP = 160.0
SB_Q_SCALE = math.log2(math.e) / math.sqrt(SB_HEAD_DIM)
MLA_Q_SCALE = math.log2(math.e) / math.sqrt(MLA_QK_DIM)


def _silu(g):
    return g * (1.0 / (1.0 + jnp.exp(-g)))


def _dot(a, b):
    return jnp.dot(a, b, preferred_element_type=F32)


def _dot_nt(a, b):
    return lax.dot_general(a, b, (((1,), (1,)), ((), ())), preferred_element_type=F32)


def _dot_tn(a, b):
    return lax.dot_general(a, b, (((0,), (0,)), ((), ())), preferred_element_type=F32)


def _adaln_kernel(c_ref, w_ref, b_ref, o_ref):
    o_ref[...] = jnp.dot(_silu(c_ref[...]), w_ref[...], preferred_element_type=F32,
                         precision=lax.Precision.HIGHEST) + b_ref[...]


def _adaln(c_pad, w_ada, b_ada):
    rows, d = c_pad.shape
    n = w_ada.shape[1]
    bn = 512
    return pl.pallas_call(
        _adaln_kernel,
        grid=(n // bn,),
        in_specs=[pl.BlockSpec((rows, d), lambda j: (0, 0)),
                  pl.BlockSpec((d, bn), lambda j: (0, j)),
                  pl.BlockSpec((1, bn), lambda j: (0, j))],
        out_specs=pl.BlockSpec((rows, bn), lambda j: (0, j)),
        out_shape=jax.ShapeDtypeStruct((rows, n), F32),
        name="adaln",
    )(c_pad, w_ada, b_ada)


def _rope_t(x1, x2, cos, sin):
    return x1 * cos - x2 * sin, x2 * cos + x1 * sin


def _inproj_kernel(x_ref, mod_ref, nw_ref, win_ref, wgm_ref, qln_ref, wuqt_ref, kvln_ref, wukt_ref, wuvt_ref,
                   gq_ref, gk_ref, cos_ref, sin_ref,
                   qsb_ref, ksb_ref, vsb_ref, gsb_ref, qmt_ref, km_ref, vmt_ref, gm_ref):
    half = MLA_ROPE_DIM // 2
    n0, n1, n2 = MLA_NOPE_DIM, MLA_NOPE_DIM + half, MLA_QK_DIM
    x = x_ref[0]
    ts = x.shape[0]
    y = x * lax.rsqrt(jnp.mean(x * x, axis=-1, keepdims=True) + EPS) * nw_ref[...]
    h = (y * (1.0 + mod_ref[0, 1:2, :]) + mod_ref[0, 0:1, :]).astype(BF16)

    def proj(lo, hi):
        return _dot(h, win_ref[:, lo:hi])

    def rms(t, w_ref):
        return t * lax.rsqrt(jnp.mean(t * t, axis=-1, keepdims=True) + EPS) * w_ref[...]

    cq = proj(_C_CQ, _C_CKV)
    ckv = proj(_C_CKV, _C_KR)
    kr = proj(_C_KR, _C_KR + LANES)
    qsb_ref[0] = (proj(_C_QSB, _C_KSB) * SB_Q_SCALE).astype(BF16)
    ksb_ref[0] = proj(_C_KSB, _C_VSB).astype(BF16)

    cqn = rms(cq, qln_ref).astype(BF16)
    ckvn = rms(ckv, kvln_ref).astype(BF16)
    q_t = _dot_nt(wuqt_ref[...], cqn)
    k_t = _dot_nt(wukt_ref[...], ckvn)
    v_t = _dot_nt(wuvt_ref[...], ckvn)
    vsb_ref[0] = proj(_C_VSB, _C_GSB).astype(BF16)
    gsb_ref[0] = proj(_C_GSB, _C_CQ)
    gm_ref[0] = _dot(h, wgm_ref[...])

    cos, sin = cos_ref[0], sin_ref[0]
    pad_rows = jnp.zeros((LANES - MLA_QK_DIM, ts), F32)

    gq = gq_ref[...]
    for hd in range(MLA_HEADS):
        blk = q_t[hd * LANES:(hd + 1) * LANES]
        inv = lax.rsqrt(jnp.sum(blk * blk, axis=0, keepdims=True) * (1.0 / MLA_QK_DIM) + EPS)
        g = blk * gq
        r1, r2 = _rope_t(g[n0:n1], g[n1:n2], cos, sin)
        out = jnp.concatenate([g[:n0], r1, r2, pad_rows], axis=0) * (inv * MLA_Q_SCALE)
        qmt_ref[0, hd * LANES:(hd + 1) * LANES, :] = out.astype(BF16)

    ones_rows = jnp.ones((MLA_V_ONES, ts), F32)
    v_rows = []
    for hd in range(MLA_HEADS):
        v_rows += [v_t[hd * MLA_V_DIM:(hd + 1) * MLA_V_DIM], ones_rows]
    vmt_ref[0] = jnp.concatenate(v_rows, axis=0).astype(BF16)

    gk = gk_ref[...]
    kr_t = kr.T
    x1, x2 = kr_t[:half], kr_t[half:MLA_ROPE_DIM]
    kr_ssq = jnp.sum(x1 * x1 + x2 * x2, axis=0, keepdims=True)
    r1, r2 = _rope_t(x1 * gk[n0:n1], x2 * gk[n1:n2], cos, sin)
    for hd in range(MLA_HEADS):
        kn = k_t[hd * LANES:hd * LANES + n0]
        ssq = jnp.sum(kn * kn, axis=0, keepdims=True) + kr_ssq
        inv = lax.rsqrt(ssq * (1.0 / MLA_QK_DIM) + EPS)
        out = jnp.concatenate([kn * gk[:n0], r1, r2, pad_rows], axis=0) * inv
        km_ref[0, :, hd * LANES:(hd + 1) * LANES] = out.astype(BF16).T


def _inproj(x, mod, norm_w, w_in_b, w_gm, qln, w_uq_t, kvln, w_uk_t, w_uv, gq, gk, cos_t, sin_t):
    B, S, D = x.shape
    ts = min(ROW_TILE, S)
    grid = (B, S // ts)

    def whole(a):
        return pl.BlockSpec(a.shape, lambda b, i: (0,) * a.ndim)

    def rows(width):
        return pl.BlockSpec((1, ts, width), lambda b, i: (b, i, 0))

    def cols(height):
        return pl.BlockSpec((1, height, ts), lambda b, i: (b, 0, i))

    out_shapes = (
        jax.ShapeDtypeStruct((B, S, SB_WIDTH), BF16),
        jax.ShapeDtypeStruct((B, S, SB_WIDTH), BF16),
        jax.ShapeDtypeStruct((B, S, SB_WIDTH), BF16),
        jax.ShapeDtypeStruct((B, S, SB_WIDTH), F32),
        jax.ShapeDtypeStruct((B, MLA_PAD_WIDTH, S), BF16),
        jax.ShapeDtypeStruct((B, S, MLA_PAD_WIDTH), BF16),
        jax.ShapeDtypeStruct((B, MLA_HEADS * MLA_V_ROWS, S), BF16),
        jax.ShapeDtypeStruct((B, S, MLA_WIDTH), F32),
    )
    out_specs = [rows(s.shape[-1]) for s in out_shapes]
    out_specs[4] = cols(MLA_PAD_WIDTH)
    out_specs[6] = cols(MLA_HEADS * MLA_V_ROWS)
    half = MLA_ROPE_DIM // 2
    return pl.pallas_call(
        _inproj_kernel,
        grid=grid,
        in_specs=[rows(D),
                  pl.BlockSpec((1, 3, D), lambda b, i: (b, 0, 0)),
                  whole(norm_w), whole(w_in_b), whole(w_gm), whole(qln), whole(w_uq_t), whole(kvln),
                  whole(w_uk_t), whole(w_uv), whole(gq), whole(gk),
                  cols(half), cols(half)],
        out_specs=out_specs,
        out_shape=out_shapes,
        compiler_params=pltpu.CompilerParams(
            dimension_semantics=("arbitrary", "arbitrary"), vmem_limit_bytes=VMEM_LIMIT),
        name="inproj",
    )(x, mod, norm_w, w_in_b, w_gm, qln, w_uq_t, kvln, w_uk_t, w_uv, gq, gk, cos_t, sin_t)


def _attn_kernel(q_ref, qnext_ref, k_ref, v_ref, g_ref, qs_ref, ks_ref, vs_ref, gs_ref,
                 o_ref, os_ref, acc_ref, m_ref, s_ref, sacc_ref, carry_ref):
    tq, tk = MLA_TQ, MLA_TK
    diag_tiles = tq // tk
    i = pl.program_id(2)
    neg = jnp.finfo(F32).min

    acc_ref[...] = jnp.zeros_like(acc_ref)
    m_ref[...] = jnp.full_like(m_ref, neg)
    sacc_ref[...] = jnp.zeros_like(sacc_ref)
    carry_ref[...] = jnp.zeros_like(carry_ref)

    def score(j, slot, diag=None, queries=q_ref):
        start = pl.multiple_of(j * tk, tk)
        lo = 0 if diag is None else diag * tk
        for hh in range(MLA_GROUP):
            q_t = queries[0, hh * LANES:(hh + 1) * LANES, lo:]
            k_blk = k_ref[0, pl.ds(start, tk), hh * LANES:(hh + 1) * LANES]
            s_ref[slot, hh, :, lo:] = _dot(k_blk, q_t)

    def consume(j, slot, diag=None):
        start = pl.multiple_of(j * tk, tk)
        lo = 0 if diag is None else diag * tk
        for hh in range(MLA_GROUP):
            v_t = v_ref[0, hh * MLA_V_ROWS:(hh + 1) * MLA_V_ROWS, pl.ds(start, tk)]
            s = s_ref[slot, hh, :, lo:]
            if diag is not None:
                causal = (lax.broadcasted_iota(jnp.int32, s.shape, 0)
                          <= lax.broadcasted_iota(jnp.int32, s.shape, 1))
                s = jnp.where(causal, s, neg)
            m_old = m_ref[hh, :, lo:]
            m_new = jnp.maximum(m_old, jnp.max(s, axis=0, keepdims=True))
            alpha = jnp.exp2(m_old[:1] - m_new[:1])
            p = jnp.exp2(s - m_new[:1])
            acc_ref[hh, :, lo:] = alpha * acc_ref[hh, :, lo:] + _dot(v_t, p.astype(BF16))
            m_ref[hh, :, lo:] = m_new

    base = i * diag_tiles

    @pl.when(i == 0)
    def _():
        score(0, 0)

    def two_tiles(t):
        score(t + 1, 1)
        consume(t, 0)
        score(t + 2, 0)
        consume(t + 1, 1)

    done = 0
    pairs = MLA_UNROLL
    while pairs >= 1:
        def group(jj, carry, pairs=pairs, done=done):
            for r in range(pairs):
                two_tiles(done + 2 * pairs * jj + 2 * r)
            return carry

        n_groups = (base - done) // (2 * pairs)
        lax.fori_loop(0, n_groups, group, 0)
        done = done + n_groups * 2 * pairs
        pairs //= 2

    def mla_finish():
        o_t = jnp.concatenate(
            [acc_ref[hh, :MLA_V_DIM] / acc_ref[hh, MLA_V_DIM:MLA_V_DIM + 1] for hh in range(MLA_GROUP)],
            axis=0)
        o_ref[0] = (o_t.T * _silu(g_ref[0])).astype(o_ref.dtype)

    assert diag_tiles == 2
    mla_pieces = [
        lambda: score(base + 1, 1, 1),
        lambda: consume(base, 0, 0),
        lambda: score(0, 0, queries=qnext_ref),
        lambda: consume(base + 1, 1, 1),
        mla_finish,
    ]

    ts = SB_TQ
    lane = lax.broadcasted_iota(jnp.int32, (ts, LANES), 1)
    q_heads = []
    for sub in range(SB_SUBS):
        q_pair = qs_ref[0, sub * ts:(sub + 1) * ts, :]
        q_heads.append([jnp.where((lane // SB_HEAD_DIM) == hh, q_pair, jnp.zeros_like(q_pair))
                        for hh in range(HEAD_PAIR)])

    strict = (lax.broadcasted_iota(jnp.int32, (ts, ts), 0)
              < lax.broadcasted_iota(jnp.int32, (ts, ts), 1))
    l_row = lax.broadcasted_iota(jnp.int32, (ts + 8, ts), 0)
    l_col = lax.broadcasted_iota(jnp.int32, (ts + 8, ts), 1)
    later = jnp.where(jnp.logical_or(l_col > l_row, l_row >= ts), 1.0, 0.0).astype(F32)

    def visit(plan, diag_first, interleave=()):
        depth = max(len(tiles) for _, tiles, _ in plan)
        items = [(sub, t, tiles[t], valid[t], h) for t in range(depth) for sub, tiles, valid in plan
                 if t < len(tiles) for h in range(HEAD_PAIR)]

        def tile_block(ref, j):
            return ref[0, pl.ds(pl.multiple_of(j * ts, ts), ts), :]

        n = len(items)
        zs, log_betas, afters, col_sums = [None] * n, [None] * n, [None] * n, [None] * n
        carries = {(sub, h): carry_ref[sub, h] for sub, _, _ in plan for h in range(HEAD_PAIR)}
        outs = {key: None for key in carries}

        def scores(idx):
            sub, _, j, _, h = items[idx]
            zs[idx] = _dot_nt(tile_block(ks_ref, j), q_heads[sub][h])

        def drops(idx):
            _, t, _, valid, _ = items[idx]
            z = zs[idx]
            drop = jnp.maximum(z, 0.0) + jnp.log2(1.0 + jnp.exp2(-jnp.abs(z)))
            if diag_first and t == 0:
                drop = jnp.where(strict, drop, 0.0)
            if valid is not None:
                drop = jnp.where(valid, drop, 0.0)
            hi = drop.astype(BF16).astype(F32)
            sums = _dot(later, hi) + _dot(later, drop - hi)
            afters[idx] = sums[:ts]
            col_sums[idx] = sums[ts:ts + 1]
            log_betas[idx] = z - drop

        def weights(idx):
            sub, t, j, valid, h = items[idx]
            w = jnp.exp2(log_betas[idx] - (afters[idx] + carries[sub, h]))
            if diag_first and t == 0:
                w = jnp.where(strict, w, 0.0)
            if valid is not None:
                w = jnp.where(valid, w, 0.0)
            pv = _dot_tn(tile_block(vs_ref, j), w.astype(BF16))
            pv = pv[h * SB_HEAD_DIM:(h + 1) * SB_HEAD_DIM, :]
            outs[sub, h] = pv if outs[sub, h] is None else outs[sub, h] + pv
            carries[sub, h] = carries[sub, h] + col_sums[idx]

        steps = n + SB_STAGE_LAG[1]
        extras = list(interleave)
        for step in range(steps):
            if step < n:
                scores(step)
            if 0 <= step - SB_STAGE_LAG[0] < n:
                drops(step - SB_STAGE_LAG[0])
            if 0 <= step - SB_STAGE_LAG[1] < n:
                weights(step - SB_STAGE_LAG[1])
            while extras and (len(interleave) - len(extras)) * steps <= step * len(interleave):
                extras.pop(0)()
        for extra in extras:
            extra()
        alive = []
        for sub, _, _ in plan:
            dead = None
            for h in range(HEAD_PAIR):
                sacc_ref[sub, h] += outs[sub, h]
                carry_ref[sub, h] = carries[sub, h]
                dead = carries[sub, h] if dead is None else jnp.minimum(dead, carries[sub, h])
            alive.append(jnp.min(dead) < SB_DEAD_LOG2_DROP)
        return alive

    first = SB_SUBS * i
    plan = []
    for sub in range(SB_SUBS):
        qi = first + sub
        plan.append((sub, [qi, jnp.maximum(qi - 1, 0)], [None, (qi > 0) if sub == 0 else None]))
    alive = visit(plan, True, interleave=mla_pieces)

    for sub in range(SB_SUBS):
        def cond(state):
            j, still_alive = state
            return jnp.logical_and(j >= 0, still_alive)

        def body(state, sub=sub):
            j, _ = state
            return j - 1, visit([(sub, [j], [None])], False)[0]

        lax.while_loop(cond, body, (first + sub - 2, alive[sub]))

    for sub in range(SB_SUBS):
        rows = slice(sub * ts, (sub + 1) * ts)
        o_t = sacc_ref[sub].reshape(HEAD_PAIR * SB_HEAD_DIM, ts)
        os_ref[0, rows, :] = (o_t.T * _silu(gs_ref[0, rows, :])).astype(os_ref.dtype)


def _attention(q_m, k_m, v_m, g_m, q_sb, k_sb, v_sb, g_sb):
    B, S, W = g_m.shape
    tq = MLA_TQ
    group_rows = MLA_GROUP * MLA_V_ROWS
    group_out = MLA_GROUP * MLA_V_DIM
    assert S % tq == 0 and MLA_TQ == 2 * MLA_TK and group_out == LANES
    assert SB_SUBS * SB_TQ == tq and SB_TQ == SB_TK and MLA_GROUP == HEAD_PAIR
    grid = (B, W // group_out, S // tq)
    tile_spec = pl.BlockSpec((1, tq, LANES), lambda b, p, i: (b, i, p))
    seq_spec = pl.BlockSpec((1, S, LANES), lambda b, p, i: (b, 0, p))
    out_shape = jax.ShapeDtypeStruct((B, S, W), BF16)
    mixed_m, mixed_sb = pl.pallas_call(
        _attn_kernel,
        grid=grid,
        in_specs=[pl.BlockSpec((1, MLA_GROUP * LANES, tq), lambda b, p, i: (b, p, i)),
                  pl.BlockSpec((1, MLA_GROUP * LANES, tq),
                               lambda b, p, i: (b, p, jnp.minimum(i + 1, S // tq - 1))),
                  pl.BlockSpec((1, S, MLA_GROUP * LANES), lambda b, p, i: (b, 0, p)),
                  pl.BlockSpec((1, group_rows, S), lambda b, p, i: (b, p, 0)),
                  tile_spec,
                  tile_spec, seq_spec, seq_spec, tile_spec],
        out_specs=[tile_spec, tile_spec],
        out_shape=[out_shape, out_shape],
        scratch_shapes=[pltpu.VMEM((MLA_GROUP, MLA_V_ROWS, tq), F32),
                        pltpu.VMEM((MLA_GROUP, 8, tq), F32),
                        pltpu.VMEM((2, MLA_GROUP, MLA_TK, tq), F32),
                        pltpu.VMEM((SB_SUBS, HEAD_PAIR, SB_HEAD_DIM, SB_TQ), F32),
                        pltpu.VMEM((SB_SUBS, HEAD_PAIR, 1, SB_TQ), F32)],
        compiler_params=pltpu.CompilerParams(
            dimension_semantics=("arbitrary", "arbitrary", "arbitrary"),
            vmem_limit_bytes=VMEM_LIMIT),
        name="attn",
    )(q_m, q_m, k_m, v_m, g_m, q_sb, k_sb, v_sb, g_sb)
    return mixed_sb, mixed_m


def _outproj_kernel(msb_ref, mmla_ref, x_ref, mod_ref, w_ref, o_ref):
    y = _dot(msb_ref[0], w_ref[:SB_WIDTH, :]) + _dot(mmla_ref[0], w_ref[SB_WIDTH:, :])
    o_ref[0] = x_ref[0] + mod_ref[0, 2:3, :] * y


def _outproj(mixed_sb, mixed_mla, x, mod, w_out):
    B, S, D = x.shape
    ts = min(OUT_ROW_TILE, S)

    def rows(width):
        return pl.BlockSpec((1, ts, width), lambda b, i: (b, i, 0))

    return pl.pallas_call(
        _outproj_kernel,
        grid=(B, S // ts),
        in_specs=[rows(SB_WIDTH), rows(MLA_WIDTH), rows(D),
                  pl.BlockSpec((1, 3, D), lambda b, i: (b, 0, 0)),
                  pl.BlockSpec(w_out.shape, lambda b, i: (0, 0))],
        out_specs=rows(D),
        out_shape=jax.ShapeDtypeStruct((B, S, D), x.dtype),
        compiler_params=pltpu.CompilerParams(
            dimension_semantics=("arbitrary", "arbitrary"), vmem_limit_bytes=VMEM_LIMIT),
        name="outproj",
    )(mixed_sb, mixed_mla, x, mod, w_out)


def _pad_last(a, width):
    return jnp.pad(a, [(0, 0)] * (a.ndim - 1) + [(0, width - a.shape[-1])])


def _layer(x, c, tables, w_ada, b_ada, norm_w, w_in, q_lora_norm, w_uq, kv_lora_norm, w_ukv,
           q_head_norm, k_head_norm, w_out):
    B, S, D = x.shape
    c_pad = jnp.pad(c, ((0, 8 - B), (0, 0)))
    ada = _adaln(c_pad, w_ada, b_ada[None, :])[:B]
    mod = ada.reshape(B, 3, D)

    ts = min(ROW_TILE, S)
    w_in_b = w_in.astype(BF16)
    w_gm = w_in[:, _C_GMLA:].astype(BF16)
    w_uq_t = _pad_last(w_uq.reshape(Q_LORA_RANK, MLA_HEADS, MLA_QK_DIM), LANES)
    w_uq_t = w_uq_t.reshape(Q_LORA_RANK, MLA_PAD_WIDTH).T.astype(BF16)
    w_ukv_h = w_ukv.reshape(KV_LORA_RANK, MLA_HEADS, MLA_NOPE_DIM + MLA_V_DIM)
    w_uk_t = _pad_last(w_ukv_h[:, :, :MLA_NOPE_DIM], LANES)
    w_uk_t = w_uk_t.reshape(KV_LORA_RANK, MLA_PAD_WIDTH).T.astype(BF16)
    w_uv = w_ukv_h[:, :, MLA_NOPE_DIM:].reshape(KV_LORA_RANK, MLA_WIDTH).T.astype(BF16)
    gq = jnp.broadcast_to(_pad_last(q_head_norm, LANES)[:, None], (LANES, ts))
    gk = jnp.broadcast_to(_pad_last(k_head_norm, LANES)[:, None], (LANES, ts))

    (q_sb, k_sb, v_sb, g_sb, q_m, k_m, v_m, g_m) = _inproj(
        x, mod, norm_w[None, :], w_in_b, w_gm, q_lora_norm[None, :], w_uq_t, kv_lora_norm[None, :],
        w_uk_t, w_uv, gq, gk, *tables)

    mixed_sb, mixed_mla = _attention(q_m, k_m, v_m, g_m, q_sb, k_sb, v_sb, g_sb)
    return _outproj(mixed_sb, mixed_mla, x, mod, w_out.astype(BF16))


def _rope_tables(positions, dtype):
    inv_freq = ROPE_THETA ** (-jnp.arange(0, MLA_ROPE_DIM, 2, dtype=F32) / MLA_ROPE_DIM)
    ang = inv_freq[None, :, None] * positions.astype(F32)[:, None, :]
    return jnp.cos(ang).astype(dtype), jnp.sin(ang).astype(dtype)


def kernel(x, c, positions, w_ada, b_ada, norm_w, w_in, q_lora_norm, w_uq, kv_lora_norm, w_ukv,
           q_head_norm, k_head_norm, w_out):
    tables = _rope_tables(positions, x.dtype)
    for l in range(w_ada.shape[0]):
        x = _layer(x, c, tables, w_ada[l], b_ada[l], norm_w[l], w_in[l], q_lora_norm[l], w_uq[l],
                   kv_lora_norm[l], w_ukv[l], q_head_norm[l], k_head_norm[l], w_out[l])
    return x
```

```python
import functools
import math

import jax
import jax.numpy as jnp
from jax import lax
from jax.experimental import pallas as pl
from jax.experimental.pallas import tpu as pltpu

F32 = jnp.float32
BF16 = jnp.bfloat16

D_MODEL = 1024
SB_HEADS = 8
SB_HEAD_DIM = 64
SB_WIDTH = SB_HEADS * SB_HEAD_DIM
MLA_HEADS = 8
MLA_NOPE_DIM = 64
MLA_ROPE_DIM = 32
MLA_QK_DIM = MLA_NOPE_DIM + MLA_ROPE_DIM
MLA_V_DIM = 64
MLA_WIDTH = MLA_HEADS * MLA_V_DIM
MLA_V_ONES = 16
MLA_V_ROWS = MLA_V_DIM + MLA_V_ONES
Q_LORA_RANK = 384
KV_LORA_RANK = 256
ROPE_THETA = 10000.0
EPS = 1e-6

LANES = 128
HEAD_PAIR = 2
MLA_PAD_WIDTH = MLA_HEADS * LANES

_C_QSB = 0
_C_KSB = _C_QSB + SB_WIDTH
_C_VSB = _C_KSB + SB_WIDTH
_C_GSB = _C_VSB + SB_WIDTH
_C_CQ = _C_GSB + SB_WIDTH
_C_CKV = _C_CQ + Q_LORA_RANK
_C_KR = _C_CKV + KV_LORA_RANK
_C_GMLA = _C_KR + MLA_ROPE_DIM

ROW_TILE = 512
OUT_ROW_TILE = 1024
SB_TQ = 256
SB_TK = 256
SB_PAIRS = 4
SB_SUBS = 2
SB_FIRST_VISIT = 2
SB_STAGE_LAG = (2, 4)
MLA_GROUP = 2
MLA_UNROLL = 4
MLA_TQ = 512
MLA_TK = 256
VMEM_LIMIT = 56 * 1024 * 1024

SB_DEAD_LOG2_DROP = 160.0
SB_Q_SCALE = math.log2(math.e) / math.sqrt(SB_HEAD_DIM)
MLA_Q_SCALE = math.log2(math.e) / math.sqrt(MLA_QK_DIM)


def _silu(g):
    return g * (1.0 / (1.0 + jnp.exp(-g)))


def _dot(a, b):
    return jnp.dot(a, b, preferred_element_type=F32)


def _dot_nt(a, b):
    return lax.dot_general(a, b, (((1,), (1,)), ((), ())), preferred_element_type=F32)


def _dot_tn(a, b):
    return lax.dot_general(a, b, (((0,), (0,)), ((), ())), preferred_element_type=F32)


def _adaln_kernel(c_ref, w_ref, b_ref, o_ref):
    o_ref[...] = jnp.dot(_silu(c_ref[...]), w_ref[...], preferred_element_type=F32,
                         precision=lax.Precision.HIGHEST) + b_ref[...]


def _adaln(c_pad, w_ada, b_ada):
    rows, d = c_pad.shape
    n = w_ada.shape[1]
    bn = 512
    return pl.pallas_call(
        _adaln_kernel,
        grid=(n // bn,),
        in_specs=[pl.BlockSpec((rows, d), lambda j: (0, 0)),
                  pl.BlockSpec((d, bn), lambda j: (0, j)),
                  pl.BlockSpec((1, bn), lambda j: (0, j))],
        out_specs=pl.BlockSpec((rows, bn), lambda j: (0, j)),
        out_shape=jax.ShapeDtypeStruct((rows, n), F32),
        name="adaln",
    )(c_pad, w_ada, b_ada)


def _rope_t(x1, x2, cos, sin):
    return x1 * cos - x2 * sin, x2 * cos + x1 * sin


def _inproj_kernel(x_ref, mod_ref, nw_ref, win_ref, wgm_ref, qln_ref, wuqt_ref, kvln_ref, wukt_ref, wuvt_ref,
                   gq_ref, gk_ref, cos_ref, sin_ref,
                   qsb_ref, ksb_ref, vsb_ref, gsb_ref, qmt_ref, km_ref, vmt_ref, gm_ref):
    half = MLA_ROPE_DIM // 2
    n0, n1, n2 = MLA_NOPE_DIM, MLA_NOPE_DIM + half, MLA_QK_DIM
    x = x_ref[0]
    ts = x.shape[0]
    y = x * lax.rsqrt(jnp.mean(x * x, axis=-1, keepdims=True) + EPS) * nw_ref[...]
    h = (y * (1.0 + mod_ref[0, 1:2, :]) + mod_ref[0, 0:1, :]).astype(BF16)

    def proj(lo, hi):
        return _dot(h, win_ref[:, lo:hi])

    def rms(t, w_ref):
        return t * lax.rsqrt(jnp.mean(t * t, axis=-1, keepdims=True) + EPS) * w_ref[...]

    cq = proj(_C_CQ, _C_CKV)
    ckv = proj(_C_CKV, _C_KR)
    kr = proj(_C_KR, _C_KR + LANES)
    qsb_ref[0] = (proj(_C_QSB, _C_KSB) * SB_Q_SCALE).astype(BF16)
    ksb_ref[0] = proj(_C_KSB, _C_VSB).astype(BF16)

    cqn = rms(cq, qln_ref).astype(BF16)
    ckvn = rms(ckv, kvln_ref).astype(BF16)
    q_t = _dot_nt(wuqt_ref[...], cqn)
    k_t = _dot_nt(wukt_ref[...], ckvn)
    v_t = _dot_nt(wuvt_ref[...], ckvn)
    vsb_ref[0] = proj(_C_VSB, _C_GSB).astype(BF16)
    gsb_ref[0] = proj(_C_GSB, _C_CQ)
    gm_ref[0] = _dot(h, wgm_ref[...])

    cos, sin = cos_ref[0], sin_ref[0]
    pad_rows = jnp.zeros((LANES - MLA_QK_DIM, ts), F32)

    gq = gq_ref[...]
    for hd in range(MLA_HEADS):
        blk = q_t[hd * LANES:(hd + 1) * LANES]
        inv = lax.rsqrt(jnp.sum(blk * blk, axis=0, keepdims=True) * (1.0 / MLA_QK_DIM) + EPS)
        g = blk * gq
        r1, r2 = _rope_t(g[n0:n1], g[n1:n2], cos, sin)
        out = jnp.concatenate([g[:n0], r1, r2, pad_rows], axis=0) * (inv * MLA_Q_SCALE)
        qmt_ref[0, hd * LANES:(hd + 1) * LANES, :] = out.astype(BF16)

    ones_rows = jnp.ones((MLA_V_ONES, ts), F32)
    v_rows = []
    for hd in range(MLA_HEADS):
        v_rows += [v_t[hd * MLA_V_DIM:(hd + 1) * MLA_V_DIM], ones_rows]
    vmt_ref[0] = jnp.concatenate(v_rows, axis=0).astype(BF16)

    gk = gk_ref[...]
    kr_t = kr.T
    x1, x2 = kr_t[:half], kr_t[half:MLA_ROPE_DIM]
    kr_ssq = jnp.sum(x1 * x1 + x2 * x2, axis=0, keepdims=True)
    r1, r2 = _rope_t(x1 * gk[n0:n1], x2 * gk[n1:n2], cos, sin)
    for hd in range(MLA_HEADS):
        kn = k_t[hd * LANES:hd * LANES + n0]
        ssq = jnp.sum(kn * kn, axis=0, keepdims=True) + kr_ssq
        inv = lax.rsqrt(ssq * (1.0 / MLA_QK_DIM) + EPS)
        out = jnp.concatenate([kn * gk[:n0], r1, r2, pad_rows], axis=0) * inv
        km_ref[0, :, hd * LANES:(hd + 1) * LANES] = out.astype(BF16).T


def _inproj(x, mod, norm_w, w_in_b, w_gm, qln, w_uq_t, kvln, w_uk_t, w_uv, gq, gk, cos_t, sin_t):
    B, S, D = x.shape
    ts = min(ROW_TILE, S)
    grid = (B, S // ts)

    def whole(a):
        return pl.BlockSpec(a.shape, lambda b, i: (0,) * a.ndim)

    def rows(width):
        return pl.BlockSpec((1, ts, width), lambda b, i: (b, i, 0))

    def cols(height):
        return pl.BlockSpec((1, height, ts), lambda b, i: (b, 0, i))

    out_shapes = (
        jax.ShapeDtypeStruct((B, S, SB_WIDTH), BF16),
        jax.ShapeDtypeStruct((B, S, SB_WIDTH), BF16),
        jax.ShapeDtypeStruct((B, S, SB_WIDTH), BF16),
        jax.ShapeDtypeStruct((B, S, SB_WIDTH), F32),
        jax.ShapeDtypeStruct((B, MLA_PAD_WIDTH, S), BF16),
        jax.ShapeDtypeStruct((B, S, MLA_PAD_WIDTH), BF16),
        jax.ShapeDtypeStruct((B, MLA_HEADS * MLA_V_ROWS, S), BF16),
        jax.ShapeDtypeStruct((B, S, MLA_WIDTH), F32),
    )
    out_specs = [rows(s.shape[-1]) for s in out_shapes]
    out_specs[4] = cols(MLA_PAD_WIDTH)
    out_specs[6] = cols(MLA_HEADS * MLA_V_ROWS)
    half = MLA_ROPE_DIM // 2
    return pl.pallas_call(
        _inproj_kernel,
        grid=grid,
        in_specs=[rows(D),
                  pl.BlockSpec((1, 3, D), lambda b, i: (b, 0, 0)),
                  whole(norm_w), whole(w_in_b), whole(w_gm), whole(qln), whole(w_uq_t), whole(kvln),
                  whole(w_uk_t), whole(w_uv), whole(gq), whole(gk),
                  cols(half), cols(half)],
        out_specs=out_specs,
        out_shape=out_shapes,
        compiler_params=pltpu.CompilerParams(
            dimension_semantics=("arbitrary", "arbitrary"), vmem_limit_bytes=VMEM_LIMIT),
        name="inproj",
    )(x, mod, norm_w, w_in_b, w_gm, qln, w_uq_t, kvln, w_uk_t, w_uv, gq, gk, cos_t, sin_t)


def _sb_kernel(q_ref, k_ref, v_ref, g_ref, o_ref, acc_ref, carry_ref):
    tq, tk = SB_TQ, SB_TK
    n_heads = SB_PAIRS * HEAD_PAIR
    i = pl.program_id(2)
    lane = lax.broadcasted_iota(jnp.int32, (tq, LANES), 1)
    q_heads = []
    for sub in range(SB_SUBS):
        heads = []
        for p in range(SB_PAIRS):
            q_pair = q_ref[0, sub * tq:(sub + 1) * tq, p * LANES:(p + 1) * LANES]
            for hh in range(HEAD_PAIR):
                heads.append(jnp.where((lane // SB_HEAD_DIM) == hh, q_pair, jnp.zeros_like(q_pair)))
        q_heads.append(heads)

    strict = (lax.broadcasted_iota(jnp.int32, (tk, tq), 0)
              < lax.broadcasted_iota(jnp.int32, (tk, tq), 1))
    l_row = lax.broadcasted_iota(jnp.int32, (tk + 8, tk), 0)
    l_col = lax.broadcasted_iota(jnp.int32, (tk + 8, tk), 1)
    later = jnp.where(jnp.logical_or(l_col > l_row, l_row >= tk), 1.0, 0.0).astype(F32)

    acc_ref[...] = jnp.zeros_like(acc_ref)
    carry_ref[...] = jnp.zeros_like(carry_ref)

    def visit(plan, diag_first):
        depth = max(len(tiles) for _, tiles in plan)
        items = [(sub, t, tiles[t], h) for t in range(depth) for sub, tiles in plan
                 if t < len(tiles) for h in range(n_heads)]

        def pair_block(ref, j, h):
            p = h // HEAD_PAIR
            return ref[0, pl.ds(pl.multiple_of(j * tk, tk), tk), p * LANES:(p + 1) * LANES]

        n = len(items)
        zs, log_betas, afters, col_sums = [None] * n, [None] * n, [None] * n, [None] * n
        carries = {(sub, h): carry_ref[sub, h] for sub, _ in plan for h in range(n_heads)}
        outs = {key: None for key in carries}

        def scores(idx):
            sub, _, j, h = items[idx]
            zs[idx] = _dot_nt(pair_block(k_ref, j, h), q_heads[sub][h])

        def drops(idx):
            _, t, _, _ = items[idx]
            z = zs[idx]
            drop = jnp.maximum(z, 0.0) + jnp.log2(1.0 + jnp.exp2(-jnp.abs(z)))
            if diag_first and t == 0:
                drop = jnp.where(strict, drop, 0.0)
            hi = drop.astype(BF16).astype(F32)
            sums = _dot(later, hi) + _dot(later, drop - hi)
            afters[idx] = sums[:tk]
            col_sums[idx] = sums[tk:tk + 1]
            log_betas[idx] = z - drop

        def weights(idx):
            sub, t, j, h = items[idx]
            w = jnp.exp2(log_betas[idx] - (afters[idx] + carries[sub, h]))
            if diag_first and t == 0:
                w = jnp.where(strict, w, 0.0)
            pv = _dot_tn(pair_block(v_ref, j, h), w.astype(BF16))
            lo_row = (h % HEAD_PAIR) * SB_HEAD_DIM
            pv = pv[lo_row:lo_row + SB_HEAD_DIM, :]
            outs[sub, h] = pv if outs[sub, h] is None else outs[sub, h] + pv
            carries[sub, h] = carries[sub, h] + col_sums[idx]

        for step in range(n + SB_STAGE_LAG[1]):
            if step < n:
                scores(step)
            if 0 <= step - SB_STAGE_LAG[0] < n:
                drops(step - SB_STAGE_LAG[0])
            if 0 <= step - SB_STAGE_LAG[1] < n:
                weights(step - SB_STAGE_LAG[1])
        alive = []
        for sub, _ in plan:
            dead = None
            for h in range(n_heads):
                acc_ref[sub, h] += outs[sub, h]
                carry_ref[sub, h] = carries[sub, h]
                dead = carries[sub, h] if dead is None else jnp.minimum(dead, carries[sub, h])
            alive.append(jnp.min(dead) < SB_DEAD_LOG2_DROP)
        return alive

    first = SB_SUBS * i

    def usual_step():
        return visit([(sub, [first + sub - t for t in range(SB_FIRST_VISIT)])
                      for sub in range(SB_SUBS)], True)

    def first_step():
        return visit([(sub, [sub - t for t in range(SB_FIRST_VISIT) if sub - t >= 0])
                      for sub in range(SB_SUBS)], True)

    assert SB_SUBS >= SB_FIRST_VISIT - 1
    alive = lax.cond(i > 0, usual_step, first_step)

    for sub in range(SB_SUBS):
        def cond(state):
            j, still_alive = state
            return jnp.logical_and(j >= 0, still_alive)

        def body(state, sub=sub):
            j, _ = state
            return j - 1, visit([(sub, [j])], False)[0]

        lax.while_loop(cond, body, (first + sub - SB_FIRST_VISIT, alive[sub]))

    for sub in range(SB_SUBS):
        rows = slice(sub * tq, (sub + 1) * tq)
        o_t = acc_ref[sub].reshape(n_heads * SB_HEAD_DIM, tq)
        o_ref[0, rows, :] = (o_t.T * _silu(g_ref[0, rows, :])).astype(o_ref.dtype)


def _sb_attention(q, k, v, g):
    B, S, W = q.shape
    rows = SB_SUBS * SB_TQ
    assert S % rows == 0 and SB_TQ == SB_TK
    wb = SB_PAIRS * LANES
    n_heads = SB_PAIRS * HEAD_PAIR
    grid = (B, W // wb, S // rows)
    q_spec = pl.BlockSpec((1, rows, wb), lambda b, p, i: (b, i, p))
    kv_spec = pl.BlockSpec((1, S, wb), lambda b, p, i: (b, 0, p))
    return pl.pallas_call(
        _sb_kernel,
        grid=grid,
        in_specs=[q_spec, kv_spec, kv_spec, q_spec],
        out_specs=q_spec,
        out_shape=jax.ShapeDtypeStruct((B, S, W), BF16),
        scratch_shapes=[pltpu.VMEM((SB_SUBS, n_heads, SB_HEAD_DIM, SB_TQ), F32),
                        pltpu.VMEM((SB_SUBS, n_heads, 1, SB_TQ), F32)],
        compiler_params=pltpu.CompilerParams(
            dimension_semantics=("arbitrary", "arbitrary", "arbitrary"),
            vmem_limit_bytes=VMEM_LIMIT),
        name="sb_attn",
    )(q, k, v, g)


def _mla_kernel(q_ref, qnext_ref, k_ref, v_ref, g_ref, o_ref, acc_ref, m_ref, s_ref):
    tq, tk = MLA_TQ, MLA_TK
    diag_tiles = tq // tk
    i = pl.program_id(2)
    neg = jnp.finfo(F32).min

    acc_ref[...] = jnp.zeros_like(acc_ref)
    m_ref[...] = jnp.full_like(m_ref, neg)

    def score(j, slot, diag=None, queries=q_ref):
        start = pl.multiple_of(j * tk, tk)
        lo = 0 if diag is None else diag * tk
        for hh in range(MLA_GROUP):
            q_t = queries[0, hh * LANES:(hh + 1) * LANES, lo:]
            k_blk = k_ref[0, pl.ds(start, tk), hh * LANES:(hh + 1) * LANES]
            s_ref[slot, hh, :, lo:] = _dot(k_blk, q_t)

    def consume(j, slot, diag=None):
        start = pl.multiple_of(j * tk, tk)
        lo = 0 if diag is None else diag * tk
        for hh in range(MLA_GROUP):
            v_t = v_ref[0, hh * MLA_V_ROWS:(hh + 1) * MLA_V_ROWS, pl.ds(start, tk)]
            s = s_ref[slot, hh, :, lo:]
            if diag is not None:
                causal = (lax.broadcasted_iota(jnp.int32, s.shape, 0)
                          <= lax.broadcasted_iota(jnp.int32, s.shape, 1))
                s = jnp.where(causal, s, neg)
            m_old = m_ref[hh, :, lo:]
            m_new = jnp.maximum(m_old, jnp.max(s, axis=0, keepdims=True))
            alpha = jnp.exp2(m_old[:1] - m_new[:1])
            p = jnp.exp2(s - m_new[:1])
            acc_ref[hh, :, lo:] = alpha * acc_ref[hh, :, lo:] + _dot(v_t, p.astype(BF16))
            m_ref[hh, :, lo:] = m_new

    base = i * diag_tiles

    def two_tiles(t):
        score(t + 1, 1)
        consume(t, 0)
        score(t + 2, 0)
        consume(t + 1, 1)

    looped = jnp.maximum(base - 2, 0)
    done = 0
    pairs = MLA_UNROLL
    while pairs >= 1:
        def group(jj, carry, pairs=pairs, done=done):
            for r in range(pairs):
                two_tiles(done + 2 * pairs * jj + 2 * r)
            return carry

        n_groups = (looped - done) // (2 * pairs)
        lax.fori_loop(0, n_groups, group, 0)
        done = done + n_groups * 2 * pairs
        pairs //= 2

    def diagonal_block():
        for d in range(diag_tiles):
            if d + 1 < diag_tiles:
                score(base + d + 1, (d + 1) % 2, d + 1)
            consume(base + d, d % 2, d)
            if d == 0:
                score(0, 0, queries=qnext_ref)
        o_t = jnp.concatenate(
            [acc_ref[hh, :MLA_V_DIM] / acc_ref[hh, MLA_V_DIM:MLA_V_DIM + 1] for hh in range(MLA_GROUP)],
            axis=0)
        o_ref[0] = (o_t.T * _silu(g_ref[0])).astype(o_ref.dtype)

    @pl.when(i > 0)
    def _():
        two_tiles(base - 2)
        diagonal_block()

    @pl.when(i == 0)
    def _():
        score(0, 0)
        diagonal_block()


def _mla_attention(q, k, v, g):
    B, S, W = g.shape
    tq = MLA_TQ
    group_rows = MLA_GROUP * MLA_V_ROWS
    group_out = MLA_GROUP * MLA_V_DIM
    assert S % tq == 0 and MLA_TQ % (2 * MLA_TK) == 0 and group_out % LANES == 0
    grid = (B, W // group_out, S // tq)
    return pl.pallas_call(
        _mla_kernel,
        grid=grid,
        in_specs=[pl.BlockSpec((1, MLA_GROUP * LANES, tq), lambda b, p, i: (b, p, i)),
                  pl.BlockSpec((1, MLA_GROUP * LANES, tq),
                               lambda b, p, i: (b, p, jnp.minimum(i + 1, S // tq - 1))),
                  pl.BlockSpec((1, S, MLA_GROUP * LANES), lambda b, p, i: (b, 0, p)),
                  pl.BlockSpec((1, group_rows, S), lambda b, p, i: (b, p, 0)),
                  pl.BlockSpec((1, tq, group_out), lambda b, p, i: (b, i, p))],
        out_specs=pl.BlockSpec((1, tq, group_out), lambda b, p, i: (b, i, p)),
        out_shape=jax.ShapeDtypeStruct((B, S, W), BF16),
        scratch_shapes=[pltpu.VMEM((MLA_GROUP, MLA_V_ROWS, tq), F32),
                        pltpu.VMEM((MLA_GROUP, 8, tq), F32),
                        pltpu.VMEM((2, MLA_GROUP, MLA_TK, tq), F32)],
        compiler_params=pltpu.CompilerParams(
            dimension_semantics=("arbitrary", "arbitrary", "arbitrary"),
            vmem_limit_bytes=VMEM_LIMIT),
        name="mla_attn",
    )(q, q, k, v, g)


def _outproj_kernel(msb_ref, mmla_ref, x_ref, mod_ref, w_ref, o_ref):
    y = _dot(msb_ref[0], w_ref[:SB_WIDTH, :]) + _dot(mmla_ref[0], w_ref[SB_WIDTH:, :])
    o_ref[0] = x_ref[0] + mod_ref[0, 2:3, :] * y


def _outproj(mixed_sb, mixed_mla, x, mod, w_out):
    B, S, D = x.shape
    ts = min(OUT_ROW_TILE, S)

    def rows(width):
        return pl.BlockSpec((1, ts, width), lambda b, i: (b, i, 0))

    return pl.pallas_call(
        _outproj_kernel,
        grid=(B, S // ts),
        in_specs=[rows(SB_WIDTH), rows(MLA_WIDTH), rows(D),
                  pl.BlockSpec((1, 3, D), lambda b, i: (b, 0, 0)),
                  pl.BlockSpec(w_out.shape, lambda b, i: (0, 0))],
        out_specs=rows(D),
        out_shape=jax.ShapeDtypeStruct((B, S, D), x.dtype),
        compiler_params=pltpu.CompilerParams(
            dimension_semantics=("arbitrary", "arbitrary"), vmem_limit_bytes=VMEM_LIMIT),
        name="outproj",
    )(mixed_sb, mixed_mla, x, mod, w_out)


def _pad_last(a, width):
    return jnp.pad(a, [(0, 0)] * (a.ndim - 1) + [(0, width - a.shape[-1])])


def _layer(x, c, tables, w_ada, b_ada, norm_w, w_in, q_lora_norm, w_uq, kv_lora_norm, w_ukv,
           q_head_norm, k_head_norm, w_out):
    B, S, D = x.shape
    c_pad = jnp.pad(c, ((0, 8 - B), (0, 0)))
    ada = _adaln(c_pad, w_ada, b_ada[None, :])[:B]
    mod = ada.reshape(B, 3, D)

    ts = min(ROW_TILE, S)
    w_in_b = w_in.astype(BF16)
    w_gm = w_in[:, _C_GMLA:].astype(BF16)
    w_uq_t = _pad_last(w_uq.reshape(Q_LORA_RANK, MLA_HEADS, MLA_QK_DIM), LANES)
    w_uq_t = w_uq_t.reshape(Q_LORA_RANK, MLA_PAD_WIDTH).T.astype(BF16)
    w_ukv_h = w_ukv.reshape(KV_LORA_RANK, MLA_HEADS, MLA_NOPE_DIM + MLA_V_DIM)
    w_uk_t = _pad_last(w_ukv_h[:, :, :MLA_NOPE_DIM], LANES)
    w_uk_t = w_uk_t.reshape(KV_LORA_RANK, MLA_PAD_WIDTH).T.astype(BF16)
    w_uv = w_ukv_h[:, :, MLA_NOPE_DIM:].reshape(KV_LORA_RANK, MLA_WIDTH).T.astype(BF16)
    gq = jnp.broadcast_to(_pad_last(q_head_norm, LANES)[:, None], (LANES, ts))
    gk = jnp.broadcast_to(_pad_last(k_head_norm, LANES)[:, None], (LANES, ts))

    (q_sb, k_sb, v_sb, g_sb, q_m, k_m, v_m, g_m) = _inproj(
        x, mod, norm_w[None, :], w_in_b, w_gm, q_lora_norm[None, :], w_uq_t, kv_lora_norm[None, :],
        w_uk_t, w_uv, gq, gk, *tables)

    mixed_sb = _sb_attention(q_sb, k_sb, v_sb, g_sb)
    mixed_mla = _mla_attention(q_m, k_m, v_m, g_m)
    return _outproj(mixed_sb, mixed_mla, x, mod, w_out.astype(BF16))


def _rope_tables(positions, dtype):
    inv_freq = ROPE_THETA ** (-jnp.arange(0, MLA_ROPE_DIM, 2, dtype=F32) / MLA_ROPE_DIM)
    ang = inv_freq[None, :, None] * positions.astype(F32)[:, None, :]
    return jnp.cos(ang).astype(dtype), jnp.sin(ang).astype(dtype)


def kernel(x, c, positions, w_ada, b_ada, norm_w, w_in, q_lora_norm, w_uq, kv_lora_norm, w_ukv,
           q_head_norm, k_head_norm, w_out):
    tables = _rope_tables(positions, x.dtype)
    for l in range(w_ada.shape[0]):
        x = _layer(x, c, tables, w_ada[l], b_ada[l], norm_w[l], w_in[l], q_lora_norm[l], w_uq[l],
                   kv_lora_norm[l], w_ukv[l], q_head_norm[l], k_head_norm[l], w_out[l])
    return x
```

```python
import functools
import math

import jax
import jax.numpy as jnp
from jax import lax
from jax.experimental import pallas as pl
from jax.experimental.pallas import tpu as pltpu

F32 = jnp.float32
BF16 = jnp.bfloat16

D_MODEL = 1024
SB_HEADS = 8
SB_HEAD_DIM = 64
SB_WIDTH = SB_HEADS * SB_HEAD_DIM
MLA_HEADS = 8
MLA_NOPE_DIM = 64
MLA_ROPE_DIM = 32
MLA_QK_DIM = MLA_NOPE_DIM + MLA_ROPE_DIM
MLA_V_DIM = 64
MLA_WIDTH = MLA_HEADS * MLA_V_DIM
MLA_V_ONES = 16
MLA_V_ROWS = MLA_V_DIM + MLA_V_ONES
Q_LORA_RANK = 384
KV_LORA_RANK = 256
ROPE_THETA = 10000.0
EPS = 1e-6

LANES = 128
HEAD_PAIR = 2
MLA_PAD_WIDTH = MLA_HEADS * LANES

_C_QSB = 0
_C_KSB = _C_QSB + SB_WIDTH
_C_VSB = _C_KSB + SB_WIDTH
_C_GSB = _C_VSB + SB_WIDTH
_C_CQ = _C_GSB + SB_WIDTH
_C_CKV = _C_CQ + Q_LORA_RANK
_C_KR = _C_CKV + KV_LORA_RANK
_C_GMLA = _C_KR + MLA_ROPE_DIM

ROW_TILE = 512
OUT_ROW_TILE = 2048
SB_TQ = 256
SB_TK = 256
SB_PAIRS = 4
SB_SUBS = 2
SB_FIRST_VISIT = 2
SB_STAGE_LAG = (2, 4)
MLA_GROUP = 2
MLA_UNROLL = 4
MLA_TQ = 512
MLA_TK = 256
VMEM_LIMIT = 56 * 1024 * 1024

SB_DEAD_LOG2_DROP = 160.0
SB_Q_SCALE = math.log2(math.e) / math.sqrt(SB_HEAD_DIM)
MLA_Q_SCALE = math.log2(math.e) / math.sqrt(MLA_QK_DIM)


def _silu(g):
    return g * (1.0 / (1.0 + jnp.exp(-g)))


def _dot(a, b):
    return jnp.dot(a, b, preferred_element_type=F32)


def _dot_nt(a, b):
    return lax.dot_general(a, b, (((1,), (1,)), ((), ())), preferred_element_type=F32)


def _dot_tn(a, b):
    return lax.dot_general(a, b, (((0,), (0,)), ((), ())), preferred_element_type=F32)


def _adaln_kernel(c_ref, w_ref, b_ref, o_ref):
    o_ref[...] = jnp.dot(_silu(c_ref[...]), w_ref[...], preferred_element_type=F32,
                         precision=lax.Precision.HIGHEST) + b_ref[...]


def _adaln(c_pad, w_ada, b_ada):
    rows, d = c_pad.shape
    n = w_ada.shape[1]
    bn = 512
    return pl.pallas_call(
        _adaln_kernel,
        grid=(n // bn,),
        in_specs=[pl.BlockSpec((rows, d), lambda j: (0, 0)),
                  pl.BlockSpec((d, bn), lambda j: (0, j)),
                  pl.BlockSpec((1, bn), lambda j: (0, j))],
        out_specs=pl.BlockSpec((rows, bn), lambda j: (0, j)),
        out_shape=jax.ShapeDtypeStruct((rows, n), F32),
        name="adaln",
    )(c_pad, w_ada, b_ada)


def _rope_t(x1, x2, cos, sin):
    return x1 * cos - x2 * sin, x2 * cos + x1 * sin


def _inproj_kernel(x_ref, mod_ref, nw_ref, win_ref, wgm_ref, qln_ref, wuqt_ref, kvln_ref, wukt_ref, wuvt_ref,
                   gq_ref, gk_ref, cos_ref, sin_ref,
                   qsb_ref, ksb_ref, vsb_ref, gsb_ref, qmt_ref, km_ref, vmt_ref, gm_ref):
    half = MLA_ROPE_DIM // 2
    n0, n1, n2 = MLA_NOPE_DIM, MLA_NOPE_DIM + half, MLA_QK_DIM
    x = x_ref[0]
    ts = x.shape[0]
    y = x * lax.rsqrt(jnp.mean(x * x, axis=-1, keepdims=True) + EPS) * nw_ref[...]
    h = (y * (1.0 + mod_ref[0, 1:2, :]) + mod_ref[0, 0:1, :]).astype(BF16)

    def proj(lo, hi):
        return _dot(h, win_ref[:, lo:hi])

    def rms(t, w_ref):
        return t * lax.rsqrt(jnp.mean(t * t, axis=-1, keepdims=True) + EPS) * w_ref[...]

    cq = proj(_C_CQ, _C_CKV)
    ckv = proj(_C_CKV, _C_KR)
    kr = proj(_C_KR, _C_KR + LANES)
    qsb_ref[0] = (proj(_C_QSB, _C_KSB) * SB_Q_SCALE).astype(BF16)
    ksb_ref[0] = proj(_C_KSB, _C_VSB).astype(BF16)

    cqn = rms(cq, qln_ref).astype(BF16)
    ckvn = rms(ckv, kvln_ref).astype(BF16)
    q_t = _dot_nt(wuqt_ref[...], cqn)
    k_t = _dot_nt(wukt_ref[...], ckvn)
    v_t = _dot_nt(wuvt_ref[...], ckvn)
    vsb_ref[0] = proj(_C_VSB, _C_GSB).astype(BF16)
    gsb_ref[0] = proj(_C_GSB, _C_CQ)
    gm_ref[0] = _dot(h, wgm_ref[...])

    cos, sin = cos_ref[0], sin_ref[0]
    pad_rows = jnp.zeros((LANES - MLA_QK_DIM, ts), F32)

    gq = gq_ref[...]
    for hd in range(MLA_HEADS):
        blk = q_t[hd * LANES:(hd + 1) * LANES]
        inv = lax.rsqrt(jnp.sum(blk * blk, axis=0, keepdims=True) * (1.0 / MLA_QK_DIM) + EPS)
        g = blk * gq
        r1, r2 = _rope_t(g[n0:n1], g[n1:n2], cos, sin)
        out = jnp.concatenate([g[:n0], r1, r2, pad_rows], axis=0) * (inv * MLA_Q_SCALE)
        qmt_ref[0, hd * LANES:(hd + 1) * LANES, :] = out.astype(BF16)

    ones_rows = jnp.ones((MLA_V_ONES, ts), F32)
    v_rows = []
    for hd in range(MLA_HEADS):
        v_rows += [v_t[hd * MLA_V_DIM:(hd + 1) * MLA_V_DIM], ones_rows]
    vmt_ref[0] = jnp.concatenate(v_rows, axis=0).astype(BF16)

    gk = gk_ref[...]
    kr_t = kr.T
    x1, x2 = kr_t[:half], kr_t[half:MLA_ROPE_DIM]
    kr_ssq = jnp.sum(x1 * x1 + x2 * x2, axis=0, keepdims=True)
    r1, r2 = _rope_t(x1 * gk[n0:n1], x2 * gk[n1:n2], cos, sin)
    for hd in range(MLA_HEADS):
        kn = k_t[hd * LANES:hd * LANES + n0]
        ssq = jnp.sum(kn * kn, axis=0, keepdims=True) + kr_ssq
        inv = lax.rsqrt(ssq * (1.0 / MLA_QK_DIM) + EPS)
        out = jnp.concatenate([kn * gk[:n0], r1, r2, pad_rows], axis=0) * inv
        km_ref[0, :, hd * LANES:(hd + 1) * LANES] = out.astype(BF16).T


def _inproj(x, mod, norm_w, w_in_b, w_gm, qln, w_uq_t, kvln, w_uk_t, w_uv, gq, gk, cos_t, sin_t):
    B, S, D = x.shape
    ts = min(ROW_TILE, S)
    grid = (B, S // ts)

    def whole(a):
        return pl.BlockSpec(a.shape, lambda b, i: (0,) * a.ndim)

    def rows(width):
        return pl.BlockSpec((1, ts, width), lambda b, i: (b, i, 0))

    def cols(height):
        return pl.BlockSpec((1, height, ts), lambda b, i: (b, 0, i))

    out_shapes = (
        jax.ShapeDtypeStruct((B, S, SB_WIDTH), BF16),
        jax.ShapeDtypeStruct((B, S, SB_WIDTH), BF16),
        jax.ShapeDtypeStruct((B, S, SB_WIDTH), BF16),
        jax.ShapeDtypeStruct((B, S, SB_WIDTH), F32),
        jax.ShapeDtypeStruct((B, MLA_PAD_WIDTH, S), BF16),
        jax.ShapeDtypeStruct((B, S, MLA_PAD_WIDTH), BF16),
        jax.ShapeDtypeStruct((B, MLA_HEADS * MLA_V_ROWS, S), BF16),
        jax.ShapeDtypeStruct((B, S, MLA_WIDTH), F32),
    )
    out_specs = [rows(s.shape[-1]) for s in out_shapes]
    out_specs[4] = cols(MLA_PAD_WIDTH)
    out_specs[6] = cols(MLA_HEADS * MLA_V_ROWS)
    half = MLA_ROPE_DIM // 2
    return pl.pallas_call(
        _inproj_kernel,
        grid=grid,
        in_specs=[rows(D),
                  pl.BlockSpec((1, 3, D), lambda b, i: (b, 0, 0)),
                  whole(norm_w), whole(w_in_b), whole(w_gm), whole(qln), whole(w_uq_t), whole(kvln),
                  whole(w_uk_t), whole(w_uv), whole(gq), whole(gk),
                  cols(half), cols(half)],
        out_specs=out_specs,
        out_shape=out_shapes,
        compiler_params=pltpu.CompilerParams(
            dimension_semantics=("arbitrary", "arbitrary"), vmem_limit_bytes=VMEM_LIMIT),
        name="inproj",
    )(x, mod, norm_w, w_in_b, w_gm, qln, w_uq_t, kvln, w_uk_t, w_uv, gq, gk, cos_t, sin_t)


def _sb_kernel(q_ref, k_ref, v_ref, g_ref, o_ref, acc_ref, carry_ref):
    tq, tk = SB_TQ, SB_TK
    n_heads = SB_PAIRS * HEAD_PAIR
    i = pl.program_id(2)
    lane = lax.broadcasted_iota(jnp.int32, (tq, LANES), 1)
    q_heads = []
    for sub in range(SB_SUBS):
        heads = []
        for p in range(SB_PAIRS):
            q_pair = q_ref[0, sub * tq:(sub + 1) * tq, p * LANES:(p + 1) * LANES]
            for hh in range(HEAD_PAIR):
                heads.append(jnp.where((lane // SB_HEAD_DIM) == hh, q_pair, jnp.zeros_like(q_pair)))
        q_heads.append(heads)

    strict = (lax.broadcasted_iota(jnp.int32, (tk, tq), 0)
              < lax.broadcasted_iota(jnp.int32, (tk, tq), 1))
    l_row = lax.broadcasted_iota(jnp.int32, (tk + 8, tk), 0)
    l_col = lax.broadcasted_iota(jnp.int32, (tk + 8, tk), 1)
    later = jnp.where(jnp.logical_or(l_col >= l_row, l_row >= tk), 1.0, 0.0).astype(F32)

    acc_ref[...] = jnp.zeros_like(acc_ref)
    carry_ref[...] = jnp.zeros_like(carry_ref)

    def visit(plan, diag_first):
        depth = max(len(tiles) for _, tiles in plan)
        items = [(sub, t, tiles[t], h) for t in range(depth) for sub, tiles in plan
                 if t < len(tiles) for h in range(n_heads)]

        def pair_block(ref, j, h):
            p = h // HEAD_PAIR
            return ref[0, pl.ds(pl.multiple_of(j * tk, tk), tk), p * LANES:(p + 1) * LANES]

        n = len(items)
        zs, afters, col_sums = [None] * n, [None] * n, [None] * n
        carries = {(sub, h): carry_ref[sub, h] for sub, _ in plan for h in range(n_heads)}
        outs = {key: None for key in carries}

        def scores(idx):
            sub, _, j, h = items[idx]
            zs[idx] = _dot_nt(pair_block(k_ref, j, h), q_heads[sub][h])

        def drops(idx):
            _, t, _, _ = items[idx]
            z = zs[idx]
            drop = jnp.maximum(z, 0.0) + jnp.log2(1.0 + jnp.exp2(-jnp.abs(z)))
            if diag_first and t == 0:
                drop = jnp.where(strict, drop, 0.0)
            hi = drop.astype(BF16).astype(F32)
            sums = _dot(later, hi) + _dot(later, drop - hi)
            afters[idx] = sums[:tk]
            col_sums[idx] = sums[tk:tk + 1]

        def weights(idx):
            sub, t, j, h = items[idx]
            w = jnp.exp2(zs[idx] - (afters[idx] + carries[sub, h]))
            if diag_first and t == 0:
                w = jnp.where(strict, w, 0.0)
            pv = _dot_tn(pair_block(v_ref, j, h), w.astype(BF16))
            lo_row = (h % HEAD_PAIR) * SB_HEAD_DIM
            pv = pv[lo_row:lo_row + SB_HEAD_DIM, :]
            outs[sub, h] = pv if outs[sub, h] is None else outs[sub, h] + pv
            carries[sub, h] = carries[sub, h] + col_sums[idx]

        for step in range(n + SB_STAGE_LAG[1]):
            if step < n:
                scores(step)
            if 0 <= step - SB_STAGE_LAG[0] < n:
                drops(step - SB_STAGE_LAG[0])
            if 0 <= step - SB_STAGE_LAG[1] < n:
                weights(step - SB_STAGE_LAG[1])
        alive = []
        for sub, _ in plan:
            dead = None
            for h in range(n_heads):
                acc_ref[sub, h] += outs[sub, h]
                carry_ref[sub, h] = carries[sub, h]
                dead = carries[sub, h] if dead is None else jnp.minimum(dead, carries[sub, h])
            alive.append(jnp.min(dead) < SB_DEAD_LOG2_DROP)
        return alive

    first = SB_SUBS * i

    def usual_step():
        return visit([(sub, [first + sub - t for t in range(SB_FIRST_VISIT)])
                      for sub in range(SB_SUBS)], True)

    def first_step():
        return visit([(sub, [sub - t for t in range(SB_FIRST_VISIT) if sub - t >= 0])
                      for sub in range(SB_SUBS)], True)

    assert SB_SUBS >= SB_FIRST_VISIT - 1
    alive = lax.cond(i > 0, usual_step, first_step)

    for sub in range(SB_SUBS):
        def cond(state):
            j, still_alive = state
            return jnp.logical_and(j >= 0, still_alive)

        def body(state, sub=sub):
            j, _ = state
            return j - 1, visit([(sub, [j])], False)[0]

        lax.while_loop(cond, body, (first + sub - SB_FIRST_VISIT, alive[sub]))

    for sub in range(SB_SUBS):
        rows = slice(sub * tq, (sub + 1) * tq)
        o_t = acc_ref[sub].reshape(n_heads * SB_HEAD_DIM, tq)
        o_ref[0, rows, :] = (o_t.T * _silu(g_ref[0, rows, :])).astype(o_ref.dtype)


def _sb_attention(q, k, v, g):
    B, S, W = q.shape
    rows = SB_SUBS * SB_TQ
    assert S % rows == 0 and SB_TQ == SB_TK
    wb = SB_PAIRS * LANES
    n_heads = SB_PAIRS * HEAD_PAIR
    grid = (B, W // wb, S // rows)
    q_spec = pl.BlockSpec((1, rows, wb), lambda b, p, i: (b, i, p))
    kv_spec = pl.BlockSpec((1, S, wb), lambda b, p, i: (b, 0, p))
    return pl.pallas_call(
        _sb_kernel,
        grid=grid,
        in_specs=[q_spec, kv_spec, kv_spec, q_spec],
        out_specs=q_spec,
        out_shape=jax.ShapeDtypeStruct((B, S, W), BF16),
        scratch_shapes=[pltpu.VMEM((SB_SUBS, n_heads, SB_HEAD_DIM, SB_TQ), F32),
                        pltpu.VMEM((SB_SUBS, n_heads, 1, SB_TQ), F32)],
        compiler_params=pltpu.CompilerParams(
            dimension_semantics=("arbitrary", "arbitrary", "arbitrary"),
            vmem_limit_bytes=VMEM_LIMIT),
        name="sb_attn",
    )(q, k, v, g)


def _mla_kernel(q_ref, qnext_ref, k_ref, v_ref, g_ref, o_ref, acc_ref, m_ref, s_ref):
    tq, tk = MLA_TQ, MLA_TK
    diag_tiles = tq // tk
    i = pl.program_id(2)
    neg = jnp.finfo(F32).min

    acc_ref[...] = jnp.zeros_like(acc_ref)
    m_ref[...] = jnp.full_like(m_ref, neg)

    def score(j, slot, diag=None, queries=q_ref):
        start = pl.multiple_of(j * tk, tk)
        lo = 0 if diag is None else diag * tk
        for hh in range(MLA_GROUP):
            q_t = queries[0, hh * LANES:(hh + 1) * LANES, lo:]
            k_blk = k_ref[0, pl.ds(start, tk), hh * LANES:(hh + 1) * LANES]
            s_ref[slot, hh, :, lo:] = _dot(k_blk, q_t)

    def consume(j, slot, diag=None):
        start = pl.multiple_of(j * tk, tk)
        lo = 0 if diag is None else diag * tk
        for hh in range(MLA_GROUP):
            v_t = v_ref[0, hh * MLA_V_ROWS:(hh + 1) * MLA_V_ROWS, pl.ds(start, tk)]
            s = s_ref[slot, hh, :, lo:]
            if diag is not None:
                causal = (lax.broadcasted_iota(jnp.int32, s.shape, 0)
                          <= lax.broadcasted_iota(jnp.int32, s.shape, 1))
                s = jnp.where(causal, s, neg)
            m_old = m_ref[hh, :, lo:]
            m_new = jnp.maximum(m_old, jnp.max(s, axis=0, keepdims=True))
            alpha = jnp.exp2(m_old[:1] - m_new[:1])
            p = jnp.exp2(s - m_new[:1])
            acc_ref[hh, :, lo:] = alpha * acc_ref[hh, :, lo:] + _dot(v_t, p.astype(BF16))
            m_ref[hh, :, lo:] = m_new

    base = i * diag_tiles

    def two_tiles(t):
        score(t + 1, 1)
        consume(t, 0)
        score(t + 2, 0)
        consume(t + 1, 1)

    looped = jnp.maximum(base - 2, 0)
    done = 0
    pairs = MLA_UNROLL
    while pairs >= 1:
        def group(jj, carry, pairs=pairs, done=done):
            for r in range(pairs):
                two_tiles(done + 2 * pairs * jj + 2 * r)
            return carry

        n_groups = (looped - done) // (2 * pairs)
        lax.fori_loop(0, n_groups, group, 0)
        done = done + n_groups * 2 * pairs
        pairs //= 2

    def diagonal_block():
        for d in range(diag_tiles):
            if d + 1 < diag_tiles:
                score(base + d + 1, (d + 1) % 2, d + 1)
            consume(base + d, d % 2, d)
            if d == 0:
                score(0, 0, queries=qnext_ref)
        o_t = jnp.concatenate(
            [acc_ref[hh, :MLA_V_DIM] / acc_ref[hh, MLA_V_DIM:MLA_V_DIM + 1] for hh in range(MLA_GROUP)],
            axis=0)
        o_ref[0] = (o_t.T * _silu(g_ref[0])).astype(o_ref.dtype)

    @pl.when(i > 0)
    def _():
        two_tiles(base - 2)
        diagonal_block()

    @pl.when(i == 0)
    def _():
        score(0, 0)
        diagonal_block()


def _mla_attention(q, k, v, g):
    B, S, W = g.shape
    tq = MLA_TQ
    group_rows = MLA_GROUP * MLA_V_ROWS
    group_out = MLA_GROUP * MLA_V_DIM
    assert S % tq == 0 and MLA_TQ % (2 * MLA_TK) == 0 and group_out % LANES == 0
    grid = (B, W // group_out, S // tq)
    return pl.pallas_call(
        _mla_kernel,
        grid=grid,
        in_specs=[pl.BlockSpec((1, MLA_GROUP * LANES, tq), lambda b, p, i: (b, p, i)),
                  pl.BlockSpec((1, MLA_GROUP * LANES, tq),
                               lambda b, p, i: (b, p, jnp.minimum(i + 1, S // tq - 1))),
                  pl.BlockSpec((1, S, MLA_GROUP * LANES), lambda b, p, i: (b, 0, p)),
                  pl.BlockSpec((1, group_rows, S), lambda b, p, i: (b, p, 0)),
                  pl.BlockSpec((1, tq, group_out), lambda b, p, i: (b, i, p))],
        out_specs=pl.BlockSpec((1, tq, group_out), lambda b, p, i: (b, i, p)),
        out_shape=jax.ShapeDtypeStruct((B, S, W), BF16),
        scratch_shapes=[pltpu.VMEM((MLA_GROUP, MLA_V_ROWS, tq), F32),
                        pltpu.VMEM((MLA_GROUP, 8, tq), F32),
                        pltpu.VMEM((2, MLA_GROUP, MLA_TK, tq), F32)],
        compiler_params=pltpu.CompilerParams(
            dimension_semantics=("arbitrary", "arbitrary", "arbitrary"),
            vmem_limit_bytes=VMEM_LIMIT),
        name="mla_attn",
    )(q, q, k, v, g)


def _outproj_kernel(msb_ref, mmla_ref, x_ref, mod_ref, w_ref, o_ref):
    y = _dot(msb_ref[0], w_ref[:SB_WIDTH, :]) + _dot(mmla_ref[0], w_ref[SB_WIDTH:, :])
    o_ref[0] = x_ref[0] + mod_ref[0, 2:3, :] * y


def _outproj(mixed_sb, mixed_mla, x, mod, w_out):
    B, S, D = x.shape
    ts = min(OUT_ROW_TILE, S)

    def rows(width):
        return pl.BlockSpec((1, ts, width), lambda b, i: (b, i, 0))

    return pl.pallas_call(
        _outproj_kernel,
        grid=(B, S // ts),
        in_specs=[rows(SB_WIDTH), rows(MLA_WIDTH), rows(D),
                  pl.BlockSpec((1, 3, D), lambda b, i: (b, 0, 0)),
                  pl.BlockSpec(w_out.shape, lambda b, i: (0, 0))],
        out_specs=rows(D),
        out_shape=jax.ShapeDtypeStruct((B, S, D), x.dtype),
        compiler_params=pltpu.CompilerParams(
            dimension_semantics=("arbitrary", "arbitrary"), vmem_limit_bytes=VMEM_LIMIT),
        name="outproj",
    )(mixed_sb, mixed_mla, x, mod, w_out)


def _pad_last(a, width):
    return jnp.pad(a, [(0, 0)] * (a.ndim - 1) + [(0, width - a.shape[-1])])


def _layer(x, c, tables, w_ada, b_ada, norm_w, w_in, q_lora_norm, w_uq, kv_lora_norm, w_ukv,
           q_head_norm, k_head_norm, w_out):
    B, S, D = x.shape
    c_pad = jnp.pad(c, ((0, 8 - B), (0, 0)))
    ada = _adaln(c_pad, w_ada, b_ada[None, :])[:B]
    mod = ada.reshape(B, 3, D)

    ts = min(ROW_TILE, S)
    w_in_b = w_in.astype(BF16)
    w_gm = w_in[:, _C_GMLA:].astype(BF16)
    w_uq_t = _pad_last(w_uq.reshape(Q_LORA_RANK, MLA_HEADS, MLA_QK_DIM), LANES)
    w_uq_t = w_uq_t.reshape(Q_LORA_RANK, MLA_PAD_WIDTH).T.astype(BF16)
    w_ukv_h = w_ukv.reshape(KV_LORA_RANK, MLA_HEADS, MLA_NOPE_DIM + MLA_V_DIM)
    w_uk_t = _pad_last(w_ukv_h[:, :, :MLA_NOPE_DIM], LANES)
    w_uk_t = w_uk_t.reshape(KV_LORA_RANK, MLA_PAD_WIDTH).T.astype(BF16)
    w_uv = w_ukv_h[:, :, MLA_NOPE_DIM:].reshape(KV_LORA_RANK, MLA_WIDTH).T.astype(BF16)
    gq = jnp.broadcast_to(_pad_last(q_head_norm, LANES)[:, None], (LANES, ts))
    gk = jnp.broadcast_to(_pad_last(k_head_norm, LANES)[:, None], (LANES, ts))

    (q_sb, k_sb, v_sb, g_sb, q_m, k_m, v_m, g_m) = _inproj(
        x, mod, norm_w[None, :], w_in_b, w_gm, q_lora_norm[None, :], w_uq_t, kv_lora_norm[None, :],
        w_uk_t, w_uv, gq, gk, *tables)

    mixed_sb = _sb_attention(q_sb, k_sb, v_sb, g_sb)
    mixed_mla = _mla_attention(q_m, k_m, v_m, g_m)
    return _outproj(mixed_sb, mixed_mla, x, mod, w_out.astype(BF16))


def _rope_tables(positions, dtype):
    inv_freq = ROPE_THETA ** (-jnp.arange(0, MLA_ROPE_DIM, 2, dtype=F32) / MLA_ROPE_DIM)
    ang = inv_freq[None, :, None] * positions.astype(F32)[:, None, :]
    return jnp.cos(ang).astype(dtype), jnp.sin(ang).astype(dtype)


def kernel(x, c, positions, w_ada, b_ada, norm_w, w_in, q_lora_norm, w_uq, kv_lora_norm, w_ukv,
           q_head_norm, k_head_norm, w_out):
    tables = _rope_tables(positions, x.dtype)
    for l in range(w_ada.shape[0]):
        x = _layer(x, c, tables, w_ada[l], b_ada[l], norm_w[l], w_in[l], q_lora_norm[l], w_uq[l],
                   kv_lora_norm[l], w_ukv[l], q_head_norm[l], k_head_norm[l], w_out[l])
    return x
```

```python
import functools
import math

import jax
import jax.numpy as jnp
from jax import lax
from jax.experimental import pallas as pl
from jax.experimental.pallas import tpu as pltpu

F32 = jnp.float32
BF16 = jnp.bfloat16

D_MODEL = 1024
SB_HEADS = 8
SB_HEAD_DIM = 64
SB_WIDTH = SB_HEADS * SB_HEAD_DIM
MLA_HEADS = 8
MLA_NOPE_DIM = 64
MLA_ROPE_DIM = 32
MLA_QK_DIM = MLA_NOPE_DIM + MLA_ROPE_DIM
MLA_V_DIM = 64
MLA_WIDTH = MLA_HEADS * MLA_V_DIM
MLA_V_ONES = 16
MLA_V_ROWS = MLA_V_DIM + MLA_V_ONES
Q_LORA_RANK = 384
KV_LORA_RANK = 256
ROPE_THETA = 10000.0
EPS = 1e-6

LANES = 128
HEAD_PAIR = 2
MLA_PAD_WIDTH = MLA_HEADS * LANES

_C_QSB = 0
_C_KSB = _C_QSB + SB_WIDTH
_C_VSB = _C_KSB + SB_WIDTH
_C_GSB = _C_VSB + SB_WIDTH
_C_CQ = _C_GSB + SB_WIDTH
_C_CKV = _C_CQ + Q_LORA_RANK
_C_KR = _C_CKV + KV_LORA_RANK
_C_GMLA = _C_KR + MLA_ROPE_DIM

ROW_TILE = 512
OUT_ROW_TILE = 2048
SB_TQ = 256
SB_TK = 256
SB_PAIRS = 4
SB_SUBS = 2
SB_FIRST_VISIT = 2
SB_STAGE_LAG = (2, 4)
MLA_GROUP = 2
MLA_UNROLL = 4
MLA_TQ = 512
MLA_TK = 256
VMEM_LIMIT = 56 * 1024 * 1024

SB_DEAD_LOG2_DROP = 160.0
SB_Q_SCALE = math.log2(math.e) / math.sqrt(SB_HEAD_DIM)
MLA_Q_SCALE = math.log2(math.e) / math.sqrt(MLA_QK_DIM)


def _silu(g):
    return g * (1.0 / (1.0 + jnp.exp(-g)))


def _dot(a, b):
    return jnp.dot(a, b, preferred_element_type=F32)


def _dot_nt(a, b):
    return lax.dot_general(a, b, (((1,), (1,)), ((), ())), preferred_element_type=F32)


def _dot_tn(a, b):
    return lax.dot_general(a, b, (((0,), (0,)), ((), ())), preferred_element_type=F32)


def _adaln_kernel(c_ref, w_ref, b_ref, o_ref):
    o_ref[...] = jnp.dot(_silu(c_ref[...]), w_ref[...], preferred_element_type=F32,
                         precision=lax.Precision.HIGHEST) + b_ref[...]


def _adaln(c_pad, w_ada, b_ada):
    rows, d = c_pad.shape
    n = w_ada.shape[1]
    bn = 512
    return pl.pallas_call(
        _adaln_kernel,
        grid=(n // bn,),
        in_specs=[pl.BlockSpec((rows, d), lambda j: (0, 0)),
                  pl.BlockSpec((d, bn), lambda j: (0, j)),
                  pl.BlockSpec((1, bn), lambda j: (0, j))],
        out_specs=pl.BlockSpec((rows, bn), lambda j: (0, j)),
        out_shape=jax.ShapeDtypeStruct((rows, n), F32),
        name="adaln",
    )(c_pad, w_ada, b_ada)


def _rope_t(x1, x2, cos, sin):
    return x1 * cos - x2 * sin, x2 * cos + x1 * sin


def _inproj_kernel(x_ref, mod_ref, nw_ref, win_ref, wgm_ref, qln_ref, wuqt_ref, kvln_ref, wukt_ref, wuvt_ref,
                   gq_ref, gk_ref, cos_ref, sin_ref,
                   qsb_ref, ksb_ref, vsb_ref, gsb_ref, qmt_ref, km_ref, vmt_ref, gm_ref):
    half = MLA_ROPE_DIM // 2
    n0, n1, n2 = MLA_NOPE_DIM, MLA_NOPE_DIM + half, MLA_QK_DIM
    x = x_ref[0]
    ts = x.shape[0]
    y = x * lax.rsqrt(jnp.mean(x * x, axis=-1, keepdims=True) + EPS) * nw_ref[...]
    h = (y * (1.0 + mod_ref[0, 1:2, :]) + mod_ref[0, 0:1, :]).astype(BF16)

    def proj(lo, hi):
        return _dot(h, win_ref[:, lo:hi])

    def rms(t, w_ref):
        return t * lax.rsqrt(jnp.mean(t * t, axis=-1, keepdims=True) + EPS) * w_ref[...]

    cq = proj(_C_CQ, _C_CKV)
    ckv = proj(_C_CKV, _C_KR)
    kr = proj(_C_KR, _C_KR + LANES)
    qsb_ref[0] = (proj(_C_QSB, _C_KSB) * SB_Q_SCALE).astype(BF16)
    ksb_ref[0] = proj(_C_KSB, _C_VSB).astype(BF16)

    cqn = rms(cq, qln_ref).astype(BF16)
    ckvn = rms(ckv, kvln_ref).astype(BF16)
    q_t = _dot_nt(wuqt_ref[...], cqn)
    k_t = _dot_nt(wukt_ref[...], ckvn)
    v_t = _dot_nt(wuvt_ref[...], ckvn)
    vsb_ref[0] = proj(_C_VSB, _C_GSB).astype(BF16)
    gsb_ref[0] = proj(_C_GSB, _C_CQ)
    gm_ref[0] = _dot(h, wgm_ref[...])

    cos, sin = cos_ref[0], sin_ref[0]
    pad_rows = jnp.zeros((LANES - MLA_QK_DIM, ts), F32)

    gq = gq_ref[...]
    for hd in range(MLA_HEADS):
        blk = q_t[hd * LANES:(hd + 1) * LANES]
        inv = lax.rsqrt(jnp.sum(blk * blk, axis=0, keepdims=True) * (1.0 / MLA_QK_DIM) + EPS)
        g = blk * gq
        r1, r2 = _rope_t(g[n0:n1], g[n1:n2], cos, sin)
        out = jnp.concatenate([g[:n0], r1, r2, pad_rows], axis=0) * (inv * MLA_Q_SCALE)
        qmt_ref[0, hd * LANES:(hd + 1) * LANES, :] = out.astype(BF16)

    ones_rows = jnp.ones((MLA_V_ONES, ts), F32)
    v_rows = []
    for hd in range(MLA_HEADS):
        v_rows += [v_t[hd * MLA_V_DIM:(hd + 1) * MLA_V_DIM], ones_rows]
    vmt_ref[0] = jnp.concatenate(v_rows, axis=0).astype(BF16)

    gk = gk_ref[...]
    kr_t = kr.T
    x1, x2 = kr_t[:half], kr_t[half:MLA_ROPE_DIM]
    kr_ssq = jnp.sum(x1 * x1 + x2 * x2, axis=0, keepdims=True)
    r1, r2 = _rope_t(x1 * gk[n0:n1], x2 * gk[n1:n2], cos, sin)
    for hd in range(MLA_HEADS):
        kn = k_t[hd * LANES:hd * LANES + n0]
        ssq = jnp.sum(kn * kn, axis=0, keepdims=True) + kr_ssq
        inv = lax.rsqrt(ssq * (1.0 / MLA_QK_DIM) + EPS)
        out = jnp.concatenate([kn * gk[:n0], r1, r2, pad_rows], axis=0) * inv
        km_ref[0, :, hd * LANES:(hd + 1) * LANES] = out.astype(BF16).T


def _inproj(x, mod, norm_w, w_in_b, w_gm, qln, w_uq_t, kvln, w_uk_t, w_uv, gq, gk, cos_t, sin_t):
    B, S, D = x.shape
    ts = min(ROW_TILE, S)
    grid = (B, S // ts)

    def whole(a):
        return pl.BlockSpec(a.shape, lambda b, i: (0,) * a.ndim)

    def rows(width):
        return pl.BlockSpec((1, ts, width), lambda b, i: (b, i, 0))

    def cols(height):
        return pl.BlockSpec((1, height, ts), lambda b, i: (b, 0, i))

    out_shapes = (
        jax.ShapeDtypeStruct((B, S, SB_WIDTH), BF16),
        jax.ShapeDtypeStruct((B, S, SB_WIDTH), BF16),
        jax.ShapeDtypeStruct((B, S, SB_WIDTH), BF16),
        jax.ShapeDtypeStruct((B, S, SB_WIDTH), F32),
        jax.ShapeDtypeStruct((B, MLA_PAD_WIDTH, S), BF16),
        jax.ShapeDtypeStruct((B, S, MLA_PAD_WIDTH), BF16),
        jax.ShapeDtypeStruct((B, MLA_HEADS * MLA_V_ROWS, S), BF16),
        jax.ShapeDtypeStruct((B, S, MLA_WIDTH), F32),
    )
    out_specs = [rows(s.shape[-1]) for s in out_shapes]
    out_specs[4] = cols(MLA_PAD_WIDTH)
    out_specs[6] = cols(MLA_HEADS * MLA_V_ROWS)
    half = MLA_ROPE_DIM // 2
    return pl.pallas_call(
        _inproj_kernel,
        grid=grid,
        in_specs=[rows(D),
                  pl.BlockSpec((1, 3, D), lambda b, i: (b, 0, 0)),
                  whole(norm_w), whole(w_in_b), whole(w_gm), whole(qln), whole(w_uq_t), whole(kvln),
                  whole(w_uk_t), whole(w_uv), whole(gq), whole(gk),
                  cols(half), cols(half)],
        out_specs=out_specs,
        out_shape=out_shapes,
        compiler_params=pltpu.CompilerParams(
            dimension_semantics=("arbitrary", "arbitrary"), vmem_limit_bytes=VMEM_LIMIT),
        name="inproj",
    )(x, mod, norm_w, w_in_b, w_gm, qln, w_uq_t, kvln, w_uk_t, w_uv, gq, gk, cos_t, sin_t)


def _sb_kernel(q_ref, k_ref, v_ref, g_ref, o_ref, acc_ref, carry_ref):
    tq, tk = SB_TQ, SB_TK
    n_heads = SB_PAIRS * HEAD_PAIR
    i = pl.program_id(2)
    lane = lax.broadcasted_iota(jnp.int32, (tq, LANES), 1)
    q_heads = []
    for sub in range(SB_SUBS):
        heads = []
        for p in range(SB_PAIRS):
            q_pair = q_ref[0, sub * tq:(sub + 1) * tq, p * LANES:(p + 1) * LANES]
            for hh in range(HEAD_PAIR):
                heads.append(jnp.where((lane // SB_HEAD_DIM) == hh, q_pair, jnp.zeros_like(q_pair)))
        q_heads.append(heads)

    strict = (lax.broadcasted_iota(jnp.int32, (tk, tq), 0)
              < lax.broadcasted_iota(jnp.int32, (tk, tq), 1))
    l_row = lax.broadcasted_iota(jnp.int32, (tk + 8, tk), 0)
    l_col = lax.broadcasted_iota(jnp.int32, (tk + 8, tk), 1)
    later = jnp.where(jnp.logical_or(l_col >= l_row, l_row >= tk), 1.0, 0.0).astype(F32)

    acc_ref[...] = jnp.zeros_like(acc_ref)
    carry_ref[...] = jnp.zeros_like(carry_ref)

    def visit(plan, diag_first):
        depth = max(len(tiles) for _, tiles in plan)
        items = [(sub, t, tiles[t], h) for t in range(depth) for sub, tiles in plan
                 if t < len(tiles) for h in range(n_heads)]

        def pair_block(ref, j, h):
            p = h // HEAD_PAIR
            return ref[0, pl.ds(pl.multiple_of(j * tk, tk), tk), p * LANES:(p + 1) * LANES]

        n = len(items)
        zs, afters, col_sums = [None] * n, [None] * n, [None] * n
        carries = {(sub, h): carry_ref[sub, h] for sub, _ in plan for h in range(n_heads)}
        outs = {key: None for key in carries}

        def scores(idx):
            sub, _, j, h = items[idx]
            zs[idx] = _dot_nt(pair_block(k_ref, j, h), q_heads[sub][h])

        def drops(idx):
            _, t, _, _ = items[idx]
            z = zs[idx]
            drop = jnp.maximum(z, 0.0) + jnp.log2(1.0 + jnp.exp2(-jnp.abs(z)))
            if diag_first and t == 0:
                drop = jnp.where(strict, drop, 0.0)
            hi = drop.astype(BF16).astype(F32)
            sums = _dot(later, hi) + _dot(later, drop - hi)
            afters[idx] = sums[:tk]
            col_sums[idx] = sums[tk:tk + 1]

        def weights(idx):
            sub, t, j, h = items[idx]
            w = jnp.exp2(zs[idx] - (afters[idx] + carries[sub, h]))
            if diag_first and t == 0:
                w = jnp.where(strict, w, 0.0)
            pv = _dot_tn(pair_block(v_ref, j, h), w.astype(BF16))
            lo_row = (h % HEAD_PAIR) * SB_HEAD_DIM
            pv = pv[lo_row:lo_row + SB_HEAD_DIM, :]
            outs[sub, h] = pv if outs[sub, h] is None else outs[sub, h] + pv
            carries[sub, h] = carries[sub, h] + col_sums[idx]

        for step in range(n + SB_STAGE_LAG[1]):
            if step < n:
                scores(step)
            if 0 <= step - SB_STAGE_LAG[0] < n:
                drops(step - SB_STAGE_LAG[0])
            if 0 <= step - SB_STAGE_LAG[1] < n:
                weights(step - SB_STAGE_LAG[1])
        alive = []
        for sub, _ in plan:
            dead = None
            for h in range(n_heads):
                acc_ref[sub, h] += outs[sub, h]
                carry_ref[sub, h] = carries[sub, h]
                dead = carries[sub, h] if dead is None else jnp.minimum(dead, carries[sub, h])
            alive.append(jnp.min(dead) < SB_DEAD_LOG2_DROP)
        return alive

    first = SB_SUBS * i

    def usual_step():
        return visit([(sub, [first + sub - t for t in range(SB_FIRST_VISIT)])
                      for sub in range(SB_SUBS)], True)

    def first_step():
        return visit([(sub, [sub - t for t in range(SB_FIRST_VISIT) if sub - t >= 0])
                      for sub in range(SB_SUBS)], True)

    assert SB_SUBS >= SB_FIRST_VISIT - 1
    alive = lax.cond(i > 0, usual_step, first_step)

    for sub in range(SB_SUBS):
        def cond(state):
            j, still_alive = state
            return jnp.logical_and(j >= 0, still_alive)

        def body(state, sub=sub):
            j, _ = state
            return j - 1, visit([(sub, [j])], False)[0]

        lax.while_loop(cond, body, (first + sub - SB_FIRST_VISIT, alive[sub]))

    for sub in range(SB_SUBS):
        rows = slice(sub * tq, (sub + 1) * tq)
        o_t = acc_ref[sub].reshape(n_heads * SB_HEAD_DIM, tq)
        o_ref[0, rows, :] = (o_t.T * _silu(g_ref[0, rows, :])).astype(o_ref.dtype)


def _sb_attention(q, k, v, g):
    B, S, W = q.shape
    rows = SB_SUBS * SB_TQ
    assert S % rows == 0 and SB_TQ == SB_TK
    wb = SB_PAIRS * LANES
    n_heads = SB_PAIRS * HEAD_PAIR
    grid = (B, W // wb, S // rows)
    q_spec = pl.BlockSpec((1, rows, wb), lambda b, p, i: (b, i, p))
    kv_spec = pl.BlockSpec((1, S, wb), lambda b, p, i: (b, 0, p))
    return pl.pallas_call(
        _sb_kernel,
        grid=grid,
        in_specs=[q_spec, kv_spec, kv_spec, q_spec],
        out_specs=q_spec,
        out_shape=jax.ShapeDtypeStruct((B, S, W), BF16),
        scratch_shapes=[pltpu.VMEM((SB_SUBS, n_heads, SB_HEAD_DIM, SB_TQ), F32),
                        pltpu.VMEM((SB_SUBS, n_heads, 1, SB_TQ), F32)],
        compiler_params=pltpu.CompilerParams(
            dimension_semantics=("arbitrary", "arbitrary", "arbitrary"),
            vmem_limit_bytes=VMEM_LIMIT),
        name="sb_attn",
    )(q, k, v, g)


def _mla_kernel(q_ref, qnext_ref, k_ref, v_ref, g_ref, o_ref, acc_ref, m_ref, s_ref, smax_ref):
    tq, tk = MLA_TQ, MLA_TK
    diag_tiles = tq // tk
    i = pl.program_id(2)
    neg = jnp.finfo(F32).min

    acc_ref[...] = jnp.zeros_like(acc_ref)
    m_ref[...] = jnp.full_like(m_ref, neg)

    def score(j, slot, diag=None, queries=q_ref):
        start = pl.multiple_of(j * tk, tk)
        lo = 0 if diag is None else diag * tk
        for hh in range(MLA_GROUP):
            q_t = queries[0, hh * LANES:(hh + 1) * LANES, lo:]
            k_blk = k_ref[0, pl.ds(start, tk), hh * LANES:(hh + 1) * LANES]
            s = _dot(k_blk, q_t)
            s_ref[slot, hh, :, lo:] = s
            if diag is None:
                smax_ref[slot, hh] = jnp.broadcast_to(jnp.max(s, axis=0, keepdims=True), (8, tq))

    def consume(j, slot, diag=None):
        start = pl.multiple_of(j * tk, tk)
        lo = 0 if diag is None else diag * tk
        for hh in range(MLA_GROUP):
            v_t = v_ref[0, hh * MLA_V_ROWS:(hh + 1) * MLA_V_ROWS, pl.ds(start, tk)]
            s = s_ref[slot, hh, :, lo:]
            if diag is None:
                tile_max = smax_ref[slot, hh]
            else:
                causal = (lax.broadcasted_iota(jnp.int32, s.shape, 0)
                          <= lax.broadcasted_iota(jnp.int32, s.shape, 1))
                s = jnp.where(causal, s, neg)
                tile_max = jnp.max(s, axis=0, keepdims=True)
            m_old = m_ref[hh, :, lo:]
            m_new = jnp.maximum(m_old, tile_max)
            alpha = jnp.exp2(m_old[:1] - m_new[:1])
            p = jnp.exp2(s - m_new[:1])
            acc_ref[hh, :, lo:] = alpha * acc_ref[hh, :, lo:] + _dot(v_t, p.astype(BF16))
            m_ref[hh, :, lo:] = m_new

    base = i * diag_tiles

    def two_tiles(t):
        score(t + 1, 1)
        consume(t, 0)
        score(t + 2, 0)
        consume(t + 1, 1)

    looped = jnp.maximum(base - 2, 0)
    done = 0
    pairs = MLA_UNROLL
    while pairs >= 1:
        def group(jj, carry, pairs=pairs, done=done):
            for r in range(pairs):
                two_tiles(done + 2 * pairs * jj + 2 * r)
            return carry

        n_groups = (looped - done) // (2 * pairs)
        lax.fori_loop(0, n_groups, group, 0)
        done = done + n_groups * 2 * pairs
        pairs //= 2

    def diagonal_block():
        for d in range(diag_tiles):
            if d + 1 < diag_tiles:
                score(base + d + 1, (d + 1) % 2, d + 1)
            consume(base + d, d % 2, d)
            if d == 0:
                score(0, 0, queries=qnext_ref)
        o_t = jnp.concatenate(
            [acc_ref[hh, :MLA_V_DIM] / acc_ref[hh, MLA_V_DIM:MLA_V_DIM + 1] for hh in range(MLA_GROUP)],
            axis=0)
        o_ref[0] = (o_t.T * _silu(g_ref[0])).astype(o_ref.dtype)

    @pl.when(i > 0)
    def _():
        two_tiles(base - 2)
        diagonal_block()

    @pl.when(i == 0)
    def _():
        score(0, 0)
        diagonal_block()


def _mla_attention(q, k, v, g):
    B, S, W = g.shape
    tq = MLA_TQ
    group_rows = MLA_GROUP * MLA_V_ROWS
    group_out = MLA_GROUP * MLA_V_DIM
    assert S % tq == 0 and MLA_TQ % (2 * MLA_TK) == 0 and group_out % LANES == 0
    grid = (B, W // group_out, S // tq)
    return pl.pallas_call(
        _mla_kernel,
        grid=grid,
        in_specs=[pl.BlockSpec((1, MLA_GROUP * LANES, tq), lambda b, p, i: (b, p, i)),
                  pl.BlockSpec((1, MLA_GROUP * LANES, tq),
                               lambda b, p, i: (b, p, jnp.minimum(i + 1, S // tq - 1))),
                  pl.BlockSpec((1, S, MLA_GROUP * LANES), lambda b, p, i: (b, 0, p)),
                  pl.BlockSpec((1, group_rows, S), lambda b, p, i: (b, p, 0)),
                  pl.BlockSpec((1, tq, group_out), lambda b, p, i: (b, i, p))],
        out_specs=pl.BlockSpec((1, tq, group_out), lambda b, p, i: (b, i, p)),
        out_shape=jax.ShapeDtypeStruct((B, S, W), BF16),
        scratch_shapes=[pltpu.VMEM((MLA_GROUP, MLA_V_ROWS, tq), F32),
                        pltpu.VMEM((MLA_GROUP, 8, tq), F32),
                        pltpu.VMEM((2, MLA_GROUP, MLA_TK, tq), F32),
                        pltpu.VMEM((2, MLA_GROUP, 8, tq), F32)],
        compiler_params=pltpu.CompilerParams(
            dimension_semantics=("arbitrary", "arbitrary", "arbitrary"),
            vmem_limit_bytes=VMEM_LIMIT),
        name="mla_attn",
    )(q, q, k, v, g)


def _outproj_kernel(msb_ref, mmla_ref, x_ref, mod_ref, w_ref, o_ref):
    y = _dot(msb_ref[0], w_ref[:SB_WIDTH, :]) + _dot(mmla_ref[0], w_ref[SB_WIDTH:, :])
    o_ref[0] = x_ref[0] + mod_ref[0, 2:3, :] * y


def _outproj(mixed_sb, mixed_mla, x, mod, w_out):
    B, S, D = x.shape
    ts = min(OUT_ROW_TILE, S)

    def rows(width):
        return pl.BlockSpec((1, ts, width), lambda b, i: (b, i, 0))

    return pl.pallas_call(
        _outproj_kernel,
        grid=(B, S // ts),
        in_specs=[rows(SB_WIDTH), rows(MLA_WIDTH), rows(D),
                  pl.BlockSpec((1, 3, D), lambda b, i: (b, 0, 0)),
                  pl.BlockSpec(w_out.shape, lambda b, i: (0, 0))],
        out_specs=rows(D),
        out_shape=jax.ShapeDtypeStruct((B, S, D), x.dtype),
        compiler_params=pltpu.CompilerParams(
            dimension_semantics=("arbitrary", "arbitrary"), vmem_limit_bytes=VMEM_LIMIT),
        name="outproj",
    )(mixed_sb, mixed_mla, x, mod, w_out)


def _pad_last(a, width):
    return jnp.pad(a, [(0, 0)] * (a.ndim - 1) + [(0, width - a.shape[-1])])


def _layer(x, c, tables, w_ada, b_ada, norm_w, w_in, q_lora_norm, w_uq, kv_lora_norm, w_ukv,
           q_head_norm, k_head_norm, w_out):
    B, S, D = x.shape
    c_pad = jnp.pad(c, ((0, 8 - B), (0, 0)))
    ada = _adaln(c_pad, w_ada, b_ada[None, :])[:B]
    mod = ada.reshape(B, 3, D)

    ts = min(ROW_TILE, S)
    w_in_b = w_in.astype(BF16)
    w_gm = w_in[:, _C_GMLA:].astype(BF16)
    w_uq_t = _pad_last(w_uq.reshape(Q_LORA_RANK, MLA_HEADS, MLA_QK_DIM), LANES)
    w_uq_t = w_uq_t.reshape(Q_LORA_RANK, MLA_PAD_WIDTH).T.astype(BF16)
    w_ukv_h = w_ukv.reshape(KV_LORA_RANK, MLA_HEADS, MLA_NOPE_DIM + MLA_V_DIM)
    w_uk_t = _pad_last(w_ukv_h[:, :, :MLA_NOPE_DIM], LANES)
    w_uk_t = w_uk_t.reshape(KV_LORA_RANK, MLA_PAD_WIDTH).T.astype(BF16)
    w_uv = w_ukv_h[:, :, MLA_NOPE_DIM:].reshape(KV_LORA_RANK, MLA_WIDTH).T.astype(BF16)
    gq = jnp.broadcast_to(_pad_last(q_head_norm, LANES)[:, None], (LANES, ts))
    gk = jnp.broadcast_to(_pad_last(k_head_norm, LANES)[:, None], (LANES, ts))

    (q_sb, k_sb, v_sb, g_sb, q_m, k_m, v_m, g_m) = _inproj(
        x, mod, norm_w[None, :], w_in_b, w_gm, q_lora_norm[None, :], w_uq_t, kv_lora_norm[None, :],
        w_uk_t, w_uv, gq, gk, *tables)

    mixed_sb = _sb_attention(q_sb, k_sb, v_sb, g_sb)
    mixed_mla = _mla_attention(q_m, k_m, v_m, g_m)
    return _outproj(mixed_sb, mixed_mla, x, mod, w_out.astype(BF16))


def _rope_tables(positions, dtype):
    inv_freq = ROPE_THETA ** (-jnp.arange(0, MLA_ROPE_DIM, 2, dtype=F32) / MLA_ROPE_DIM)
    ang = inv_freq[None, :, None] * positions.astype(F32)[:, None, :]
    return jnp.cos(ang).astype(dtype), jnp.sin(ang).astype(dtype)


def kernel(x, c, positions, w_ada, b_ada, norm_w, w_in, q_lora_norm, w_uq, kv_lora_norm, w_ukv,
           q_head_norm, k_head_norm, w_out):
    tables = _rope_tables(positions, x.dtype)
    for l in range(w_ada.shape[0]):
        x = _layer(x, c, tables, w_ada[l], b_ada[l], norm_w[l], w_in[l], q_lora_norm[l], w_uq[l],
                   kv_lora_norm[l], w_ukv[l], q_head_norm[l], k_head_norm[l], w_out[l])
    return x
```

```python
import math

import jax
import jax.numpy as jnp
from jax import lax
from jax.experimental import pallas as pl
from jax.experimental.pallas import tpu as pltpu

F32 = jnp.float32
BF16 = jnp.bfloat16

SB_HEADS = 8
SB_HEAD_DIM = 64
SB_WIDTH = SB_HEADS * SB_HEAD_DIM
MLA_HEADS = 8
MLA_NOPE_DIM = 64
MLA_ROPE_DIM = 32
MLA_QK_DIM = MLA_NOPE_DIM + MLA_ROPE_DIM
MLA_V_DIM = 64
MLA_WIDTH = MLA_HEADS * MLA_V_DIM
MLA_V_ONES = 16
MLA_V_ROWS = MLA_V_DIM + MLA_V_ONES
Q_LORA_RANK = 384
KV_LORA_RANK = 256
ROPE_THETA = 10000.0
EPS = 1e-6

LANES = 128
SUBLANES = 8
HEAD_PAIR = 2
MLA_PAD_WIDTH = MLA_HEADS * LANES

_C_QSB = 0
_C_KSB = _C_QSB + SB_WIDTH
_C_VSB = _C_KSB + SB_WIDTH
_C_GSB = _C_VSB + SB_WIDTH
_C_CQ = _C_GSB + SB_WIDTH
_C_CKV = _C_CQ + Q_LORA_RANK
_C_KR = _C_CKV + KV_LORA_RANK
_C_GMLA = _C_KR + MLA_ROPE_DIM

ADA_ROW_TILE = 256
ROW_TILE = 512
OUT_ROW_TILE = 1024
SB_TQ = 256
SB_TK = 256
SB_PAIRS = 4
SB_SUBS = 2
SB_FIRST_VISIT = 2
SB_STAGE_LAG = (2, 4)
MLA_GROUP = 2
MLA_UNROLL = 4
MLA_TQ = 512
MLA_TK = 256
VMEM_LIMIT = 56 * 1024 * 1024

SB_DEAD_LOG2_DROP = 160.0
SB_Q_SCALE = math.log2(math.e) / math.sqrt(SB_HEAD_DIM)
MLA_Q_SCALE = math.log2(math.e) / math.sqrt(MLA_QK_DIM)


def _silu(g):
    return g * (1.0 / (1.0 + jnp.exp(-g)))


def _dot(a, b):
    return jnp.dot(a, b, preferred_element_type=F32)


def _dot_nt(a, b):
    return lax.dot_general(a, b, (((1,), (1,)), ((), ())), preferred_element_type=F32)


def _dot_tn(a, b):
    return lax.dot_general(a, b, (((0,), (0,)), ((), ())), preferred_element_type=F32)


def _adaln_kernel(ct_ref, w_ref, b_ref, o_ref):
    @pl.when(pl.program_id(0) == 0)
    def _():
        o_ref[...] = jnp.broadcast_to(b_ref[...], o_ref.shape)

    s_t = _silu(ct_ref[...])
    w = w_ref[...]
    rows = [jnp.sum(w * s_t[:, b:b + 1], axis=0, keepdims=True) for b in range(s_t.shape[1])]
    o_ref[...] += jnp.concatenate(rows, axis=0)


def _adaln(c_t, w_ada, b_ada):
    d, rows = c_t.shape
    n = w_ada.shape[1]
    kb = ADA_ROW_TILE
    return pl.pallas_call(
        _adaln_kernel,
        grid=(d // kb,),
        in_specs=[pl.BlockSpec((kb, rows), lambda k: (k, 0)),
                  pl.BlockSpec((kb, n), lambda k: (k, 0)),
                  pl.BlockSpec((1, n), lambda k: (0, 0))],
        out_specs=pl.BlockSpec((rows, n), lambda k: (0, 0)),
        out_shape=jax.ShapeDtypeStruct((rows, n), F32),
        compiler_params=pltpu.CompilerParams(dimension_semantics=("arbitrary",)),
        name="adaln",
    )(c_t, w_ada, b_ada)


def _rope_t(x1, x2, cos, sin):
    return x1 * cos - x2 * sin, x2 * cos + x1 * sin


def _inproj_kernel(x_ref, mod_ref, nw_ref, win_ref, wgm_ref, qln_ref, wuqt_ref, kvln_ref, wukt_ref, wuvt_ref,
                   gq_ref, gk_ref, cos_ref, sin_ref,
                   qsb_ref, ksb_ref, vsb_ref, gsb_ref, qmt_ref, km_ref, vmt_ref, gm_ref):
    half = MLA_ROPE_DIM // 2
    n0, n1, n2 = MLA_NOPE_DIM, MLA_NOPE_DIM + half, MLA_QK_DIM
    x = x_ref[0]
    ts = x.shape[0]
    y = x * lax.rsqrt(jnp.mean(x * x, axis=-1, keepdims=True) + EPS) * nw_ref[...]
    h = (y * (1.0 + mod_ref[0, 1:2, :]) + mod_ref[0, 0:1, :]).astype(BF16)

    def proj(lo, hi):
        return _dot(h, win_ref[:, lo:hi])

    def rms(t, w_ref):
        return t * lax.rsqrt(jnp.mean(t * t, axis=-1, keepdims=True) + EPS) * w_ref[...]

    cq = proj(_C_CQ, _C_CKV)
    ckv = proj(_C_CKV, _C_KR)
    kr = proj(_C_KR, _C_KR + LANES)
    qsb_ref[0] = (proj(_C_QSB, _C_KSB) * SB_Q_SCALE).astype(BF16)
    ksb_ref[0] = proj(_C_KSB, _C_VSB).astype(BF16)

    cqn = rms(cq, qln_ref).astype(BF16)
    ckvn = rms(ckv, kvln_ref).astype(BF16)
    q_t = _dot_nt(wuqt_ref[...], cqn)
    k_t = _dot_nt(wukt_ref[...], ckvn)
    v_t = _dot_nt(wuvt_ref[...], ckvn)
    vsb_ref[0] = proj(_C_VSB, _C_GSB).astype(BF16)
    gsb_ref[0] = proj(_C_GSB, _C_CQ)
    gm_ref[0] = _dot(h, wgm_ref[...])

    cos, sin = cos_ref[0], sin_ref[0]
    pad_rows = jnp.zeros((LANES - MLA_QK_DIM, ts), F32)

    gq = gq_ref[...]
    for hd in range(MLA_HEADS):
        blk = q_t[hd * LANES:(hd + 1) * LANES]
        inv = lax.rsqrt(jnp.sum(blk * blk, axis=0, keepdims=True) * (1.0 / MLA_QK_DIM) + EPS)
        g = blk * gq
        r1, r2 = _rope_t(g[n0:n1], g[n1:n2], cos, sin)
        out = jnp.concatenate([g[:n0], r1, r2, pad_rows], axis=0) * (inv * MLA_Q_SCALE)
        qmt_ref[0, hd * LANES:(hd + 1) * LANES, :] = out.astype(BF16)

    ones_rows = jnp.ones((MLA_V_ONES, ts), F32)
    v_rows = []
    for hd in range(MLA_HEADS):
        v_rows += [v_t[hd * MLA_V_DIM:(hd + 1) * MLA_V_DIM], ones_rows]
    vmt_ref[0] = jnp.concatenate(v_rows, axis=0).astype(BF16)

    gk = gk_ref[...]
    kr_t = kr.T
    x1, x2 = kr_t[:half], kr_t[half:MLA_ROPE_DIM]
    kr_ssq = jnp.sum(x1 * x1 + x2 * x2, axis=0, keepdims=True)
    r1, r2 = _rope_t(x1 * gk[n0:n1], x2 * gk[n1:n2], cos, sin)
    for hd in range(MLA_HEADS):
        kn = k_t[hd * LANES:hd * LANES + n0]
        ssq = jnp.sum(kn * kn, axis=0, keepdims=True) + kr_ssq
        inv = lax.rsqrt(ssq * (1.0 / MLA_QK_DIM) + EPS)
        out = jnp.concatenate([kn * gk[:n0], r1, r2, pad_rows], axis=0) * inv
        km_ref[0, :, hd * LANES:(hd + 1) * LANES] = out.astype(BF16).T


def _inproj(x, mod, norm_w, w_in_b, w_gm, qln, w_uq_t, kvln, w_uk_t, w_uv, gq, gk, cos_t, sin_t):
    B, S, D = x.shape
    ts = min(ROW_TILE, S)
    grid = (B, S // ts)

    def whole(a):
        return pl.BlockSpec(a.shape, lambda b, i: (0,) * a.ndim)

    def rows(width):
        return pl.BlockSpec((1, ts, width), lambda b, i: (b, i, 0))

    def cols(height):
        return pl.BlockSpec((1, height, ts), lambda b, i: (b, 0, i))

    out_shapes = (
        jax.ShapeDtypeStruct((B, S, SB_WIDTH), BF16),
        jax.ShapeDtypeStruct((B, S, SB_WIDTH), BF16),
        jax.ShapeDtypeStruct((B, S, SB_WIDTH), BF16),
        jax.ShapeDtypeStruct((B, S, SB_WIDTH), F32),
        jax.ShapeDtypeStruct((B, MLA_PAD_WIDTH, S), BF16),
        jax.ShapeDtypeStruct((B, S, MLA_PAD_WIDTH), BF16),
        jax.ShapeDtypeStruct((B, MLA_HEADS * MLA_V_ROWS, S), BF16),
        jax.ShapeDtypeStruct((B, S, MLA_WIDTH), F32),
    )
    out_specs = [rows(s.shape[-1]) for s in out_shapes]
    out_specs[4] = cols(MLA_PAD_WIDTH)
    out_specs[6] = cols(MLA_HEADS * MLA_V_ROWS)
    half = MLA_ROPE_DIM // 2
    return pl.pallas_call(
        _inproj_kernel,
        grid=grid,
        in_specs=[rows(D),
                  pl.BlockSpec((1, 3, D), lambda b, i: (b, 0, 0)),
                  whole(norm_w), whole(w_in_b), whole(w_gm), whole(qln), whole(w_uq_t), whole(kvln),
                  whole(w_uk_t), whole(w_uv), whole(gq), whole(gk),
                  cols(half), cols(half)],
        out_specs=out_specs,
        out_shape=out_shapes,
        compiler_params=pltpu.CompilerParams(
            dimension_semantics=("arbitrary", "arbitrary"), vmem_limit_bytes=VMEM_LIMIT),
        name="inproj",
    )(x, mod, norm_w, w_in_b, w_gm, qln, w_uq_t, kvln, w_uk_t, w_uv, gq, gk, cos_t, sin_t)


def _sb_kernel(q_ref, k_ref, v_ref, g_ref, o_ref, acc_ref, carry_ref):
    tq, tk = SB_TQ, SB_TK
    n_heads = SB_PAIRS * HEAD_PAIR
    i = pl.program_id(2)
    lane = lax.broadcasted_iota(jnp.int32, (tq, LANES), 1)
    q_heads = []
    for sub in range(SB_SUBS):
        heads = []
        for p in range(SB_PAIRS):
            q_pair = q_ref[0, sub * tq:(sub + 1) * tq, p * LANES:(p + 1) * LANES]
            for hh in range(HEAD_PAIR):
                heads.append(jnp.where((lane // SB_HEAD_DIM) == hh, q_pair, jnp.zeros_like(q_pair)))
        q_heads.append(heads)

    strict = (lax.broadcasted_iota(jnp.int32, (tk, tq), 0)
              < lax.broadcasted_iota(jnp.int32, (tk, tq), 1))
    l_row = lax.broadcasted_iota(jnp.int32, (tk + SUBLANES, tk), 0)
    l_col = lax.broadcasted_iota(jnp.int32, (tk + SUBLANES, tk), 1)
    later = jnp.where(jnp.logical_or(l_col >= l_row, l_row >= tk), 1.0, 0.0).astype(F32)

    acc_ref[...] = jnp.zeros_like(acc_ref)
    carry_ref[...] = jnp.zeros_like(carry_ref)

    def visit(plan, diag_first):
        depth = max(len(tiles) for _, tiles in plan)
        items = [(sub, t, tiles[t], h) for t in range(depth) for sub, tiles in plan
                 if t < len(tiles) for h in range(n_heads)]

        def pair_block(ref, j, h):
            p = h // HEAD_PAIR
            return ref[0, pl.ds(pl.multiple_of(j * tk, tk), tk), p * LANES:(p + 1) * LANES]

        n = len(items)
        zs, afters, col_sums = [None] * n, [None] * n, [None] * n
        carries = {(sub, h): carry_ref[sub, h] for sub, _ in plan for h in range(n_heads)}
        outs = {key: None for key in carries}

        def scores(idx):
            sub, _, j, h = items[idx]
            zs[idx] = _dot_nt(pair_block(k_ref, j, h), q_heads[sub][h])

        def drops(idx):
            _, t, _, _ = items[idx]
            z = zs[idx]
            drop = jnp.maximum(z, 0.0) + jnp.log2(1.0 + jnp.exp2(-jnp.abs(z)))
            if diag_first and t == 0:
                drop = jnp.where(strict, drop, 0.0)
            hi = drop.astype(BF16).astype(F32)
            sums = _dot(later, hi) + _dot(later, drop - hi)
            afters[idx] = sums[:tk]
            col_sums[idx] = sums[tk:tk + 1]

        def weights(idx):
            sub, t, j, h = items[idx]
            w = jnp.exp2(zs[idx] - (afters[idx] + carries[sub, h]))
            if diag_first and t == 0:
                w = jnp.where(strict, w, 0.0)
            pv = _dot_tn(pair_block(v_ref, j, h), w.astype(BF16))
            lo_row = (h % HEAD_PAIR) * SB_HEAD_DIM
            pv = pv[lo_row:lo_row + SB_HEAD_DIM, :]
            outs[sub, h] = pv if outs[sub, h] is None else outs[sub, h] + pv
            carries[sub, h] = carries[sub, h] + col_sums[idx]

        for step in range(n + SB_STAGE_LAG[1]):
            if step < n:
                scores(step)
            if 0 <= step - SB_STAGE_LAG[0] < n:
                drops(step - SB_STAGE_LAG[0])
            if 0 <= step - SB_STAGE_LAG[1] < n:
                weights(step - SB_STAGE_LAG[1])
        alive = []
        for sub, _ in plan:
            dead = None
            for h in range(n_heads):
                acc_ref[sub, h] += outs[sub, h]
                carry_ref[sub, h] = carries[sub, h]
                dead = carries[sub, h] if dead is None else jnp.minimum(dead, carries[sub, h])
            alive.append(jnp.min(dead) < SB_DEAD_LOG2_DROP)
        return alive

    first = SB_SUBS * i

    def usual_step():
        return visit([(sub, [first + sub - t for t in range(SB_FIRST_VISIT)])
                      for sub in range(SB_SUBS)], True)

    def first_step():
        return visit([(sub, [sub - t for t in range(SB_FIRST_VISIT) if sub - t >= 0])
                      for sub in range(SB_SUBS)], True)

    assert SB_SUBS >= SB_FIRST_VISIT - 1
    alive = lax.cond(i > 0, usual_step, first_step)

    for sub in range(SB_SUBS):
        def cond(state):
            j, still_alive = state
            return jnp.logical_and(j >= 0, still_alive)

        def body(state, sub=sub):
            j, _ = state
            return j - 1, visit([(sub, [j])], False)[0]

        lax.while_loop(cond, body, (first + sub - SB_FIRST_VISIT, alive[sub]))

    for sub in range(SB_SUBS):
        rows = slice(sub * tq, (sub + 1) * tq)
        o_t = acc_ref[sub].reshape(n_heads * SB_HEAD_DIM, tq)
        o_ref[0, rows, :] = (o_t.T * _silu(g_ref[0, rows, :])).astype(o_ref.dtype)


def _sb_attention(q, k, v, g):
    B, S, W = q.shape
    rows = SB_SUBS * SB_TQ
    assert S % rows == 0 and SB_TQ == SB_TK
    wb = SB_PAIRS * LANES
    n_heads = SB_PAIRS * HEAD_PAIR
    grid = (B, W // wb, S // rows)
    q_spec = pl.BlockSpec((1, rows, wb), lambda b, p, i: (b, i, p))
    kv_spec = pl.BlockSpec((1, S, wb), lambda b, p, i: (b, 0, p))
    return pl.pallas_call(
        _sb_kernel,
        grid=grid,
        in_specs=[q_spec, kv_spec, kv_spec, q_spec],
        out_specs=q_spec,
        out_shape=jax.ShapeDtypeStruct((B, S, W), BF16),
        scratch_shapes=[pltpu.VMEM((SB_SUBS, n_heads, SB_HEAD_DIM, SB_TQ), F32),
                        pltpu.VMEM((SB_SUBS, n_heads, 1, SB_TQ), F32)],
        compiler_params=pltpu.CompilerParams(
            dimension_semantics=("arbitrary", "arbitrary", "arbitrary"),
            vmem_limit_bytes=VMEM_LIMIT),
        name="sb_attn",
    )(q, k, v, g)


def _mla_kernel(q_ref, qnext_ref, k_ref, v_ref, g_ref, o_ref, acc_ref, m_ref, s_ref, smax_ref):
    tq, tk = MLA_TQ, MLA_TK
    diag_tiles = tq // tk
    i = pl.program_id(2)
    neg = jnp.finfo(F32).min

    acc_ref[...] = jnp.zeros_like(acc_ref)
    m_ref[...] = jnp.full_like(m_ref, neg)

    def score(j, slot, diag=None, queries=q_ref):
        start = pl.multiple_of(j * tk, tk)
        lo = 0 if diag is None else diag * tk
        for hh in range(MLA_GROUP):
            q_t = queries[0, hh * LANES:(hh + 1) * LANES, lo:]
            k_blk = k_ref[0, pl.ds(start, tk), hh * LANES:(hh + 1) * LANES]
            s = _dot(k_blk, q_t)
            s_ref[slot, hh, :, lo:] = s
            if diag is None:
                smax_ref[slot, hh] = jnp.broadcast_to(jnp.max(s, axis=0, keepdims=True), (SUBLANES, tq))

    def consume(j, slot, diag=None):
        start = pl.multiple_of(j * tk, tk)
        lo = 0 if diag is None else diag * tk
        for hh in range(MLA_GROUP):
            v_t = v_ref[0, hh * MLA_V_ROWS:(hh + 1) * MLA_V_ROWS, pl.ds(start, tk)]
            s = s_ref[slot, hh, :, lo:]
            if diag is None:
                tile_max = smax_ref[slot, hh]
            else:
                causal = (lax.broadcasted_iota(jnp.int32, s.shape, 0)
                          <= lax.broadcasted_iota(jnp.int32, s.shape, 1))
                s = jnp.where(causal, s, neg)
                tile_max = jnp.max(s, axis=0, keepdims=True)
            m_old = m_ref[hh, :, lo:]
            m_new = jnp.maximum(m_old, tile_max)
            alpha = jnp.exp2(m_old[:1] - m_new[:1])
            p = jnp.exp2(s - m_new[:1])
            acc_ref[hh, :, lo:] = alpha * acc_ref[hh, :, lo:] + _dot(v_t, p.astype(BF16))
            m_ref[hh, :, lo:] = m_new

    base = i * diag_tiles

    def two_tiles(t):
        score(t + 1, 1)
        consume(t, 0)
        score(t + 2, 0)
        consume(t + 1, 1)

    looped = jnp.maximum(base - 2, 0)
    done = 0
    pairs = MLA_UNROLL
    while pairs >= 1:
        def group(jj, carry, pairs=pairs, done=done):
            for r in range(pairs):
                two_tiles(done + 2 * pairs * jj + 2 * r)
            return carry

        n_groups = (looped - done) // (2 * pairs)
        lax.fori_loop(0, n_groups, group, 0)
        done = done + n_groups * 2 * pairs
        pairs //= 2

    def diagonal_block():
        for d in range(diag_tiles):
            if d + 1 < diag_tiles:
                score(base + d + 1, (d + 1) % 2, d + 1)
            consume(base + d, d % 2, d)
            if d == 0:
                score(0, 0, queries=qnext_ref)
        o_t = jnp.concatenate(
            [acc_ref[hh, :MLA_V_DIM] / acc_ref[hh, MLA_V_DIM:MLA_V_DIM + 1] for hh in range(MLA_GROUP)],
            axis=0)
        o_ref[0] = (o_t.T * _silu(g_ref[0])).astype(o_ref.dtype)

    @pl.when(i > 0)
    def _():
        two_tiles(base - 2)
        diagonal_block()

    @pl.when(i == 0)
    def _():
        score(0, 0)
        diagonal_block()


def _mla_attention(q, k, v, g):
    B, S, W = g.shape
    tq = MLA_TQ
    group_rows = MLA_GROUP * MLA_V_ROWS
    group_out = MLA_GROUP * MLA_V_DIM
    assert S % tq == 0 and MLA_TQ % (2 * MLA_TK) == 0 and group_out % LANES == 0
    grid = (B, W // group_out, S // tq)
    return pl.pallas_call(
        _mla_kernel,
        grid=grid,
        in_specs=[pl.BlockSpec((1, MLA_GROUP * LANES, tq), lambda b, p, i: (b, p, i)),
                  pl.BlockSpec((1, MLA_GROUP * LANES, tq),
                               lambda b, p, i: (b, p, jnp.minimum(i + 1, S // tq - 1))),
                  pl.BlockSpec((1, S, MLA_GROUP * LANES), lambda b, p, i: (b, 0, p)),
                  pl.BlockSpec((1, group_rows, S), lambda b, p, i: (b, p, 0)),
                  pl.BlockSpec((1, tq, group_out), lambda b, p, i: (b, i, p))],
        out_specs=pl.BlockSpec((1, tq, group_out), lambda b, p, i: (b, i, p)),
        out_shape=jax.ShapeDtypeStruct((B, S, W), BF16),
        scratch_shapes=[pltpu.VMEM((MLA_GROUP, MLA_V_ROWS, tq), F32),
                        pltpu.VMEM((MLA_GROUP, SUBLANES, tq), F32),
                        pltpu.VMEM((2, MLA_GROUP, MLA_TK, tq), F32),
                        pltpu.VMEM((2, MLA_GROUP, SUBLANES, tq), F32)],
        compiler_params=pltpu.CompilerParams(
            dimension_semantics=("arbitrary", "arbitrary", "arbitrary"),
            vmem_limit_bytes=VMEM_LIMIT),
        name="mla_attn",
    )(q, q, k, v, g)


def _outproj_kernel(msb_ref, mmla_ref, x_ref, mod_ref, w_ref, o_ref):
    y = _dot(msb_ref[0], w_ref[:SB_WIDTH, :]) + _dot(mmla_ref[0], w_ref[SB_WIDTH:, :])
    o_ref[0] = x_ref[0] + mod_ref[0, 2:3, :] * y


def _outproj(mixed_sb, mixed_mla, x, mod, w_out):
    B, S, D = x.shape
    ts = min(OUT_ROW_TILE, S)

    def rows(width):
        return pl.BlockSpec((1, ts, width), lambda b, i: (b, i, 0))

    return pl.pallas_call(
        _outproj_kernel,
        grid=(B, S // ts),
        in_specs=[rows(SB_WIDTH), rows(MLA_WIDTH), rows(D),
                  pl.BlockSpec((1, 3, D), lambda b, i: (b, 0, 0)),
                  pl.BlockSpec(w_out.shape, lambda b, i: (0, 0))],
        out_specs=rows(D),
        out_shape=jax.ShapeDtypeStruct((B, S, D), x.dtype),
        compiler_params=pltpu.CompilerParams(
            dimension_semantics=("arbitrary", "arbitrary"), vmem_limit_bytes=VMEM_LIMIT),
        name="outproj",
    )(mixed_sb, mixed_mla, x, mod, w_out)


def _pad_last(a, width):
    return jnp.pad(a, [(0, 0)] * (a.ndim - 1) + [(0, width - a.shape[-1])])


def _layer(x, c, tables, w_ada, b_ada, norm_w, w_in, q_lora_norm, w_uq, kv_lora_norm, w_ukv,
           q_head_norm, k_head_norm, w_out):
    B, S, D = x.shape
    mod = _adaln(c.T, w_ada, b_ada[None, :]).reshape(B, 3, D)

    ts = min(ROW_TILE, S)
    w_in_b = w_in.astype(BF16)
    w_gm = w_in[:, _C_GMLA:].astype(BF16)
    w_uq_t = _pad_last(w_uq.reshape(Q_LORA_RANK, MLA_HEADS, MLA_QK_DIM), LANES)
    w_uq_t = w_uq_t.reshape(Q_LORA_RANK, MLA_PAD_WIDTH).T.astype(BF16)
    w_ukv_h = w_ukv.reshape(KV_LORA_RANK, MLA_HEADS, MLA_NOPE_DIM + MLA_V_DIM)
    w_uk_t = _pad_last(w_ukv_h[:, :, :MLA_NOPE_DIM], LANES)
    w_uk_t = w_uk_t.reshape(KV_LORA_RANK, MLA_PAD_WIDTH).T.astype(BF16)
    w_uv = w_ukv_h[:, :, MLA_NOPE_DIM:].reshape(KV_LORA_RANK, MLA_WIDTH).T.astype(BF16)
    gq = jnp.broadcast_to(_pad_last(q_head_norm, LANES)[:, None], (LANES, ts))
    gk = jnp.broadcast_to(_pad_last(k_head_norm, LANES)[:, None], (LANES, ts))

    (q_sb, k_sb, v_sb, g_sb, q_m, k_m, v_m, g_m) = _inproj(
        x, mod, norm_w[None, :], w_in_b, w_gm, q_lora_norm[None, :], w_uq_t, kv_lora_norm[None, :],
        w_uk_t, w_uv, gq, gk, *tables)

    mixed_sb = _sb_attention(q_sb, k_sb, v_sb, g_sb)
    mixed_mla = _mla_attention(q_m, k_m, v_m, g_m)
    return _outproj(mixed_sb, mixed_mla, x, mod, w_out.astype(BF16))


def _rope_tables(positions, dtype):
    inv_freq = ROPE_THETA ** (-jnp.arange(0, MLA_ROPE_DIM, 2, dtype=F32) / MLA_ROPE_DIM)
    ang = inv_freq[None, :, None] * positions.astype(F32)[:, None, :]
    return jnp.cos(ang).astype(dtype), jnp.sin(ang).astype(dtype)


def kernel(x, c, positions, w_ada, b_ada, norm_w, w_in, q_lora_norm, w_uq, kv_lora_norm, w_ukv,
           q_head_norm, k_head_norm, w_out):
    tables = _rope_tables(positions, x.dtype)
    for l in range(w_ada.shape[0]):
        x = _layer(x, c, tables, w_ada[l], b_ada[l], norm_w[l], w_in[l], q_lora_norm[l], w_uq[l],
                   kv_lora_norm[l], w_ukv[l], q_head_norm[l], k_head_norm[l], w_out[l])
    return x
```

```python
import math

import jax
import jax.numpy as jnp
from jax import lax
from jax.experimental import pallas as pl
from jax.experimental.pallas import tpu as pltpu

F32 = jnp.float32
BF16 = jnp.bfloat16

SB_HEADS = 8
SB_HEAD_DIM = 64
SB_WIDTH = SB_HEADS * SB_HEAD_DIM
MLA_HEADS = 8
MLA_NOPE_DIM = 64
MLA_ROPE_DIM = 32
MLA_QK_DIM = MLA_NOPE_DIM + MLA_ROPE_DIM
MLA_V_DIM = 64
MLA_WIDTH = MLA_HEADS * MLA_V_DIM
MLA_V_ONES = 16
MLA_V_ROWS = MLA_V_DIM + MLA_V_ONES
Q_LORA_RANK = 384
KV_LORA_RANK = 256
ROPE_THETA = 10000.0
EPS = 1e-6

LANES = 128
SUBLANES = 8
HEAD_PAIR = 2
MLA_PAD_WIDTH = MLA_HEADS * LANES

_C_QSB = 0
_C_KSB = _C_QSB + SB_WIDTH
_C_VSB = _C_KSB + SB_WIDTH
_C_GSB = _C_VSB + SB_WIDTH
_C_CQ = _C_GSB + SB_WIDTH
_C_CKV = _C_CQ + Q_LORA_RANK
_C_KR = _C_CKV + KV_LORA_RANK
_C_GMLA = _C_KR + MLA_ROPE_DIM

ADA_COL_TILE = 512
ROW_TILE = 512
OUT_ROW_TILE = 2048
SB_TQ = 256
SB_TK = 256
SB_PAIRS = 4
SB_SUBS = 2
SB_FIRST_VISIT = 2
SB_STAGE_LAG = (2, 4)
MLA_GROUP = 2
MLA_UNROLL = 4
MLA_TQ = 512
MLA_TK = 256
VMEM_LIMIT = 56 * 1024 * 1024

SB_DEAD_LOG2_DROP = 160.0
SB_Q_SCALE = math.log2(math.e) / math.sqrt(SB_HEAD_DIM)
MLA_Q_SCALE = math.log2(math.e) / math.sqrt(MLA_QK_DIM)


def _silu(g):
    return g * (1.0 / (1.0 + jnp.exp(-g)))


def _dot(a, b):
    return jnp.dot(a, b, preferred_element_type=F32)


def _dot_nt(a, b):
    return lax.dot_general(a, b, (((1,), (1,)), ((), ())), preferred_element_type=F32)


def _dot_tn(a, b):
    return lax.dot_general(a, b, (((0,), (0,)), ((), ())), preferred_element_type=F32)


def _adaln_kernel(ct_ref, w_ref, b_ref, o_ref):
    s_t = _silu(ct_ref[...])
    w = w_ref[...]
    rows = [jnp.sum(w * s_t[:, b:b + 1], axis=0, keepdims=True) for b in range(s_t.shape[1])]
    o_ref[...] = jnp.concatenate(rows, axis=0) + b_ref[...]


def _adaln(c_t, w_ada, b_ada):
    d, rows = c_t.shape
    n = w_ada.shape[1]
    bn = ADA_COL_TILE
    return pl.pallas_call(
        _adaln_kernel,
        grid=(n // bn,),
        in_specs=[pl.BlockSpec((d, rows), lambda j: (0, 0)),
                  pl.BlockSpec((d, bn), lambda j: (0, j)),
                  pl.BlockSpec((1, bn), lambda j: (0, j))],
        out_specs=pl.BlockSpec((rows, bn), lambda j: (0, j)),
        out_shape=jax.ShapeDtypeStruct((rows, n), F32),
        name="adaln",
    )(c_t, w_ada, b_ada)


def _rope_t(x1, x2, cos, sin):
    return x1 * cos - x2 * sin, x2 * cos + x1 * sin


def _inproj_kernel(x_ref, mod_ref, nw_ref, win_ref, wgm_ref, qln_ref, wuqt_ref, kvln_ref, wukt_ref, wuvt_ref,
                   gq_ref, gk_ref, cos_ref, sin_ref,
                   qsb_ref, ksb_ref, vsb_ref, gsb_ref, qmt_ref, km_ref, vmt_ref, gm_ref):
    half = MLA_ROPE_DIM // 2
    n0, n1, n2 = MLA_NOPE_DIM, MLA_NOPE_DIM + half, MLA_QK_DIM
    x = x_ref[0]
    ts = x.shape[0]
    y = x * lax.rsqrt(jnp.mean(x * x, axis=-1, keepdims=True) + EPS) * nw_ref[...]
    h = (y * (1.0 + mod_ref[0, 1:2, :]) + mod_ref[0, 0:1, :]).astype(BF16)

    def proj(lo, hi):
        return _dot(h, win_ref[:, lo:hi])

    def rms(t, w_ref):
        return t * lax.rsqrt(jnp.mean(t * t, axis=-1, keepdims=True) + EPS) * w_ref[...]

    cq = proj(_C_CQ, _C_CKV)
    ckv = proj(_C_CKV, _C_KR)
    kr = proj(_C_KR, _C_KR + LANES)
    qsb_ref[0] = (proj(_C_QSB, _C_KSB) * SB_Q_SCALE).astype(BF16)
    ksb_ref[0] = proj(_C_KSB, _C_VSB).astype(BF16)

    cqn = rms(cq, qln_ref).astype(BF16)
    ckvn = rms(ckv, kvln_ref).astype(BF16)
    q_t = _dot_nt(wuqt_ref[...], cqn)
    k_t = _dot_nt(wukt_ref[...], ckvn)
    v_t = _dot_nt(wuvt_ref[...], ckvn)
    vsb_ref[0] = proj(_C_VSB, _C_GSB).astype(BF16)
    gsb_ref[0] = proj(_C_GSB, _C_CQ)
    gm_ref[0] = _dot(h, wgm_ref[...])

    cos, sin = cos_ref[0], sin_ref[0]
    pad_rows = jnp.zeros((LANES - MLA_QK_DIM, ts), F32)

    gq = gq_ref[...]
    for hd in range(MLA_HEADS):
        blk = q_t[hd * MLA_QK_DIM:(hd + 1) * MLA_QK_DIM]
        inv = lax.rsqrt(jnp.sum(blk * blk, axis=0, keepdims=True) * (1.0 / MLA_QK_DIM) + EPS)
        g = blk * gq
        r1, r2 = _rope_t(g[n0:n1], g[n1:n2], cos, sin)
        out = jnp.concatenate([g[:n0], r1, r2, pad_rows], axis=0) * (inv * MLA_Q_SCALE)
        qmt_ref[0, hd * LANES:(hd + 1) * LANES, :] = out.astype(BF16)

    ones_rows = jnp.ones((MLA_V_ONES, ts), F32)
    v_rows = []
    for hd in range(MLA_HEADS):
        v_rows += [v_t[hd * MLA_V_DIM:(hd + 1) * MLA_V_DIM], ones_rows]
    vmt_ref[0] = jnp.concatenate(v_rows, axis=0).astype(BF16)

    gk = gk_ref[...]
    kr_t = kr.T
    x1, x2 = kr_t[:half], kr_t[half:MLA_ROPE_DIM]
    kr_ssq = jnp.sum(x1 * x1 + x2 * x2, axis=0, keepdims=True)
    r1, r2 = _rope_t(x1 * gk[n0:n1], x2 * gk[n1:n2], cos, sin)
    for hd in range(MLA_HEADS):
        kn = k_t[hd * n0:(hd + 1) * n0]
        ssq = jnp.sum(kn * kn, axis=0, keepdims=True) + kr_ssq
        inv = lax.rsqrt(ssq * (1.0 / MLA_QK_DIM) + EPS)
        out = jnp.concatenate([kn * gk[:n0], r1, r2, pad_rows], axis=0) * inv
        km_ref[0, :, hd * LANES:(hd + 1) * LANES] = out.astype(BF16).T


def _inproj(x, mod, norm_w, w_in_b, w_gm, qln, w_uq_t, kvln, w_uk_t, w_uv, gq, gk, cos_t, sin_t):
    B, S, D = x.shape
    ts = min(ROW_TILE, S)
    grid = (B, S // ts)

    def whole(a):
        return pl.BlockSpec(a.shape, lambda b, i: (0,) * a.ndim)

    def rows(width):
        return pl.BlockSpec((1, ts, width), lambda b, i: (b, i, 0))

    def cols(height):
        return pl.BlockSpec((1, height, ts), lambda b, i: (b, 0, i))

    out_shapes = (
        jax.ShapeDtypeStruct((B, S, SB_WIDTH), BF16),
        jax.ShapeDtypeStruct((B, S, SB_WIDTH), BF16),
        jax.ShapeDtypeStruct((B, S, SB_WIDTH), BF16),
        jax.ShapeDtypeStruct((B, S, SB_WIDTH), F32),
        jax.ShapeDtypeStruct((B, MLA_PAD_WIDTH, S), BF16),
        jax.ShapeDtypeStruct((B, S, MLA_PAD_WIDTH), BF16),
        jax.ShapeDtypeStruct((B, MLA_HEADS * MLA_V_ROWS, S), BF16),
        jax.ShapeDtypeStruct((B, S, MLA_WIDTH), F32),
    )
    out_specs = [rows(s.shape[-1]) for s in out_shapes]
    out_specs[4] = cols(MLA_PAD_WIDTH)
    out_specs[6] = cols(MLA_HEADS * MLA_V_ROWS)
    half = MLA_ROPE_DIM // 2
    return pl.pallas_call(
        _inproj_kernel,
        grid=grid,
        in_specs=[rows(D),
                  pl.BlockSpec((1, 3, D), lambda b, i: (b, 0, 0)),
                  whole(norm_w), whole(w_in_b), whole(w_gm), whole(qln), whole(w_uq_t), whole(kvln),
                  whole(w_uk_t), whole(w_uv), whole(gq), whole(gk),
                  cols(half), cols(half)],
        out_specs=out_specs,
        out_shape=out_shapes,
        compiler_params=pltpu.CompilerParams(
            dimension_semantics=("arbitrary", "arbitrary"), vmem_limit_bytes=VMEM_LIMIT),
        name="inproj",
    )(x, mod, norm_w, w_in_b, w_gm, qln, w_uq_t, kvln, w_uk_t, w_uv, gq, gk, cos_t, sin_t)


def _sb_kernel(q_ref, k_ref, v_ref, g_ref, o_ref, acc_ref, carry_ref):
    tq, tk = SB_TQ, SB_TK
    n_heads = SB_PAIRS * HEAD_PAIR
    i = pl.program_id(2)
    lane = lax.broadcasted_iota(jnp.int32, (tq, LANES), 1)
    q_heads = []
    for sub in range(SB_SUBS):
        heads = []
        for p in range(SB_PAIRS):
            q_pair = q_ref[0, sub * tq:(sub + 1) * tq, p * LANES:(p + 1) * LANES]
            for hh in range(HEAD_PAIR):
                heads.append(jnp.where((lane // SB_HEAD_DIM) == hh, q_pair, jnp.zeros_like(q_pair)))
        q_heads.append(heads)

    strict = (lax.broadcasted_iota(jnp.int32, (tk, tq), 0)
              < lax.broadcasted_iota(jnp.int32, (tk, tq), 1))
    l_row = lax.broadcasted_iota(jnp.int32, (tk + SUBLANES, tk), 0)
    l_col = lax.broadcasted_iota(jnp.int32, (tk + SUBLANES, tk), 1)
    later = jnp.where(jnp.logical_or(l_col >= l_row, l_row >= tk), 1.0, 0.0).astype(F32)

    acc_ref[...] = jnp.zeros_like(acc_ref)
    carry_ref[...] = jnp.zeros_like(carry_ref)

    def visit(plan, diag_first):
        depth = max(len(tiles) for _, tiles in plan)
        items = [(sub, t, tiles[t], h) for t in range(depth) for sub, tiles in plan
                 if t < len(tiles) for h in range(n_heads)]

        def pair_block(ref, j, h):
            p = h // HEAD_PAIR
            return ref[0, pl.ds(pl.multiple_of(j * tk, tk), tk), p * LANES:(p + 1) * LANES]

        n = len(items)
        zs, afters, col_sums = [None] * n, [None] * n, [None] * n
        carries = {(sub, h): carry_ref[sub, h] for sub, _ in plan for h in range(n_heads)}
        outs = {key: None for key in carries}

        def scores(idx):
            sub, _, j, h = items[idx]
            zs[idx] = _dot_nt(pair_block(k_ref, j, h), q_heads[sub][h])

        def drops(idx):
            _, t, _, _ = items[idx]
            z = zs[idx]
            drop = jnp.maximum(z, 0.0) + jnp.log2(1.0 + jnp.exp2(-jnp.abs(z)))
            if diag_first and t == 0:
                drop = jnp.where(strict, drop, 0.0)
            hi = drop.astype(BF16).astype(F32)
            sums = _dot(later, hi) + _dot(later, drop - hi)
            afters[idx] = sums[:tk]
            col_sums[idx] = sums[tk:tk + 1]

        def weights(idx):
            sub, t, j, h = items[idx]
            w = jnp.exp2(zs[idx] - (afters[idx] + carries[sub, h]))
            if diag_first and t == 0:
                w = jnp.where(strict, w, 0.0)
            pv = _dot_tn(pair_block(v_ref, j, h), w.astype(BF16))
            lo_row = (h % HEAD_PAIR) * SB_HEAD_DIM
            pv = pv[lo_row:lo_row + SB_HEAD_DIM, :]
            outs[sub, h] = pv if outs[sub, h] is None else outs[sub, h] + pv
            carries[sub, h] = carries[sub, h] + col_sums[idx]

        for step in range(n + SB_STAGE_LAG[1]):
            if step < n:
                scores(step)
            if 0 <= step - SB_STAGE_LAG[0] < n:
                drops(step - SB_STAGE_LAG[0])
            if 0 <= step - SB_STAGE_LAG[1] < n:
                weights(step - SB_STAGE_LAG[1])
        alive = []
        for sub, _ in plan:
            dead = None
            for h in range(n_heads):
                acc_ref[sub, h] += outs[sub, h]
                carry_ref[sub, h] = carries[sub, h]
                dead = carries[sub, h] if dead is None else jnp.minimum(dead, carries[sub, h])
            alive.append(jnp.min(dead) < SB_DEAD_LOG2_DROP)
        return alive

    first = SB_SUBS * i

    def usual_step():
        return visit([(sub, [first + sub - t for t in range(SB_FIRST_VISIT)])
                      for sub in range(SB_SUBS)], True)

    def first_step():
        return visit([(sub, [sub - t for t in range(SB_FIRST_VISIT) if sub - t >= 0])
                      for sub in range(SB_SUBS)], True)

    assert SB_SUBS >= SB_FIRST_VISIT - 1
    alive = lax.cond(i > 0, usual_step, first_step)

    for sub in range(SB_SUBS):
        def cond(state):
            j, still_alive = state
            return jnp.logical_and(j >= 0, still_alive)

        def body(state, sub=sub):
            j, _ = state
            return j - 1, visit([(sub, [j])], False)[0]

        lax.while_loop(cond, body, (first + sub - SB_FIRST_VISIT, alive[sub]))

    for sub in range(SB_SUBS):
        rows = slice(sub * tq, (sub + 1) * tq)
        o_t = acc_ref[sub].reshape(n_heads * SB_HEAD_DIM, tq)
        o_ref[0, rows, :] = (o_t.T * _silu(g_ref[0, rows, :])).astype(o_ref.dtype)


def _sb_attention(q, k, v, g):
    B, S, W = q.shape
    rows = SB_SUBS * SB_TQ
    assert S % rows == 0 and SB_TQ == SB_TK
    wb = SB_PAIRS * LANES
    n_heads = SB_PAIRS * HEAD_PAIR
    grid = (B, W // wb, S // rows)
    q_spec = pl.BlockSpec((1, rows, wb), lambda b, p, i: (b, i, p))
    kv_spec = pl.BlockSpec((1, S, wb), lambda b, p, i: (b, 0, p))
    return pl.pallas_call(
        _sb_kernel,
        grid=grid,
        in_specs=[q_spec, kv_spec, kv_spec, q_spec],
        out_specs=q_spec,
        out_shape=jax.ShapeDtypeStruct((B, S, W), BF16),
        scratch_shapes=[pltpu.VMEM((SB_SUBS, n_heads, SB_HEAD_DIM, SB_TQ), F32),
                        pltpu.VMEM((SB_SUBS, n_heads, 1, SB_TQ), F32)],
        compiler_params=pltpu.CompilerParams(
            dimension_semantics=("arbitrary", "arbitrary", "arbitrary"),
            vmem_limit_bytes=VMEM_LIMIT),
        name="sb_attn",
    )(q, k, v, g)


def _mla_kernel(q_ref, qnext_ref, k_ref, v_ref, g_ref, o_ref, acc_ref, m_ref, s_ref, smax_ref):
    tq, tk = MLA_TQ, MLA_TK
    diag_tiles = tq // tk
    i = pl.program_id(2)
    neg = jnp.finfo(F32).min

    acc_ref[...] = jnp.zeros_like(acc_ref)
    m_ref[...] = jnp.full_like(m_ref, neg)

    def score(j, slot, diag=None, queries=q_ref):
        start = pl.multiple_of(j * tk, tk)
        lo = 0 if diag is None else diag * tk
        for hh in range(MLA_GROUP):
            q_t = queries[0, hh * LANES:(hh + 1) * LANES, lo:]
            k_blk = k_ref[0, pl.ds(start, tk), hh * LANES:(hh + 1) * LANES]
            s = _dot(k_blk, q_t)
            s_ref[slot, hh, :, lo:] = s
            if diag is None:
                smax_ref[slot, hh] = jnp.broadcast_to(jnp.max(s, axis=0, keepdims=True), (SUBLANES, tq))

    def consume(j, slot, diag=None):
        start = pl.multiple_of(j * tk, tk)
        lo = 0 if diag is None else diag * tk
        for hh in range(MLA_GROUP):
            v_t = v_ref[0, hh * MLA_V_ROWS:(hh + 1) * MLA_V_ROWS, pl.ds(start, tk)]
            s = s_ref[slot, hh, :, lo:]
            if diag is None:
                tile_max = smax_ref[slot, hh]
            else:
                causal = (lax.broadcasted_iota(jnp.int32, s.shape, 0)
                          <= lax.broadcasted_iota(jnp.int32, s.shape, 1))
                s = jnp.where(causal, s, neg)
                tile_max = jnp.max(s, axis=0, keepdims=True)
            m_old = m_ref[hh, :, lo:]
            m_new = jnp.maximum(m_old, tile_max)
            alpha = jnp.exp2(m_old[:1] - m_new[:1])
            p = jnp.exp2(s - m_new[:1])
            acc_ref[hh, :, lo:] = alpha * acc_ref[hh, :, lo:] + _dot(v_t, p.astype(BF16))
            m_ref[hh, :, lo:] = m_new

    base = i * diag_tiles

    def two_tiles(t):
        score(t + 1, 1)
        consume(t, 0)
        score(t + 2, 0)
        consume(t + 1, 1)

    looped = jnp.maximum(base - 2, 0)
    done = 0
    pairs = MLA_UNROLL
    while pairs >= 1:
        def group(jj, carry, pairs=pairs, done=done):
            for r in range(pairs):
                two_tiles(done + 2 * pairs * jj + 2 * r)
            return carry

        n_groups = (looped - done) // (2 * pairs)
        lax.fori_loop(0, n_groups, group, 0)
        done = done + n_groups * 2 * pairs
        pairs //= 2

    def diagonal_block():
        for d in range(diag_tiles):
            if d + 1 < diag_tiles:
                score(base + d + 1, (d + 1) % 2, d + 1)
            consume(base + d, d % 2, d)
            if d == 0:
                score(0, 0, queries=qnext_ref)
        o_t = jnp.concatenate(
            [acc_ref[hh, :MLA_V_DIM] / acc_ref[hh, MLA_V_DIM:MLA_V_DIM + 1] for hh in range(MLA_GROUP)],
            axis=0)
        o_ref[0] = (o_t.T * _silu(g_ref[0])).astype(o_ref.dtype)

    @pl.when(i > 0)
    def _():
        two_tiles(base - 2)
        diagonal_block()

    @pl.when(i == 0)
    def _():
        score(0, 0)
        diagonal_block()


def _mla_attention(q, k, v, g):
    B, S, W = g.shape
    tq = MLA_TQ
    group_rows = MLA_GROUP * MLA_V_ROWS
    group_out = MLA_GROUP * MLA_V_DIM
    assert S % tq == 0 and MLA_TQ % (2 * MLA_TK) == 0 and group_out % LANES == 0
    grid = (B, W // group_out, S // tq)
    return pl.pallas_call(
        _mla_kernel,
        grid=grid,
        in_specs=[pl.BlockSpec((1, MLA_GROUP * LANES, tq), lambda b, p, i: (b, p, i)),
                  pl.BlockSpec((1, MLA_GROUP * LANES, tq),
                               lambda b, p, i: (b, p, jnp.minimum(i + 1, S // tq - 1))),
                  pl.BlockSpec((1, S, MLA_GROUP * LANES), lambda b, p, i: (b, 0, p)),
                  pl.BlockSpec((1, group_rows, S), lambda b, p, i: (b, p, 0)),
                  pl.BlockSpec((1, tq, group_out), lambda b, p, i: (b, i, p))],
        out_specs=pl.BlockSpec((1, tq, group_out), lambda b, p, i: (b, i, p)),
        out_shape=jax.ShapeDtypeStruct((B, S, W), BF16),
        scratch_shapes=[pltpu.VMEM((MLA_GROUP, MLA_V_ROWS, tq), F32),
                        pltpu.VMEM((MLA_GROUP, SUBLANES, tq), F32),
                        pltpu.VMEM((2, MLA_GROUP, MLA_TK, tq), F32),
                        pltpu.VMEM((2, MLA_GROUP, SUBLANES, tq), F32)],
        compiler_params=pltpu.CompilerParams(
            dimension_semantics=("arbitrary", "arbitrary", "arbitrary"),
            vmem_limit_bytes=VMEM_LIMIT),
        name="mla_attn",
    )(q, q, k, v, g)


def _outproj_kernel(msb_ref, mmla_ref, x_ref, mod_ref, w_ref, o_ref):
    y = _dot(msb_ref[0], w_ref[:SB_WIDTH, :]) + _dot(mmla_ref[0], w_ref[SB_WIDTH:, :])
    o_ref[0] = x_ref[0] + mod_ref[0, 2:3, :] * y


def _outproj(mixed_sb, mixed_mla, x, mod, w_out):
    B, S, D = x.shape
    ts = min(OUT_ROW_TILE, S)

    def rows(width):
        return pl.BlockSpec((1, ts, width), lambda b, i: (b, i, 0))

    return pl.pallas_call(
        _outproj_kernel,
        grid=(B, S // ts),
        in_specs=[rows(SB_WIDTH), rows(MLA_WIDTH), rows(D),
                  pl.BlockSpec((1, 3, D), lambda b, i: (b, 0, 0)),
                  pl.BlockSpec(w_out.shape, lambda b, i: (0, 0))],
        out_specs=rows(D),
        out_shape=jax.ShapeDtypeStruct((B, S, D), x.dtype),
        compiler_params=pltpu.CompilerParams(
            dimension_semantics=("arbitrary", "arbitrary"), vmem_limit_bytes=VMEM_LIMIT),
        name="outproj",
    )(mixed_sb, mixed_mla, x, mod, w_out)


def _layer(x, c, tables, w_ada, b_ada, norm_w, w_in, q_lora_norm, w_uq, kv_lora_norm, w_ukv,
           q_head_norm, k_head_norm, w_out):
    B, S, D = x.shape
    mod = _adaln(c.T, w_ada, b_ada[None, :]).reshape(B, 3, D)

    ts = min(ROW_TILE, S)
    w_in_b = w_in.astype(BF16)
    w_gm = w_in[:, _C_GMLA:].astype(BF16)
    w_uq_t = w_uq.T.astype(BF16)
    w_ukv_h = w_ukv.reshape(KV_LORA_RANK, MLA_HEADS, MLA_NOPE_DIM + MLA_V_DIM)
    w_uk_t = w_ukv_h[:, :, :MLA_NOPE_DIM].reshape(KV_LORA_RANK, -1).T.astype(BF16)
    w_uv = w_ukv_h[:, :, MLA_NOPE_DIM:].reshape(KV_LORA_RANK, MLA_WIDTH).T.astype(BF16)
    gq = jnp.broadcast_to(q_head_norm[:, None], (MLA_QK_DIM, ts))
    gk = jnp.broadcast_to(k_head_norm[:, None], (MLA_QK_DIM, ts))

    (q_sb, k_sb, v_sb, g_sb, q_m, k_m, v_m, g_m) = _inproj(
        x, mod, norm_w[None, :], w_in_b, w_gm, q_lora_norm[None, :], w_uq_t, kv_lora_norm[None, :],
        w_uk_t, w_uv, gq, gk, *tables)

    mixed_sb = _sb_attention(q_sb, k_sb, v_sb, g_sb)
    mixed_mla = _mla_attention(q_m, k_m, v_m, g_m)
    return _outproj(mixed_sb, mixed_mla, x, mod, w_out.astype(BF16))


def _rope_tables(positions, dtype):
    inv_freq = ROPE_THETA ** (-jnp.arange(0, MLA_ROPE_DIM, 2, dtype=F32) / MLA_ROPE_DIM)
    ang = inv_freq[None, :, None] * positions.astype(F32)[:, None, :]
    return jnp.cos(ang).astype(dtype), jnp.sin(ang).astype(dtype)


def kernel(x, c, positions, w_ada, b_ada, norm_w, w_in, q_lora_norm, w_uq, kv_lora_norm, w_ukv,
           q_head_norm, k_head_norm, w_out):
    tables = _rope_tables(positions, x.dtype)
    for l in range(w_ada.shape[0]):
        x = _layer(x, c, tables, w_ada[l], b_ada[l], norm_w[l], w_in[l], q_lora_norm[l], w_uq[l],
                   kv_lora_norm[l], w_ukv[l], q_head_norm[l], k_head_norm[l], w_out[l])
    return x
```

```python
import math

import jax
import jax.numpy as jnp
from jax import lax
from jax.experimental import pallas as pl
from jax.experimental.pallas import tpu as pltpu

F32 = jnp.float32
BF16 = jnp.bfloat16

SB_HEADS = 8
SB_HEAD_DIM = 64
SB_WIDTH = SB_HEADS * SB_HEAD_DIM
MLA_HEADS = 8
MLA_NOPE_DIM = 64
MLA_ROPE_DIM = 32
MLA_QK_DIM = MLA_NOPE_DIM + MLA_ROPE_DIM
MLA_V_DIM = 64
MLA_WIDTH = MLA_HEADS * MLA_V_DIM
MLA_V_ONES = 16
MLA_V_ROWS = MLA_V_DIM + MLA_V_ONES
Q_LORA_RANK = 384
KV_LORA_RANK = 256
ROPE_THETA = 10000.0
EPS = 1e-6

LANES = 128
SUBLANES = 8
HEAD_PAIR = 2
MLA_PAD_WIDTH = MLA_HEADS * LANES

_C_QSB = 0
_C_KSB = _C_QSB + SB_WIDTH
_C_VSB = _C_KSB + SB_WIDTH
_C_GSB = _C_VSB + SB_WIDTH
_C_CQ = _C_GSB + SB_WIDTH
_C_CKV = _C_CQ + Q_LORA_RANK
_C_KR = _C_CKV + KV_LORA_RANK
_C_GMLA = _C_KR + MLA_ROPE_DIM

ADA_COL_TILE = 512
ROW_TILE = 512
OUT_ROW_TILE = 2048
SB_TQ = 256
SB_TK = 256
SB_PAIRS = 4
SB_SUBS = 2
SB_FIRST_VISIT = 2
SB_STAGE_LAG = (2, 4)
MLA_GROUP = 2
MLA_UNROLL = 4
MLA_TQ = 512
MLA_TK = 256
VMEM_LIMIT = 56 * 1024 * 1024

SB_DEAD_LOG2_DROP = 160.0
SB_Q_SCALE = math.log2(math.e) / math.sqrt(SB_HEAD_DIM)
MLA_Q_SCALE = math.log2(math.e) / math.sqrt(MLA_QK_DIM)


def _silu(g):
    return g * (1.0 / (1.0 + jnp.exp(-g)))


def _dot(a, b):
    return jnp.dot(a, b, preferred_element_type=F32)


def _dot_nt(a, b):
    return lax.dot_general(a, b, (((1,), (1,)), ((), ())), preferred_element_type=F32)


def _dot_tn(a, b):
    return lax.dot_general(a, b, (((0,), (0,)), ((), ())), preferred_element_type=F32)


def _adaln_kernel(ct_ref, w_ref, b_ref, o_ref):
    s_t = _silu(ct_ref[...])
    w = w_ref[...]
    rows = [jnp.sum(w * s_t[:, b:b + 1], axis=0, keepdims=True) for b in range(s_t.shape[1])]
    o_ref[...] = jnp.concatenate(rows, axis=0) + b_ref[...]


def _adaln(c_t, w_ada, b_ada):
    d, rows = c_t.shape
    n = w_ada.shape[1]
    bn = ADA_COL_TILE
    return pl.pallas_call(
        _adaln_kernel,
        grid=(n // bn,),
        in_specs=[pl.BlockSpec((d, rows), lambda j: (0, 0)),
                  pl.BlockSpec((d, bn), lambda j: (0, j)),
                  pl.BlockSpec((1, bn), lambda j: (0, j))],
        out_specs=pl.BlockSpec((rows, bn), lambda j: (0, j)),
        out_shape=jax.ShapeDtypeStruct((rows, n), F32),
        name="adaln",
    )(c_t, w_ada, b_ada)


def _rope_t(x1, x2, cos, sin):
    return x1 * cos - x2 * sin, x2 * cos + x1 * sin


def _inproj_kernel(x_ref, mod_ref, nw_ref, win_ref, wgm_ref, qln_ref, wuqt_ref, kvln_ref, wukt_ref, wuvt_ref,
                   gq_ref, gk_ref, cos_ref, sin_ref,
                   qsb_ref, ksb_ref, vsb_ref, gsb_ref, qmt_ref, km_ref, vmt_ref, gm_ref):
    half = MLA_ROPE_DIM // 2
    n0, n1, n2 = MLA_NOPE_DIM, MLA_NOPE_DIM + half, MLA_QK_DIM
    x = x_ref[0]
    ts = x.shape[0]
    y = x * lax.rsqrt(jnp.mean(x * x, axis=-1, keepdims=True) + EPS) * nw_ref[...]
    h = (y * (1.0 + mod_ref[0, 1:2, :]) + mod_ref[0, 0:1, :]).astype(BF16)

    def proj(lo, hi):
        return _dot(h, win_ref[:, lo:hi])

    def rms(t, w_ref):
        return t * lax.rsqrt(jnp.mean(t * t, axis=-1, keepdims=True) + EPS) * w_ref[...]

    latents = proj(_C_CQ, _C_KR + LANES)
    cq = latents[:, :Q_LORA_RANK]
    ckv = latents[:, Q_LORA_RANK:Q_LORA_RANK + KV_LORA_RANK]
    kr = latents[:, Q_LORA_RANK + KV_LORA_RANK:]
    qsb_ref[0] = (proj(_C_QSB, _C_KSB) * SB_Q_SCALE).astype(BF16)
    ksb_ref[0] = proj(_C_KSB, _C_VSB).astype(BF16)

    cqn = rms(cq, qln_ref).astype(BF16)
    ckvn = rms(ckv, kvln_ref).astype(BF16)
    q_t = _dot_nt(wuqt_ref[...], cqn)
    k_t = _dot_nt(wukt_ref[...], ckvn)
    v_t = _dot_nt(wuvt_ref[...], ckvn)
    vsb_ref[0] = proj(_C_VSB, _C_GSB).astype(BF16)
    gsb_ref[0] = proj(_C_GSB, _C_CQ)
    gm_ref[0] = _dot(h, wgm_ref[...])

    cos, sin = cos_ref[0], sin_ref[0]
    pad_rows = jnp.zeros((LANES - MLA_QK_DIM, ts), F32)

    gq = gq_ref[...]
    for hd in range(MLA_HEADS):
        blk = q_t[hd * MLA_QK_DIM:(hd + 1) * MLA_QK_DIM]
        inv = lax.rsqrt(jnp.sum(blk * blk, axis=0, keepdims=True) * (1.0 / MLA_QK_DIM) + EPS)
        g = blk * gq
        r1, r2 = _rope_t(g[n0:n1], g[n1:n2], cos, sin)
        out = jnp.concatenate([g[:n0], r1, r2, pad_rows], axis=0) * (inv * MLA_Q_SCALE)
        qmt_ref[0, hd * LANES:(hd + 1) * LANES, :] = out.astype(BF16)

    ones_rows = jnp.ones((MLA_V_ONES, ts), F32)
    v_rows = []
    for hd in range(MLA_HEADS):
        v_rows += [v_t[hd * MLA_V_DIM:(hd + 1) * MLA_V_DIM], ones_rows]
    vmt_ref[0] = jnp.concatenate(v_rows, axis=0).astype(BF16)

    gk = gk_ref[...]
    kr_t = kr.T
    x1, x2 = kr_t[:half], kr_t[half:MLA_ROPE_DIM]
    kr_ssq = jnp.sum(x1 * x1 + x2 * x2, axis=0, keepdims=True)
    r1, r2 = _rope_t(x1 * gk[n0:n1], x2 * gk[n1:n2], cos, sin)
    for hd in range(MLA_HEADS):
        kn = k_t[hd * n0:(hd + 1) * n0]
        ssq = jnp.sum(kn * kn, axis=0, keepdims=True) + kr_ssq
        inv = lax.rsqrt(ssq * (1.0 / MLA_QK_DIM) + EPS)
        out = jnp.concatenate([kn * gk[:n0], r1, r2, pad_rows], axis=0) * inv
        km_ref[0, :, hd * LANES:(hd + 1) * LANES] = out.astype(BF16).T


def _inproj(x, mod, norm_w, w_in_b, w_gm, qln, w_uq_t, kvln, w_uk_t, w_uv, gq, gk, cos_t, sin_t):
    B, S, D = x.shape
    ts = min(ROW_TILE, S)
    grid = (B, S // ts)

    def whole(a):
        return pl.BlockSpec(a.shape, lambda b, i: (0,) * a.ndim)

    def rows(width):
        return pl.BlockSpec((1, ts, width), lambda b, i: (b, i, 0))

    def cols(height):
        return pl.BlockSpec((1, height, ts), lambda b, i: (b, 0, i))

    out_shapes = (
        jax.ShapeDtypeStruct((B, S, SB_WIDTH), BF16),
        jax.ShapeDtypeStruct((B, S, SB_WIDTH), BF16),
        jax.ShapeDtypeStruct((B, S, SB_WIDTH), BF16),
        jax.ShapeDtypeStruct((B, S, SB_WIDTH), F32),
        jax.ShapeDtypeStruct((B, MLA_PAD_WIDTH, S), BF16),
        jax.ShapeDtypeStruct((B, S, MLA_PAD_WIDTH), BF16),
        jax.ShapeDtypeStruct((B, MLA_HEADS * MLA_V_ROWS, S), BF16),
        jax.ShapeDtypeStruct((B, S, MLA_WIDTH), F32),
    )
    out_specs = [rows(s.shape[-1]) for s in out_shapes]
    out_specs[4] = cols(MLA_PAD_WIDTH)
    out_specs[6] = cols(MLA_HEADS * MLA_V_ROWS)
    half = MLA_ROPE_DIM // 2
    return pl.pallas_call(
        _inproj_kernel,
        grid=grid,
        in_specs=[rows(D),
                  pl.BlockSpec((1, 3, D), lambda b, i: (b, 0, 0)),
                  whole(norm_w), whole(w_in_b), whole(w_gm), whole(qln), whole(w_uq_t), whole(kvln),
                  whole(w_uk_t), whole(w_uv), whole(gq), whole(gk),
                  cols(half), cols(half)],
        out_specs=out_specs,
        out_shape=out_shapes,
        compiler_params=pltpu.CompilerParams(
            dimension_semantics=("arbitrary", "arbitrary"), vmem_limit_bytes=VMEM_LIMIT),
        name="inproj",
    )(x, mod, norm_w, w_in_b, w_gm, qln, w_uq_t, kvln, w_uk_t, w_uv, gq, gk, cos_t, sin_t)


def _sb_kernel(q_ref, k_ref, v_ref, g_ref, o_ref, acc_ref, carry_ref):
    tq, tk = SB_TQ, SB_TK
    n_heads = SB_PAIRS * HEAD_PAIR
    i = pl.program_id(2)
    lane = lax.broadcasted_iota(jnp.int32, (tq, LANES), 1)
    q_heads = []
    for sub in range(SB_SUBS):
        heads = []
        for p in range(SB_PAIRS):
            q_pair = q_ref[0, sub * tq:(sub + 1) * tq, p * LANES:(p + 1) * LANES]
            for hh in range(HEAD_PAIR):
                heads.append(jnp.where((lane // SB_HEAD_DIM) == hh, q_pair, jnp.zeros_like(q_pair)))
        q_heads.append(heads)

    strict = (lax.broadcasted_iota(jnp.int32, (tk, tq), 0)
              < lax.broadcasted_iota(jnp.int32, (tk, tq), 1))
    l_row = lax.broadcasted_iota(jnp.int32, (tk + SUBLANES, tk), 0)
    l_col = lax.broadcasted_iota(jnp.int32, (tk + SUBLANES, tk), 1)
    later = jnp.where(jnp.logical_or(l_col >= l_row, l_row >= tk), 1.0, 0.0).astype(F32)

    acc_ref[...] = jnp.zeros_like(acc_ref)
    carry_ref[...] = jnp.zeros_like(carry_ref)

    def visit(plan, diag_first):
        depth = max(len(tiles) for _, tiles in plan)
        items = [(sub, t, tiles[t], h) for t in range(depth) for sub, tiles in plan
                 if t < len(tiles) for h in range(n_heads)]

        def pair_block(ref, j, h):
            p = h // HEAD_PAIR
            return ref[0, pl.ds(pl.multiple_of(j * tk, tk), tk), p * LANES:(p + 1) * LANES]

        n = len(items)
        zs, afters, col_sums = [None] * n, [None] * n, [None] * n
        carries = {(sub, h): carry_ref[sub, h] for sub, _ in plan for h in range(n_heads)}
        outs = {key: None for key in carries}

        def scores(idx):
            sub, _, j, h = items[idx]
            zs[idx] = _dot_nt(pair_block(k_ref, j, h), q_heads[sub][h])

        def drops(idx):
            _, t, _, _ = items[idx]
            z = zs[idx]
            drop = jnp.maximum(z, 0.0) + jnp.log2(1.0 + jnp.exp2(-jnp.abs(z)))
            if diag_first and t == 0:
                drop = jnp.where(strict, drop, 0.0)
            hi = drop.astype(BF16).astype(F32)
            sums = _dot(later, hi) + _dot(later, drop - hi)
            afters[idx] = sums[:tk]
            col_sums[idx] = sums[tk:tk + 1]

        def weights(idx):
            sub, t, j, h = items[idx]
            w = jnp.exp2(zs[idx] - (afters[idx] + carries[sub, h]))
            if diag_first and t == 0:
                w = jnp.where(strict, w, 0.0)
            pv = _dot_tn(pair_block(v_ref, j, h), w.astype(BF16))
            lo_row = (h % HEAD_PAIR) * SB_HEAD_DIM
            pv = pv[lo_row:lo_row + SB_HEAD_DIM, :]
            outs[sub, h] = pv if outs[sub, h] is None else outs[sub, h] + pv
            carries[sub, h] = carries[sub, h] + col_sums[idx]

        for step in range(n + SB_STAGE_LAG[1]):
            if step < n:
                scores(step)
            if 0 <= step - SB_STAGE_LAG[0] < n:
                drops(step - SB_STAGE_LAG[0])
            if 0 <= step - SB_STAGE_LAG[1] < n:
                weights(step - SB_STAGE_LAG[1])
        alive = []
        for sub, _ in plan:
            dead = None
            for h in range(n_heads):
                acc_ref[sub, h] += outs[sub, h]
                carry_ref[sub, h] = carries[sub, h]
                dead = carries[sub, h] if dead is None else jnp.minimum(dead, carries[sub, h])
            alive.append(jnp.min(dead) < SB_DEAD_LOG2_DROP)
        return alive

    first = SB_SUBS * i

    def usual_step():
        return visit([(sub, [first + sub - t for t in range(SB_FIRST_VISIT)])
                      for sub in range(SB_SUBS)], True)

    def first_step():
        return visit([(sub, [sub - t for t in range(SB_FIRST_VISIT) if sub - t >= 0])
                      for sub in range(SB_SUBS)], True)

    assert SB_SUBS >= SB_FIRST_VISIT - 1
    alive = lax.cond(i > 0, usual_step, first_step)

    for sub in range(SB_SUBS):
        def cond(state):
            j, still_alive = state
            return jnp.logical_and(j >= 0, still_alive)

        def body(state, sub=sub):
            j, _ = state
            return j - 1, visit([(sub, [j])], False)[0]

        lax.while_loop(cond, body, (first + sub - SB_FIRST_VISIT, alive[sub]))

    for sub in range(SB_SUBS):
        rows = slice(sub * tq, (sub + 1) * tq)
        o_t = acc_ref[sub].reshape(n_heads * SB_HEAD_DIM, tq)
        o_ref[0, rows, :] = (o_t.T * _silu(g_ref[0, rows, :])).astype(o_ref.dtype)


def _sb_attention(q, k, v, g):
    B, S, W = q.shape
    rows = SB_SUBS * SB_TQ
    assert S % rows == 0 and SB_TQ == SB_TK
    wb = SB_PAIRS * LANES
    n_heads = SB_PAIRS * HEAD_PAIR
    grid = (B, W // wb, S // rows)
    q_spec = pl.BlockSpec((1, rows, wb), lambda b, p, i: (b, i, p))
    kv_spec = pl.BlockSpec((1, S, wb), lambda b, p, i: (b, 0, p))
    return pl.pallas_call(
        _sb_kernel,
        grid=grid,
        in_specs=[q_spec, kv_spec, kv_spec, q_spec],
        out_specs=q_spec,
        out_shape=jax.ShapeDtypeStruct((B, S, W), BF16),
        scratch_shapes=[pltpu.VMEM((SB_SUBS, n_heads, SB_HEAD_DIM, SB_TQ), F32),
                        pltpu.VMEM((SB_SUBS, n_heads, 1, SB_TQ), F32)],
        compiler_params=pltpu.CompilerParams(
            dimension_semantics=("arbitrary", "arbitrary", "arbitrary"),
            vmem_limit_bytes=VMEM_LIMIT),
        name="sb_attn",
    )(q, k, v, g)


def _mla_kernel(q_ref, qnext_ref, k_ref, v_ref, g_ref, o_ref, acc_ref, m_ref, s_ref, smax_ref):
    tq, tk = MLA_TQ, MLA_TK
    diag_tiles = tq // tk
    i = pl.program_id(2)
    neg = jnp.finfo(F32).min

    acc_ref[...] = jnp.zeros_like(acc_ref)
    m_ref[...] = jnp.full_like(m_ref, neg)

    def score(j, slot, diag=None, queries=q_ref):
        start = pl.multiple_of(j * tk, tk)
        lo = 0 if diag is None else diag * tk
        for hh in range(MLA_GROUP):
            q_t = queries[0, hh * LANES:(hh + 1) * LANES, lo:]
            k_blk = k_ref[0, pl.ds(start, tk), hh * LANES:(hh + 1) * LANES]
            s = _dot(k_blk, q_t)
            s_ref[slot, hh, :, lo:] = s
            if diag is None:
                smax_ref[slot, hh] = jnp.broadcast_to(jnp.max(s, axis=0, keepdims=True), (SUBLANES, tq))

    def consume(j, slot, diag=None):
        start = pl.multiple_of(j * tk, tk)
        lo = 0 if diag is None else diag * tk
        for hh in range(MLA_GROUP):
            v_t = v_ref[0, hh * MLA_V_ROWS:(hh + 1) * MLA_V_ROWS, pl.ds(start, tk)]
            s = s_ref[slot, hh, :, lo:]
            if diag is None:
                tile_max = smax_ref[slot, hh]
            else:
                causal = (lax.broadcasted_iota(jnp.int32, s.shape, 0)
                          <= lax.broadcasted_iota(jnp.int32, s.shape, 1))
                s = jnp.where(causal, s, neg)
                tile_max = jnp.max(s, axis=0, keepdims=True)
            m_old = m_ref[hh, :, lo:]
            m_new = jnp.maximum(m_old, tile_max)
            alpha = jnp.exp2(m_old[:1] - m_new[:1])
            p = jnp.exp2(s - m_new[:1])
            acc_ref[hh, :, lo:] = alpha * acc_ref[hh, :, lo:] + _dot(v_t, p.astype(BF16))
            m_ref[hh, :, lo:] = m_new

    base = i * diag_tiles

    def two_tiles(t):
        score(t + 1, 1)
        consume(t, 0)
        score(t + 2, 0)
        consume(t + 1, 1)

    looped = jnp.maximum(base - 2, 0)
    done = 0
    pairs = MLA_UNROLL
    while pairs >= 1:
        def group(jj, carry, pairs=pairs, done=done):
            for r in range(pairs):
                two_tiles(done + 2 * pairs * jj + 2 * r)
            return carry

        n_groups = (looped - done) // (2 * pairs)
        lax.fori_loop(0, n_groups, group, 0)
        done = done + n_groups * 2 * pairs
        pairs //= 2

    def diagonal_block():
        for d in range(diag_tiles):
            if d + 1 < diag_tiles:
                score(base + d + 1, (d + 1) % 2, d + 1)
            consume(base + d, d % 2, d)
            if d == 0:
                score(0, 0, queries=qnext_ref)
        o_t = jnp.concatenate(
            [acc_ref[hh, :MLA_V_DIM] / acc_ref[hh, MLA_V_DIM:MLA_V_DIM + 1] for hh in range(MLA_GROUP)],
            axis=0)
        o_ref[0] = (o_t.T * _silu(g_ref[0])).astype(o_ref.dtype)

    @pl.when(i > 0)
    def _():
        two_tiles(base - 2)
        diagonal_block()

    @pl.when(i == 0)
    def _():
        score(0, 0)
        diagonal_block()


def _mla_attention(q, k, v, g):
    B, S, W = g.shape
    tq = MLA_TQ
    group_rows = MLA_GROUP * MLA_V_ROWS
    group_out = MLA_GROUP * MLA_V_DIM
    assert S % tq == 0 and MLA_TQ % (2 * MLA_TK) == 0 and group_out % LANES == 0
    grid = (B, W // group_out, S // tq)
    return pl.pallas_call(
        _mla_kernel,
        grid=grid,
        in_specs=[pl.BlockSpec((1, MLA_GROUP * LANES, tq), lambda b, p, i: (b, p, i)),
                  pl.BlockSpec((1, MLA_GROUP * LANES, tq),
                               lambda b, p, i: (b, p, jnp.minimum(i + 1, S // tq - 1))),
                  pl.BlockSpec((1, S, MLA_GROUP * LANES), lambda b, p, i: (b, 0, p)),
                  pl.BlockSpec((1, group_rows, S), lambda b, p, i: (b, p, 0)),
                  pl.BlockSpec((1, tq, group_out), lambda b, p, i: (b, i, p))],
        out_specs=pl.BlockSpec((1, tq, group_out), lambda b, p, i: (b, i, p)),
        out_shape=jax.ShapeDtypeStruct((B, S, W), BF16),
        scratch_shapes=[pltpu.VMEM((MLA_GROUP, MLA_V_ROWS, tq), F32),
                        pltpu.VMEM((MLA_GROUP, SUBLANES, tq), F32),
                        pltpu.VMEM((2, MLA_GROUP, MLA_TK, tq), F32),
                        pltpu.VMEM((2, MLA_GROUP, SUBLANES, tq), F32)],
        compiler_params=pltpu.CompilerParams(
            dimension_semantics=("arbitrary", "arbitrary", "arbitrary"),
            vmem_limit_bytes=VMEM_LIMIT),
        name="mla_attn",
    )(q, q, k, v, g)


def _outproj_kernel(msb_ref, mmla_ref, x_ref, mod_ref, w_ref, o_ref):
    y = _dot(msb_ref[0], w_ref[:SB_WIDTH, :]) + _dot(mmla_ref[0], w_ref[SB_WIDTH:, :])
    o_ref[0] = x_ref[0] + mod_ref[0, 2:3, :] * y


def _outproj(mixed_sb, mixed_mla, x, mod, w_out):
    B, S, D = x.shape
    ts = min(OUT_ROW_TILE, S)

    def rows(width):
        return pl.BlockSpec((1, ts, width), lambda b, i: (b, i, 0))

    return pl.pallas_call(
        _outproj_kernel,
        grid=(B, S // ts),
        in_specs=[rows(SB_WIDTH), rows(MLA_WIDTH), rows(D),
                  pl.BlockSpec((1, 3, D), lambda b, i: (b, 0, 0)),
                  pl.BlockSpec(w_out.shape, lambda b, i: (0, 0))],
        out_specs=rows(D),
        out_shape=jax.ShapeDtypeStruct((B, S, D), x.dtype),
        compiler_params=pltpu.CompilerParams(
            dimension_semantics=("arbitrary", "arbitrary"), vmem_limit_bytes=VMEM_LIMIT),
        name="outproj",
    )(mixed_sb, mixed_mla, x, mod, w_out)


def _layer(x, c, tables, w_ada, b_ada, norm_w, w_in, q_lora_norm, w_uq, kv_lora_norm, w_ukv,
           q_head_norm, k_head_norm, w_out):
    B, S, D = x.shape
    mod = _adaln(c.T, w_ada, b_ada[None, :]).reshape(B, 3, D)

    ts = min(ROW_TILE, S)
    w_in_b = w_in.astype(BF16)
    w_gm = w_in[:, _C_GMLA:].astype(BF16)
    w_uq_t = w_uq.T.astype(BF16)
    w_ukv_h = w_ukv.reshape(KV_LORA_RANK, MLA_HEADS, MLA_NOPE_DIM + MLA_V_DIM)
    w_uk_t = w_ukv_h[:, :, :MLA_NOPE_DIM].reshape(KV_LORA_RANK, -1).T.astype(BF16)
    w_uv = w_ukv_h[:, :, MLA_NOPE_DIM:].reshape(KV_LORA_RANK, MLA_WIDTH).T.astype(BF16)
    gq = jnp.broadcast_to(q_head_norm[:, None], (MLA_QK_DIM, ts))
    gk = jnp.broadcast_to(k_head_norm[:, None], (MLA_QK_DIM, ts))

    (q_sb, k_sb, v_sb, g_sb, q_m, k_m, v_m, g_m) = _inproj(
        x, mod, norm_w[None, :], w_in_b, w_gm, q_lora_norm[None, :], w_uq_t, kv_lora_norm[None, :],
        w_uk_t, w_uv, gq, gk, *tables)

    mixed_sb = _sb_attention(q_sb, k_sb, v_sb, g_sb)
    mixed_mla = _mla_attention(q_m, k_m, v_m, g_m)
    return _outproj(mixed_sb, mixed_mla, x, mod, w_out.astype(BF16))


def _rope_tables(positions, dtype):
    inv_freq = ROPE_THETA ** (-jnp.arange(0, MLA_ROPE_DIM, 2, dtype=F32) / MLA_ROPE_DIM)
    ang = inv_freq[None, :, None] * positions.astype(F32)[:, None, :]
    return jnp.cos(ang).astype(dtype), jnp.sin(ang).astype(dtype)


def kernel(x, c, positions, w_ada, b_ada, norm_w, w_in, q_lora_norm, w_uq, kv_lora_norm, w_ukv,
           q_head_norm, k_head_norm, w_out):
    tables = _rope_tables(positions, x.dtype)
    for l in range(w_ada.shape[0]):
        x = _layer(x, c, tables, w_ada[l], b_ada[l], norm_w[l], w_in[l], q_lora_norm[l], w_uq[l],
                   kv_lora_norm[l], w_ukv[l], q_head_norm[l], k_head_norm[l], w_out[l])
    return x
```

```python
import math

import jax
import jax.numpy as jnp
from jax import lax
from jax.experimental import pallas as pl
from jax.experimental.pallas import tpu as pltpu

F32 = jnp.float32
BF16 = jnp.bfloat16

SB_HEADS = 8
SB_HEAD_DIM = 64
SB_WIDTH = SB_HEADS * SB_HEAD_DIM
MLA_HEADS = 8
MLA_NOPE_DIM = 64
MLA_ROPE_DIM = 32
MLA_QK_DIM = MLA_NOPE_DIM + MLA_ROPE_DIM
MLA_V_DIM = 64
MLA_WIDTH = MLA_HEADS * MLA_V_DIM
MLA_V_ONES = 16
MLA_V_ROWS = MLA_V_DIM + MLA_V_ONES
Q_LORA_RANK = 384
KV_LORA_RANK = 256
ROPE_THETA = 10000.0
EPS = 1e-6

LANES = 128
SUBLANES = 8
HEAD_PAIR = 2
MLA_PAD_WIDTH = MLA_HEADS * LANES

_C_QSB = 0
_C_KSB = _C_QSB + SB_WIDTH
_C_VSB = _C_KSB + SB_WIDTH
_C_GSB = _C_VSB + SB_WIDTH
_C_CQ = _C_GSB + SB_WIDTH
_C_CKV = _C_CQ + Q_LORA_RANK
_C_KR = _C_CKV + KV_LORA_RANK
_C_GMLA = _C_KR + MLA_ROPE_DIM

ADA_COL_TILE = 512
ROW_TILE = 512
OUT_ROW_TILE = 2048
SB_TQ = 256
SB_TK = 256
SB_PAIRS = 4
SB_SUBS = 2
SB_FIRST_VISIT = 2
SB_STAGE_LAG = (2, 4)
MLA_GROUP = 2
MLA_UNROLL = 4
MLA_TQ = 512
MLA_TK = 256
VMEM_LIMIT = 56 * 1024 * 1024

SB_DEAD_LOG2_DROP = 160.0
SB_Q_SCALE = math.log2(math.e) / math.sqrt(SB_HEAD_DIM)
MLA_Q_SCALE = math.log2(math.e) / math.sqrt(MLA_QK_DIM)


def _silu(g):
    return g * (1.0 / (1.0 + jnp.exp(-g)))


def _dot(a, b):
    return jnp.dot(a, b, preferred_element_type=F32)


def _dot_nt(a, b):
    return lax.dot_general(a, b, (((1,), (1,)), ((), ())), preferred_element_type=F32)


def _dot_tn(a, b):
    return lax.dot_general(a, b, (((0,), (0,)), ((), ())), preferred_element_type=F32)


def _adaln_kernel(ct_ref, w_ref, b_ref, o_ref):
    s_t = _silu(ct_ref[...])
    w = w_ref[...]
    rows = [jnp.sum(w * s_t[:, b:b + 1], axis=0, keepdims=True) for b in range(s_t.shape[1])]
    o_ref[...] = jnp.concatenate(rows, axis=0) + b_ref[...]


def _adaln(c_t, w_ada, b_ada):
    d, rows = c_t.shape
    n = w_ada.shape[1]
    bn = ADA_COL_TILE
    return pl.pallas_call(
        _adaln_kernel,
        grid=(n // bn,),
        in_specs=[pl.BlockSpec((d, rows), lambda j: (0, 0)),
                  pl.BlockSpec((d, bn), lambda j: (0, j)),
                  pl.BlockSpec((1, bn), lambda j: (0, j))],
        out_specs=pl.BlockSpec((rows, bn), lambda j: (0, j)),
        out_shape=jax.ShapeDtypeStruct((rows, n), F32),
        name="adaln",
    )(c_t, w_ada, b_ada)


def _rope_t(x1, x2, cos, sin):
    return x1 * cos - x2 * sin, x2 * cos + x1 * sin


def _inproj_kernel(x_ref, mod_ref, nw_ref, win_ref, wgm_ref, qln_ref, wuqt_ref, kvln_ref, wukt_ref, wuvt_ref,
                   gq_ref, gk_ref, cos_ref, sin_ref,
                   qsb_ref, ksb_ref, vsb_ref, gsb_ref, qmt_ref, km_ref, vmt_ref, gm_ref):
    half = MLA_ROPE_DIM // 2
    n0, n1, n2 = MLA_NOPE_DIM, MLA_NOPE_DIM + half, MLA_QK_DIM
    x = x_ref[0]
    ts = x.shape[0]
    y = x * lax.rsqrt(jnp.mean(x * x, axis=-1, keepdims=True) + EPS) * nw_ref[...]
    h = y * (1.0 + mod_ref[0, 1:2, :]) + mod_ref[0, 0:1, :]

    def proj(lo, hi):
        return _dot(h, win_ref[:, lo:hi])

    def rms(t, w_ref):
        return t * lax.rsqrt(jnp.mean(t * t, axis=-1, keepdims=True) + EPS) * w_ref[...]

    latents = proj(_C_CQ, _C_KR + LANES)
    cq = latents[:, :Q_LORA_RANK]
    ckv = latents[:, Q_LORA_RANK:Q_LORA_RANK + KV_LORA_RANK]
    kr = latents[:, Q_LORA_RANK + KV_LORA_RANK:]
    qsb_ref[0] = (proj(_C_QSB, _C_KSB) * SB_Q_SCALE).astype(BF16)
    ksb_ref[0] = proj(_C_KSB, _C_VSB).astype(BF16)

    cqn = rms(cq, qln_ref).astype(BF16)
    ckvn = rms(ckv, kvln_ref).astype(BF16)
    q_t = _dot_nt(wuqt_ref[...], cqn)
    k_t = _dot_nt(wukt_ref[...], ckvn)
    v_t = _dot_nt(wuvt_ref[...], ckvn)
    vsb_ref[0] = proj(_C_VSB, _C_GSB).astype(BF16)
    gsb_ref[0] = proj(_C_GSB, _C_CQ)
    gm_ref[0] = _dot(h, wgm_ref[...])

    cos, sin = cos_ref[0], sin_ref[0]
    pad_rows = jnp.zeros((LANES - MLA_QK_DIM, ts), F32)

    gq = gq_ref[...]
    for hd in range(MLA_HEADS):
        blk = q_t[hd * MLA_QK_DIM:(hd + 1) * MLA_QK_DIM]
        inv = lax.rsqrt(jnp.sum(blk * blk, axis=0, keepdims=True) * (1.0 / MLA_QK_DIM) + EPS)
        g = blk * gq
        r1, r2 = _rope_t(g[n0:n1], g[n1:n2], cos, sin)
        out = jnp.concatenate([g[:n0], r1, r2, pad_rows], axis=0) * (inv * MLA_Q_SCALE)
        qmt_ref[0, hd * LANES:(hd + 1) * LANES, :] = out.astype(BF16)

    ones_rows = jnp.ones((MLA_V_ONES, ts), F32)
    v_rows = []
    for hd in range(MLA_HEADS):
        v_rows += [v_t[hd * MLA_V_DIM:(hd + 1) * MLA_V_DIM], ones_rows]
    vmt_ref[0] = jnp.concatenate(v_rows, axis=0).astype(BF16)

    gk = gk_ref[...]
    kr_t = kr.T
    x1, x2 = kr_t[:half], kr_t[half:MLA_ROPE_DIM]
    kr_ssq = jnp.sum(x1 * x1 + x2 * x2, axis=0, keepdims=True)
    r1, r2 = _rope_t(x1 * gk[n0:n1], x2 * gk[n1:n2], cos, sin)
    for hd in range(MLA_HEADS):
        kn = k_t[hd * n0:(hd + 1) * n0]
        ssq = jnp.sum(kn * kn, axis=0, keepdims=True) + kr_ssq
        inv = lax.rsqrt(ssq * (1.0 / MLA_QK_DIM) + EPS)
        out = jnp.concatenate([kn * gk[:n0], r1, r2, pad_rows], axis=0) * inv
        km_ref[0, :, hd * LANES:(hd + 1) * LANES] = out.astype(BF16).T


def _inproj(x, mod, norm_w, w_in_b, w_gm, qln, w_uq_t, kvln, w_uk_t, w_uv, gq, gk, cos_t, sin_t):
    B, S, D = x.shape
    ts = min(ROW_TILE, S)
    grid = (B, S // ts)

    def whole(a):
        return pl.BlockSpec(a.shape, lambda b, i: (0,) * a.ndim)

    def once(a):
        return pl.BlockSpec(a.shape, lambda b, i: (0,) * a.ndim, pipeline_mode=pl.Buffered(1))

    def rows(width):
        return pl.BlockSpec((1, ts, width), lambda b, i: (b, i, 0))

    def cols(height):
        return pl.BlockSpec((1, height, ts), lambda b, i: (b, 0, i))

    out_shapes = (
        jax.ShapeDtypeStruct((B, S, SB_WIDTH), BF16),
        jax.ShapeDtypeStruct((B, S, SB_WIDTH), BF16),
        jax.ShapeDtypeStruct((B, S, SB_WIDTH), BF16),
        jax.ShapeDtypeStruct((B, S, SB_WIDTH), F32),
        jax.ShapeDtypeStruct((B, MLA_PAD_WIDTH, S), BF16),
        jax.ShapeDtypeStruct((B, S, MLA_PAD_WIDTH), BF16),
        jax.ShapeDtypeStruct((B, MLA_HEADS * MLA_V_ROWS, S), BF16),
        jax.ShapeDtypeStruct((B, S, MLA_WIDTH), F32),
    )
    out_specs = [rows(s.shape[-1]) for s in out_shapes]
    out_specs[4] = cols(MLA_PAD_WIDTH)
    out_specs[6] = cols(MLA_HEADS * MLA_V_ROWS)
    half = MLA_ROPE_DIM // 2
    return pl.pallas_call(
        _inproj_kernel,
        grid=grid,
        in_specs=[rows(D),
                  pl.BlockSpec((1, 3, D), lambda b, i: (b, 0, 0)),
                  whole(norm_w), once(w_in_b), once(w_gm), whole(qln), whole(w_uq_t), whole(kvln),
                  whole(w_uk_t), whole(w_uv), whole(gq), whole(gk),
                  cols(half), cols(half)],
        out_specs=out_specs,
        out_shape=out_shapes,
        compiler_params=pltpu.CompilerParams(
            dimension_semantics=("arbitrary", "arbitrary"), vmem_limit_bytes=VMEM_LIMIT),
        name="inproj",
    )(x, mod, norm_w, w_in_b, w_gm, qln, w_uq_t, kvln, w_uk_t, w_uv, gq, gk, cos_t, sin_t)


def _sb_kernel(q_ref, k_ref, v_ref, g_ref, o_ref, acc_ref, carry_ref):
    tq, tk = SB_TQ, SB_TK
    n_heads = SB_PAIRS * HEAD_PAIR
    i = pl.program_id(2)
    lane = lax.broadcasted_iota(jnp.int32, (tq, LANES), 1)
    q_heads = []
    for sub in range(SB_SUBS):
        heads = []
        for p in range(SB_PAIRS):
            q_pair = q_ref[0, sub * tq:(sub + 1) * tq, p * LANES:(p + 1) * LANES]
            for hh in range(HEAD_PAIR):
                heads.append(jnp.where((lane // SB_HEAD_DIM) == hh, q_pair, jnp.zeros_like(q_pair)))
        q_heads.append(heads)

    strict = (lax.broadcasted_iota(jnp.int32, (tk, tq), 0)
              < lax.broadcasted_iota(jnp.int32, (tk, tq), 1))
    l_row = lax.broadcasted_iota(jnp.int32, (tk + SUBLANES, tk), 0)
    l_col = lax.broadcasted_iota(jnp.int32, (tk + SUBLANES, tk), 1)
    later = jnp.where(jnp.logical_or(l_col >= l_row, l_row >= tk), 1.0, 0.0).astype(F32)

    acc_ref[...] = jnp.zeros_like(acc_ref)
    carry_ref[...] = jnp.zeros_like(carry_ref)

    def visit(plan, diag_first):
        depth = max(len(tiles) for _, tiles in plan)
        items = [(sub, t, tiles[t], h) for t in range(depth) for sub, tiles in plan
                 if t < len(tiles) for h in range(n_heads)]

        def pair_block(ref, j, h):
            p = h // HEAD_PAIR
            return ref[0, pl.ds(pl.multiple_of(j * tk, tk), tk), p * LANES:(p + 1) * LANES]

        n = len(items)
        zs, afters, col_sums = [None] * n, [None] * n, [None] * n
        carries = {(sub, h): carry_ref[sub, h] for sub, _ in plan for h in range(n_heads)}
        outs = {key: None for key in carries}

        def scores(idx):
            sub, _, j, h = items[idx]
            zs[idx] = _dot_nt(pair_block(k_ref, j, h), q_heads[sub][h])

        def drops(idx):
            _, t, _, _ = items[idx]
            z = zs[idx]
            drop = jnp.maximum(z, 0.0) + jnp.log2(1.0 + jnp.exp2(-jnp.abs(z)))
            if diag_first and t == 0:
                drop = jnp.where(strict, drop, 0.0)
            hi = drop.astype(BF16).astype(F32)
            sums = _dot(later, hi) + _dot(later, drop - hi)
            afters[idx] = sums[:tk]
            col_sums[idx] = sums[tk:tk + 1]

        def weights(idx):
            sub, t, j, h = items[idx]
            w = jnp.exp2(zs[idx] - (afters[idx] + carries[sub, h]))
            if diag_first and t == 0:
                w = jnp.where(strict, w, 0.0)
            pv = _dot_tn(pair_block(v_ref, j, h), w.astype(BF16))
            lo_row = (h % HEAD_PAIR) * SB_HEAD_DIM
            pv = pv[lo_row:lo_row + SB_HEAD_DIM, :]
            outs[sub, h] = pv if outs[sub, h] is None else outs[sub, h] + pv
            carries[sub, h] = carries[sub, h] + col_sums[idx]

        for step in range(n + SB_STAGE_LAG[1]):
            if step < n:
                scores(step)
            if 0 <= step - SB_STAGE_LAG[0] < n:
                drops(step - SB_STAGE_LAG[0])
            if 0 <= step - SB_STAGE_LAG[1] < n:
                weights(step - SB_STAGE_LAG[1])
        alive = []
        for sub, _ in plan:
            dead = None
            for h in range(n_heads):
                acc_ref[sub, h] += outs[sub, h]
                carry_ref[sub, h] = carries[sub, h]
                dead = carries[sub, h] if dead is None else jnp.minimum(dead, carries[sub, h])
            alive.append(jnp.min(dead) < SB_DEAD_LOG2_DROP)
        return alive

    first = SB_SUBS * i

    def usual_step():
        return visit([(sub, [first + sub - t for t in range(SB_FIRST_VISIT)])
                      for sub in range(SB_SUBS)], True)

    def first_step():
        return visit([(sub, [sub - t for t in range(SB_FIRST_VISIT) if sub - t >= 0])
                      for sub in range(SB_SUBS)], True)

    assert SB_SUBS >= SB_FIRST_VISIT - 1
    alive = lax.cond(i > 0, usual_step, first_step)

    for sub in range(SB_SUBS):
        def cond(state):
            j, still_alive = state
            return jnp.logical_and(j >= 0, still_alive)

        def body(state, sub=sub):
            j, _ = state
            return j - 1, visit([(sub, [j])], False)[0]

        lax.while_loop(cond, body, (first + sub - SB_FIRST_VISIT, alive[sub]))

    for sub in range(SB_SUBS):
        rows = slice(sub * tq, (sub + 1) * tq)
        o_t = acc_ref[sub].reshape(n_heads * SB_HEAD_DIM, tq)
        o_ref[0, rows, :] = (o_t.T * _silu(g_ref[0, rows, :])).astype(o_ref.dtype)


def _sb_attention(q, k, v, g):
    B, S, W = q.shape
    rows = SB_SUBS * SB_TQ
    assert S % rows == 0 and SB_TQ == SB_TK
    wb = SB_PAIRS * LANES
    n_heads = SB_PAIRS * HEAD_PAIR
    grid = (B, W // wb, S // rows)
    q_spec = pl.BlockSpec((1, rows, wb), lambda b, p, i: (b, i, p))
    kv_spec = pl.BlockSpec((1, S, wb), lambda b, p, i: (b, 0, p))
    return pl.pallas_call(
        _sb_kernel,
        grid=grid,
        in_specs=[q_spec, kv_spec, kv_spec, q_spec],
        out_specs=q_spec,
        out_shape=jax.ShapeDtypeStruct((B, S, W), BF16),
        scratch_shapes=[pltpu.VMEM((SB_SUBS, n_heads, SB_HEAD_DIM, SB_TQ), F32),
                        pltpu.VMEM((SB_SUBS, n_heads, 1, SB_TQ), F32)],
        compiler_params=pltpu.CompilerParams(
            dimension_semantics=("arbitrary", "arbitrary", "arbitrary"),
            vmem_limit_bytes=VMEM_LIMIT),
        name="sb_attn",
    )(q, k, v, g)


def _mla_kernel(q_ref, qnext_ref, k_ref, v_ref, g_ref, o_ref, acc_ref, m_ref, s_ref, smax_ref):
    tq, tk = MLA_TQ, MLA_TK
    diag_tiles = tq // tk
    i = pl.program_id(2)
    neg = jnp.finfo(F32).min

    acc_ref[...] = jnp.zeros_like(acc_ref)
    m_ref[...] = jnp.full_like(m_ref, neg)

    def score(j, slot, diag=None, queries=q_ref):
        start = pl.multiple_of(j * tk, tk)
        lo = 0 if diag is None else diag * tk
        for hh in range(MLA_GROUP):
            q_t = queries[0, hh * LANES:(hh + 1) * LANES, lo:]
            k_blk = k_ref[0, pl.ds(start, tk), hh * LANES:(hh + 1) * LANES]
            s = _dot(k_blk, q_t)
            s_ref[slot, hh, :, lo:] = s
            if diag is None:
                smax_ref[slot, hh] = jnp.broadcast_to(jnp.max(s, axis=0, keepdims=True), (SUBLANES, tq))

    def consume(j, slot, diag=None):
        start = pl.multiple_of(j * tk, tk)
        lo = 0 if diag is None else diag * tk
        for hh in range(MLA_GROUP):
            v_t = v_ref[0, hh * MLA_V_ROWS:(hh + 1) * MLA_V_ROWS, pl.ds(start, tk)]
            s = s_ref[slot, hh, :, lo:]
            if diag is None:
                tile_max = smax_ref[slot, hh]
            else:
                causal = (lax.broadcasted_iota(jnp.int32, s.shape, 0)
                          <= lax.broadcasted_iota(jnp.int32, s.shape, 1))
                s = jnp.where(causal, s, neg)
                tile_max = jnp.max(s, axis=0, keepdims=True)
            m_old = m_ref[hh, :, lo:]
            m_new = jnp.maximum(m_old, tile_max)
            alpha = jnp.exp2(m_old[:1] - m_new[:1])
            p = jnp.exp2(s - m_new[:1])
            acc_ref[hh, :, lo:] = alpha * acc_ref[hh, :, lo:] + _dot(v_t, p.astype(BF16))
            m_ref[hh, :, lo:] = m_new

    base = i * diag_tiles

    def two_tiles(t):
        score(t + 1, 1)
        consume(t, 0)
        score(t + 2, 0)
        consume(t + 1, 1)

    looped = jnp.maximum(base - 2, 0)
    done = 0
    pairs = MLA_UNROLL
    while pairs >= 1:
        def group(jj, carry, pairs=pairs, done=done):
            for r in range(pairs):
                two_tiles(done + 2 * pairs * jj + 2 * r)
            return carry

        n_groups = (looped - done) // (2 * pairs)
        lax.fori_loop(0, n_groups, group, 0)
        done = done + n_groups * 2 * pairs
        pairs //= 2

    def diagonal_block():
        for d in range(diag_tiles):
            if d + 1 < diag_tiles:
                score(base + d + 1, (d + 1) % 2, d + 1)
            consume(base + d, d % 2, d)
            if d == 0:
                score(0, 0, queries=qnext_ref)
        o_t = jnp.concatenate(
            [acc_ref[hh, :MLA_V_DIM] / acc_ref[hh, MLA_V_DIM:MLA_V_DIM + 1] for hh in range(MLA_GROUP)],
            axis=0)
        o_ref[0] = (o_t.T * _silu(g_ref[0])).astype(o_ref.dtype)

    @pl.when(i > 0)
    def _():
        two_tiles(base - 2)
        diagonal_block()

    @pl.when(i == 0)
    def _():
        score(0, 0)
        diagonal_block()


def _mla_attention(q, k, v, g):
    B, S, W = g.shape
    tq = MLA_TQ
    group_rows = MLA_GROUP * MLA_V_ROWS
    group_out = MLA_GROUP * MLA_V_DIM
    assert S % tq == 0 and MLA_TQ % (2 * MLA_TK) == 0 and group_out % LANES == 0
    grid = (B, W // group_out, S // tq)
    return pl.pallas_call(
        _mla_kernel,
        grid=grid,
        in_specs=[pl.BlockSpec((1, MLA_GROUP * LANES, tq), lambda b, p, i: (b, p, i)),
                  pl.BlockSpec((1, MLA_GROUP * LANES, tq),
                               lambda b, p, i: (b, p, jnp.minimum(i + 1, S // tq - 1))),
                  pl.BlockSpec((1, S, MLA_GROUP * LANES), lambda b, p, i: (b, 0, p)),
                  pl.BlockSpec((1, group_rows, S), lambda b, p, i: (b, p, 0)),
                  pl.BlockSpec((1, tq, group_out), lambda b, p, i: (b, i, p))],
        out_specs=pl.BlockSpec((1, tq, group_out), lambda b, p, i: (b, i, p)),
        out_shape=jax.ShapeDtypeStruct((B, S, W), BF16),
        scratch_shapes=[pltpu.VMEM((MLA_GROUP, MLA_V_ROWS, tq), F32),
                        pltpu.VMEM((MLA_GROUP, SUBLANES, tq), F32),
                        pltpu.VMEM((2, MLA_GROUP, MLA_TK, tq), F32),
                        pltpu.VMEM((2, MLA_GROUP, SUBLANES, tq), F32)],
        compiler_params=pltpu.CompilerParams(
            dimension_semantics=("arbitrary", "arbitrary", "arbitrary"),
            vmem_limit_bytes=VMEM_LIMIT),
        name="mla_attn",
    )(q, q, k, v, g)


def _outproj_kernel(msb_ref, mmla_ref, x_ref, mod_ref, w_ref, o_ref):
    y = _dot(msb_ref[0], w_ref[:SB_WIDTH, :]) + _dot(mmla_ref[0], w_ref[SB_WIDTH:, :])
    o_ref[0] = x_ref[0] + mod_ref[0, 2:3, :] * y


def _outproj(mixed_sb, mixed_mla, x, mod, w_out):
    B, S, D = x.shape
    ts = min(OUT_ROW_TILE, S)

    def rows(width):
        return pl.BlockSpec((1, ts, width), lambda b, i: (b, i, 0))

    return pl.pallas_call(
        _outproj_kernel,
        grid=(B, S // ts),
        in_specs=[rows(SB_WIDTH), rows(MLA_WIDTH), rows(D),
                  pl.BlockSpec((1, 3, D), lambda b, i: (b, 0, 0)),
                  pl.BlockSpec(w_out.shape, lambda b, i: (0, 0))],
        out_specs=rows(D),
        out_shape=jax.ShapeDtypeStruct((B, S, D), x.dtype),
        compiler_params=pltpu.CompilerParams(
            dimension_semantics=("arbitrary", "arbitrary"), vmem_limit_bytes=VMEM_LIMIT),
        name="outproj",
    )(mixed_sb, mixed_mla, x, mod, w_out)


def _layer(x, c, tables, w_ada, b_ada, norm_w, w_in, q_lora_norm, w_uq, kv_lora_norm, w_ukv,
           q_head_norm, k_head_norm, w_out):
    B, S, D = x.shape
    mod = _adaln(c.T, w_ada, b_ada[None, :]).reshape(B, 3, D)

    ts = min(ROW_TILE, S)
    w_in_b = w_in
    w_gm = w_in[:, _C_GMLA:]
    w_uq_t = w_uq.T.astype(BF16)
    w_ukv_h = w_ukv.reshape(KV_LORA_RANK, MLA_HEADS, MLA_NOPE_DIM + MLA_V_DIM)
    w_uk_t = w_ukv_h[:, :, :MLA_NOPE_DIM].reshape(KV_LORA_RANK, -1).T.astype(BF16)
    w_uv = w_ukv_h[:, :, MLA_NOPE_DIM:].reshape(KV_LORA_RANK, MLA_WIDTH).T.astype(BF16)
    gq = jnp.broadcast_to(q_head_norm[:, None], (MLA_QK_DIM, ts))
    gk = jnp.broadcast_to(k_head_norm[:, None], (MLA_QK_DIM, ts))

    (q_sb, k_sb, v_sb, g_sb, q_m, k_m, v_m, g_m) = _inproj(
        x, mod, norm_w[None, :], w_in_b, w_gm, q_lora_norm[None, :], w_uq_t, kv_lora_norm[None, :],
        w_uk_t, w_uv, gq, gk, *tables)

    mixed_sb = _sb_attention(q_sb, k_sb, v_sb, g_sb)
    mixed_mla = _mla_attention(q_m, k_m, v_m, g_m)
    return _outproj(mixed_sb, mixed_mla, x, mod, w_out.astype(BF16))


def _rope_tables(positions, dtype):
    inv_freq = ROPE_THETA ** (-jnp.arange(0, MLA_ROPE_DIM, 2, dtype=F32) / MLA_ROPE_DIM)
    ang = inv_freq[None, :, None] * positions.astype(F32)[:, None, :]
    return jnp.cos(ang).astype(dtype), jnp.sin(ang).astype(dtype)


def kernel(x, c, positions, w_ada, b_ada, norm_w, w_in, q_lora_norm, w_uq, kv_lora_norm, w_ukv,
           q_head_norm, k_head_norm, w_out):
    tables = _rope_tables(positions, x.dtype)
    for l in range(w_ada.shape[0]):
        x = _layer(x, c, tables, w_ada[l], b_ada[l], norm_w[l], w_in[l], q_lora_norm[l], w_uq[l],
                   kv_lora_norm[l], w_ukv[l], q_head_norm[l], k_head_norm[l], w_out[l])
    return x
```

```python
import math

import jax
import jax.numpy as jnp
from jax import lax
from jax.experimental import pallas as pl
from jax.experimental.pallas import tpu as pltpu

F32 = jnp.float32
BF16 = jnp.bfloat16

SB_HEADS = 8
SB_HEAD_DIM = 64
SB_WIDTH = SB_HEADS * SB_HEAD_DIM
MLA_HEADS = 8
MLA_NOPE_DIM = 64
MLA_ROPE_DIM = 32
MLA_QK_DIM = MLA_NOPE_DIM + MLA_ROPE_DIM
MLA_V_DIM = 64
MLA_WIDTH = MLA_HEADS * MLA_V_DIM
MLA_V_ONES = 16
MLA_V_ROWS = MLA_V_DIM + MLA_V_ONES
Q_LORA_RANK = 384
KV_LORA_RANK = 256
ROPE_THETA = 10000.0
EPS = 1e-6

LANES = 128
SUBLANES = 8
HEAD_PAIR = 2
MLA_PAD_WIDTH = MLA_HEADS * LANES

_C_QSB = 0
_C_KSB = _C_QSB + SB_WIDTH
_C_VSB = _C_KSB + SB_WIDTH
_C_GSB = _C_VSB + SB_WIDTH
_C_CQ = _C_GSB + SB_WIDTH
_C_CKV = _C_CQ + Q_LORA_RANK
_C_KR = _C_CKV + KV_LORA_RANK
_C_GMLA = _C_KR + MLA_ROPE_DIM

ADA_COL_TILE = 512
ROW_TILE = 512
OUT_ROW_TILE = 2048
SB_TQ = 256
SB_TK = 256
SB_PAIRS = 4
SB_SUBS = 2
SB_FIRST_VISIT = 2
SB_STAGE_LAG = (2, 4)
MLA_GROUP = 2
MLA_UNROLL = 4
MLA_TQ = 512
MLA_TK = 256
VMEM_LIMIT = 56 * 1024 * 1024

SB_DEAD_LOG2_DROP = 160.0
SB_Q_SCALE = math.log2(math.e) / math.sqrt(SB_HEAD_DIM)
MLA_Q_SCALE = math.log2(math.e) / math.sqrt(MLA_QK_DIM)


def _silu(g):
    return g * (1.0 / (1.0 + jnp.exp(-g)))


def _dot(a, b):
    return jnp.dot(a, b, preferred_element_type=F32)


def _dot_nt(a, b):
    return lax.dot_general(a, b, (((1,), (1,)), ((), ())), preferred_element_type=F32)


def _dot_tn(a, b):
    return lax.dot_general(a, b, (((0,), (0,)), ((), ())), preferred_element_type=F32)


def _adaln_kernel(ct_ref, w_ref, b_ref, o_ref):
    s_t = _silu(ct_ref[...])
    w = w_ref[...]
    rows = [jnp.sum(w * s_t[:, b:b + 1], axis=0, keepdims=True) for b in range(s_t.shape[1])]
    o_ref[...] = jnp.concatenate(rows, axis=0) + b_ref[...]


def _adaln(c_t, w_ada, b_ada):
    d, rows = c_t.shape
    n = w_ada.shape[1]
    bn = ADA_COL_TILE
    return pl.pallas_call(
        _adaln_kernel,
        grid=(n // bn,),
        in_specs=[pl.BlockSpec((d, rows), lambda j: (0, 0)),
                  pl.BlockSpec((d, bn), lambda j: (0, j)),
                  pl.BlockSpec((1, bn), lambda j: (0, j))],
        out_specs=pl.BlockSpec((rows, bn), lambda j: (0, j)),
        out_shape=jax.ShapeDtypeStruct((rows, n), F32),
        name="adaln",
    )(c_t, w_ada, b_ada)


def _rope_t(x1, x2, cos, sin):
    return x1 * cos - x2 * sin, x2 * cos + x1 * sin


def _inproj_kernel(x_ref, mod_ref, nw_ref, win_ref, wgm_ref, qln_ref, wuqt_ref, kvln_ref, wukt_ref, wuvt_ref,
                   gq_ref, gk_ref, cos_ref, sin_ref,
                   qsb_ref, ksb_ref, vsb_ref, gsb_ref, qmt_ref, km_ref, vmt_ref, gm_ref):
    half = MLA_ROPE_DIM // 2
    n0, n1, n2 = MLA_NOPE_DIM, MLA_NOPE_DIM + half, MLA_QK_DIM
    x = x_ref[0]
    ts = x.shape[0]
    y = x * lax.rsqrt(jnp.mean(x * x, axis=-1, keepdims=True) + EPS) * nw_ref[...]
    h = (y * (1.0 + mod_ref[0, 1:2, :]) + mod_ref[0, 0:1, :]).astype(BF16)

    def proj(lo, hi):
        return _dot(h, win_ref[:, lo:hi])

    def rms(t, w_ref):
        return t * lax.rsqrt(jnp.mean(t * t, axis=-1, keepdims=True) + EPS) * w_ref[...]

    latents = proj(_C_CQ, _C_KR + LANES)
    cq = latents[:, :Q_LORA_RANK]
    ckv = latents[:, Q_LORA_RANK:Q_LORA_RANK + KV_LORA_RANK]
    kr = latents[:, Q_LORA_RANK + KV_LORA_RANK:]
    qsb_ref[0] = (proj(_C_QSB, _C_KSB) * SB_Q_SCALE).astype(BF16)
    ksb_ref[0] = proj(_C_KSB, _C_VSB).astype(BF16)

    cqn = rms(cq, qln_ref).astype(BF16)
    ckvn = rms(ckv, kvln_ref).astype(BF16)
    q_t = _dot_nt(wuqt_ref[...], cqn)
    k_t = _dot_nt(wukt_ref[...], ckvn)
    v_t = _dot_nt(wuvt_ref[...], ckvn)
    vsb_ref[0] = proj(_C_VSB, _C_GSB).astype(BF16)
    gsb_ref[0] = proj(_C_GSB, _C_CQ)
    gm_ref[0] = _dot(h, wgm_ref[...])

    cos, sin = cos_ref[0], sin_ref[0]
    pad_rows = jnp.zeros((LANES - MLA_QK_DIM, ts), F32)

    gq = gq_ref[...]
    for hd in range(MLA_HEADS):
        blk = q_t[hd * MLA_QK_DIM:(hd + 1) * MLA_QK_DIM]
        inv = lax.rsqrt(jnp.sum(blk * blk, axis=0, keepdims=True) * (1.0 / MLA_QK_DIM) + EPS)
        g = blk * gq
        r1, r2 = _rope_t(g[n0:n1], g[n1:n2], cos, sin)
        out = jnp.concatenate([g[:n0], r1, r2, pad_rows], axis=0) * (inv * MLA_Q_SCALE)
        qmt_ref[0, hd * LANES:(hd + 1) * LANES, :] = out.astype(BF16)

    ones_rows = jnp.ones((MLA_V_ONES, ts), F32)
    v_rows = []
    for hd in range(MLA_HEADS):
        v_rows += [v_t[hd * MLA_V_DIM:(hd + 1) * MLA_V_DIM], ones_rows]
    vmt_ref[0] = jnp.concatenate(v_rows, axis=0).astype(BF16)

    gk = gk_ref[...]
    kr_t = kr.T
    x1, x2 = kr_t[:half], kr_t[half:MLA_ROPE_DIM]
    kr_ssq = jnp.sum(x1 * x1 + x2 * x2, axis=0, keepdims=True)
    r1, r2 = _rope_t(x1 * gk[n0:n1], x2 * gk[n1:n2], cos, sin)
    for hd in range(MLA_HEADS):
        kn = k_t[hd * n0:(hd + 1) * n0]
        ssq = jnp.sum(kn * kn, axis=0, keepdims=True) + kr_ssq
        inv = lax.rsqrt(ssq * (1.0 / MLA_QK_DIM) + EPS)
        out = jnp.concatenate([kn * gk[:n0], r1, r2, pad_rows], axis=0) * inv
        km_ref[0, :, hd * LANES:(hd + 1) * LANES] = out.astype(BF16).T


def _inproj(x, mod, norm_w, w_in_b, w_gm, qln, w_uq_t, kvln, w_uk_t, w_uv, gq, gk, cos_t, sin_t):
    B, S, D = x.shape
    ts = min(ROW_TILE, S)
    grid = (B, S // ts)

    def whole(a):
        return pl.BlockSpec(a.shape, lambda b, i: (0,) * a.ndim)

    def rows(width):
        return pl.BlockSpec((1, ts, width), lambda b, i: (b, i, 0))

    def cols(height):
        return pl.BlockSpec((1, height, ts), lambda b, i: (b, 0, i))

    out_shapes = (
        jax.ShapeDtypeStruct((B, S, SB_WIDTH), BF16),
        jax.ShapeDtypeStruct((B, S, SB_WIDTH), BF16),
        jax.ShapeDtypeStruct((B, S, SB_WIDTH), BF16),
        jax.ShapeDtypeStruct((B, S, SB_WIDTH), F32),
        jax.ShapeDtypeStruct((B, MLA_PAD_WIDTH, S), BF16),
        jax.ShapeDtypeStruct((B, S, MLA_PAD_WIDTH), BF16),
        jax.ShapeDtypeStruct((B, MLA_HEADS * MLA_V_ROWS, S), BF16),
        jax.ShapeDtypeStruct((B, S, MLA_WIDTH), F32),
    )
    out_specs = [rows(s.shape[-1]) for s in out_shapes]
    out_specs[4] = cols(MLA_PAD_WIDTH)
    out_specs[6] = cols(MLA_HEADS * MLA_V_ROWS)
    half = MLA_ROPE_DIM // 2
    return pl.pallas_call(
        _inproj_kernel,
        grid=grid,
        in_specs=[rows(D),
                  pl.BlockSpec((1, 3, D), lambda b, i: (b, 0, 0)),
                  whole(norm_w), whole(w_in_b), whole(w_gm), whole(qln), whole(w_uq_t), whole(kvln),
                  whole(w_uk_t), whole(w_uv), whole(gq), whole(gk),
                  cols(half), cols(half)],
        out_specs=out_specs,
        out_shape=out_shapes,
        compiler_params=pltpu.CompilerParams(
            dimension_semantics=("arbitrary", "arbitrary"), vmem_limit_bytes=VMEM_LIMIT),
        name="inproj",
    )(x, mod, norm_w, w_in_b, w_gm, qln, w_uq_t, kvln, w_uk_t, w_uv, gq, gk, cos_t, sin_t)


def _sb_kernel(q_ref, k_ref, v_ref, g_ref, o_ref, acc_ref, carry_ref):
    tq, tk = SB_TQ, SB_TK
    n_heads = SB_PAIRS * HEAD_PAIR
    i = pl.program_id(2)
    lane = lax.broadcasted_iota(jnp.int32, (tq, LANES), 1)
    q_heads = []
    for sub in range(SB_SUBS):
        heads = []
        for p in range(SB_PAIRS):
            q_pair = q_ref[0, sub * tq:(sub + 1) * tq, p * LANES:(p + 1) * LANES]
            for hh in range(HEAD_PAIR):
                heads.append(jnp.where((lane // SB_HEAD_DIM) == hh, q_pair, jnp.zeros_like(q_pair)))
        q_heads.append(heads)

    strict = (lax.broadcasted_iota(jnp.int32, (tk, tq), 0)
              < lax.broadcasted_iota(jnp.int32, (tk, tq), 1))
    l_row = lax.broadcasted_iota(jnp.int32, (tk + SUBLANES, tk), 0)
    l_col = lax.broadcasted_iota(jnp.int32, (tk + SUBLANES, tk), 1)
    later = jnp.where(jnp.logical_or(l_col > l_row, l_row >= tk), 1.0, 0.0).astype(F32)

    acc_ref[...] = jnp.zeros_like(acc_ref)
    carry_ref[...] = jnp.zeros_like(carry_ref)

    def visit(plan, diag_first):
        depth = max(len(tiles) for _, tiles in plan)
        items = [(sub, t, tiles[t], h) for t in range(depth) for sub, tiles in plan
                 if t < len(tiles) for h in range(n_heads)]

        def pair_block(ref, j, h):
            p = h // HEAD_PAIR
            return ref[0, pl.ds(pl.multiple_of(j * tk, tk), tk), p * LANES:(p + 1) * LANES]

        n = len(items)
        zs, log_betas, afters, col_sums = [None] * n, [None] * n, [None] * n, [None] * n
        carries = {(sub, h): carry_ref[sub, h] for sub, _ in plan for h in range(n_heads)}
        outs = {key: None for key in carries}

        def scores(idx):
            sub, _, j, h = items[idx]
            zs[idx] = _dot_nt(pair_block(k_ref, j, h), q_heads[sub][h])

        def drops(idx):
            _, t, _, _ = items[idx]
            z = zs[idx]
            drop = jnp.maximum(z, 0.0) + jnp.log2(1.0 + jnp.exp2(-jnp.abs(z)))
            if diag_first and t == 0:
                drop = jnp.where(strict, drop, 0.0)
            sums = _dot(later, drop)
            afters[idx] = sums[:tk]
            col_sums[idx] = sums[tk:tk + 1]
            log_betas[idx] = z - drop

        def weights(idx):
            sub, t, j, h = items[idx]
            w = jnp.exp2(log_betas[idx] - (afters[idx] + carries[sub, h]))
            if diag_first and t == 0:
                w = jnp.where(strict, w, 0.0)
            pv = _dot_tn(pair_block(v_ref, j, h), w.astype(BF16))
            lo_row = (h % HEAD_PAIR) * SB_HEAD_DIM
            pv = pv[lo_row:lo_row + SB_HEAD_DIM, :]
            outs[sub, h] = pv if outs[sub, h] is None else outs[sub, h] + pv
            carries[sub, h] = carries[sub, h] + col_sums[idx]

        for step in range(n + SB_STAGE_LAG[1]):
            if step < n:
                scores(step)
            if 0 <= step - SB_STAGE_LAG[0] < n:
                drops(step - SB_STAGE_LAG[0])
            if 0 <= step - SB_STAGE_LAG[1] < n:
                weights(step - SB_STAGE_LAG[1])
        alive = []
        for sub, _ in plan:
            dead = None
            for h in range(n_heads):
                acc_ref[sub, h] += outs[sub, h]
                carry_ref[sub, h] = carries[sub, h]
                dead = carries[sub, h] if dead is None else jnp.minimum(dead, carries[sub, h])
            alive.append(jnp.min(dead) < SB_DEAD_LOG2_DROP)
        return alive

    first = SB_SUBS * i

    def usual_step():
        return visit([(sub, [first + sub - t for t in range(SB_FIRST_VISIT)])
                      for sub in range(SB_SUBS)], True)

    def first_step():
        return visit([(sub, [sub - t for t in range(SB_FIRST_VISIT) if sub - t >= 0])
                      for sub in range(SB_SUBS)], True)

    assert SB_SUBS >= SB_FIRST_VISIT - 1
    alive = lax.cond(i > 0, usual_step, first_step)

    for sub in range(SB_SUBS):
        def cond(state):
            j, still_alive = state
            return jnp.logical_and(j >= 0, still_alive)

        def body(state, sub=sub):
            j, _ = state
            return j - 1, visit([(sub, [j])], False)[0]

        lax.while_loop(cond, body, (first + sub - SB_FIRST_VISIT, alive[sub]))

    for sub in range(SB_SUBS):
        rows = slice(sub * tq, (sub + 1) * tq)
        o_t = acc_ref[sub].reshape(n_heads * SB_HEAD_DIM, tq)
        o_ref[0, rows, :] = (o_t.T * _silu(g_ref[0, rows, :])).astype(o_ref.dtype)


def _sb_attention(q, k, v, g):
    B, S, W = q.shape
    rows = SB_SUBS * SB_TQ
    assert S % rows == 0 and SB_TQ == SB_TK
    wb = SB_PAIRS * LANES
    n_heads = SB_PAIRS * HEAD_PAIR
    grid = (B, W // wb, S // rows)
    q_spec = pl.BlockSpec((1, rows, wb), lambda b, p, i: (b, i, p))
    kv_spec = pl.BlockSpec((1, S, wb), lambda b, p, i: (b, 0, p))
    return pl.pallas_call(
        _sb_kernel,
        grid=grid,
        in_specs=[q_spec, kv_spec, kv_spec, q_spec],
        out_specs=q_spec,
        out_shape=jax.ShapeDtypeStruct((B, S, W), BF16),
        scratch_shapes=[pltpu.VMEM((SB_SUBS, n_heads, SB_HEAD_DIM, SB_TQ), F32),
                        pltpu.VMEM((SB_SUBS, n_heads, 1, SB_TQ), F32)],
        compiler_params=pltpu.CompilerParams(
            dimension_semantics=("arbitrary", "arbitrary", "arbitrary"),
            vmem_limit_bytes=VMEM_LIMIT),
        name="sb_attn",
    )(q, k, v, g)


def _mla_kernel(q_ref, qnext_ref, k_ref, v_ref, g_ref, o_ref, acc_ref, m_ref, s_ref, smax_ref):
    tq, tk = MLA_TQ, MLA_TK
    diag_tiles = tq // tk
    i = pl.program_id(2)
    neg = jnp.finfo(F32).min

    acc_ref[...] = jnp.zeros_like(acc_ref)
    m_ref[...] = jnp.full_like(m_ref, neg)

    def score(j, slot, diag=None, queries=q_ref):
        start = pl.multiple_of(j * tk, tk)
        lo = 0 if diag is None else diag * tk
        for hh in range(MLA_GROUP):
            q_t = queries[0, hh * LANES:(hh + 1) * LANES, lo:]
            k_blk = k_ref[0, pl.ds(start, tk), hh * LANES:(hh + 1) * LANES]
            s = _dot(k_blk, q_t)
            s_ref[slot, hh, :, lo:] = s
            if diag is None:
                smax_ref[slot, hh] = jnp.broadcast_to(jnp.max(s, axis=0, keepdims=True), (SUBLANES, tq))

    def consume(j, slot, diag=None):
        start = pl.multiple_of(j * tk, tk)
        lo = 0 if diag is None else diag * tk
        for hh in range(MLA_GROUP):
            v_t = v_ref[0, hh * MLA_V_ROWS:(hh + 1) * MLA_V_ROWS, pl.ds(start, tk)]
            s = s_ref[slot, hh, :, lo:]
            if diag is None:
                tile_max = smax_ref[slot, hh]
            else:
                causal = (lax.broadcasted_iota(jnp.int32, s.shape, 0)
                          <= lax.broadcasted_iota(jnp.int32, s.shape, 1))
                s = jnp.where(causal, s, neg)
                tile_max = jnp.max(s, axis=0, keepdims=True)
            m_old = m_ref[hh, :, lo:]
            m_new = jnp.maximum(m_old, tile_max)
            alpha = jnp.exp2(m_old[:1] - m_new[:1])
            p = jnp.exp2(s - m_new[:1])
            acc_ref[hh, :, lo:] = alpha * acc_ref[hh, :, lo:] + _dot(v_t, p.astype(BF16))
            m_ref[hh, :, lo:] = m_new

    base = i * diag_tiles

    def two_tiles(t):
        score(t + 1, 1)
        consume(t, 0)
        score(t + 2, 0)
        consume(t + 1, 1)

    looped = jnp.maximum(base - 2, 0)
    done = 0
    pairs = MLA_UNROLL
    while pairs >= 1:
        def group(jj, carry, pairs=pairs, done=done):
            for r in range(pairs):
                two_tiles(done + 2 * pairs * jj + 2 * r)
            return carry

        n_groups = (looped - done) // (2 * pairs)
        lax.fori_loop(0, n_groups, group, 0)
        done = done + n_groups * 2 * pairs
        pairs //= 2

    def diagonal_block():
        for d in range(diag_tiles):
            if d + 1 < diag_tiles:
                score(base + d + 1, (d + 1) % 2, d + 1)
            consume(base + d, d % 2, d)
            if d == 0:
                score(0, 0, queries=qnext_ref)
        o_t = jnp.concatenate(
            [acc_ref[hh, :MLA_V_DIM] / acc_ref[hh, MLA_V_DIM:MLA_V_DIM + 1] for hh in range(MLA_GROUP)],
            axis=0)
        o_ref[0] = (o_t.T * _silu(g_ref[0])).astype(o_ref.dtype)

    @pl.when(i > 0)
    def _():
        two_tiles(base - 2)
        diagonal_block()

    @pl.when(i == 0)
    def _():
        score(0, 0)
        diagonal_block()


def _mla_attention(q, k, v, g):
    B, S, W = g.shape
    tq = MLA_TQ
    group_rows = MLA_GROUP * MLA_V_ROWS
    group_out = MLA_GROUP * MLA_V_DIM
    assert S % tq == 0 and MLA_TQ % (2 * MLA_TK) == 0 and group_out % LANES == 0
    grid = (B, W // group_out, S // tq)
    return pl.pallas_call(
        _mla_kernel,
        grid=grid,
        in_specs=[pl.BlockSpec((1, MLA_GROUP * LANES, tq), lambda b, p, i: (b, p, i)),
                  pl.BlockSpec((1, MLA_GROUP * LANES, tq),
                               lambda b, p, i: (b, p, jnp.minimum(i + 1, S // tq - 1))),
                  pl.BlockSpec((1, S, MLA_GROUP * LANES), lambda b, p, i: (b, 0, p)),
                  pl.BlockSpec((1, group_rows, S), lambda b, p, i: (b, p, 0)),
                  pl.BlockSpec((1, tq, group_out), lambda b, p, i: (b, i, p))],
        out_specs=pl.BlockSpec((1, tq, group_out), lambda b, p, i: (b, i, p)),
        out_shape=jax.ShapeDtypeStruct((B, S, W), BF16),
        scratch_shapes=[pltpu.VMEM((MLA_GROUP, MLA_V_ROWS, tq), F32),
                        pltpu.VMEM((MLA_GROUP, SUBLANES, tq), F32),
                        pltpu.VMEM((2, MLA_GROUP, MLA_TK, tq), F32),
                        pltpu.VMEM((2, MLA_GROUP, SUBLANES, tq), F32)],
        compiler_params=pltpu.CompilerParams(
            dimension_semantics=("arbitrary", "arbitrary", "arbitrary"),
            vmem_limit_bytes=VMEM_LIMIT),
        name="mla_attn",
    )(q, q, k, v, g)


def _outproj_kernel(msb_ref, mmla_ref, x_ref, mod_ref, w_ref, o_ref):
    y = _dot(msb_ref[0], w_ref[:SB_WIDTH, :]) + _dot(mmla_ref[0], w_ref[SB_WIDTH:, :])
    o_ref[0] = x_ref[0] + mod_ref[0, 2:3, :] * y


def _outproj(mixed_sb, mixed_mla, x, mod, w_out):
    B, S, D = x.shape
    ts = min(OUT_ROW_TILE, S)

    def rows(width):
        return pl.BlockSpec((1, ts, width), lambda b, i: (b, i, 0))

    return pl.pallas_call(
        _outproj_kernel,
        grid=(B, S // ts),
        in_specs=[rows(SB_WIDTH), rows(MLA_WIDTH), rows(D),
                  pl.BlockSpec((1, 3, D), lambda b, i: (b, 0, 0)),
                  pl.BlockSpec(w_out.shape, lambda b, i: (0, 0))],
        out_specs=rows(D),
        out_shape=jax.ShapeDtypeStruct((B, S, D), x.dtype),
        compiler_params=pltpu.CompilerParams(
            dimension_semantics=("arbitrary", "arbitrary"), vmem_limit_bytes=VMEM_LIMIT),
        name="outproj",
    )(mixed_sb, mixed_mla, x, mod, w_out)


def _layer(x, c, tables, w_ada, b_ada, norm_w, w_in, q_lora_norm, w_uq, kv_lora_norm, w_ukv,
           q_head_norm, k_head_norm, w_out):
    B, S, D = x.shape
    mod = _adaln(c.T, w_ada, b_ada[None, :]).reshape(B, 3, D)

    ts = min(ROW_TILE, S)
    w_in_b = w_in.astype(BF16)
    w_gm = w_in[:, _C_GMLA:].astype(BF16)
    w_uq_t = w_uq.T.astype(BF16)
    w_ukv_h = w_ukv.reshape(KV_LORA_RANK, MLA_HEADS, MLA_NOPE_DIM + MLA_V_DIM)
    w_uk_t = w_ukv_h[:, :, :MLA_NOPE_DIM].reshape(KV_LORA_RANK, -1).T.astype(BF16)
    w_uv = w_ukv_h[:, :, MLA_NOPE_DIM:].reshape(KV_LORA_RANK, MLA_WIDTH).T.astype(BF16)
    gq = jnp.broadcast_to(q_head_norm[:, None], (MLA_QK_DIM, ts))
    gk = jnp.broadcast_to(k_head_norm[:, None], (MLA_QK_DIM, ts))

    (q_sb, k_sb, v_sb, g_sb, q_m, k_m, v_m, g_m) = _inproj(
        x, mod, norm_w[None, :], w_in_b, w_gm, q_lora_norm[None, :], w_uq_t, kv_lora_norm[None, :],
        w_uk_t, w_uv, gq, gk, *tables)

    mixed_sb = _sb_attention(q_sb, k_sb, v_sb, g_sb)
    mixed_mla = _mla_attention(q_m, k_m, v_m, g_m)
    return _outproj(mixed_sb, mixed_mla, x, mod, w_out.astype(BF16))


def _rope_tables(positions, dtype):
    inv_freq = ROPE_THETA ** (-jnp.arange(0, MLA_ROPE_DIM, 2, dtype=F32) / MLA_ROPE_DIM)
    ang = inv_freq[None, :, None] * positions.astype(F32)[:, None, :]
    return jnp.cos(ang).astype(dtype), jnp.sin(ang).astype(dtype)


def kernel(x, c, positions, w_ada, b_ada, norm_w, w_in, q_lora_norm, w_uq, kv_lora_norm, w_ukv,
           q_head_norm, k_head_norm, w_out):
    tables = _rope_tables(positions, x.dtype)
    for l in range(w_ada.shape[0]):
        x = _layer(x, c, tables, w_ada[l], b_ada[l], norm_w[l], w_in[l], q_lora_norm[l], w_uq[l],
                   kv_lora_norm[l], w_ukv[l], q_head_norm[l], k_head_norm[l], w_out[l])
    return x
```

```python
import math

import jax
import jax.numpy as jnp
from jax import lax
from jax.experimental import pallas as pl
from jax.experimental.pallas import tpu as pltpu

F32 = jnp.float32
BF16 = jnp.bfloat16

SB_HEADS = 8
SB_HEAD_DIM = 64
SB_WIDTH = SB_HEADS * SB_HEAD_DIM
MLA_HEADS = 8
MLA_NOPE_DIM = 64
MLA_ROPE_DIM = 32
MLA_QK_DIM = MLA_NOPE_DIM + MLA_ROPE_DIM
MLA_V_DIM = 64
MLA_WIDTH = MLA_HEADS * MLA_V_DIM
MLA_V_ONES = 8
MLA_V_ROWS = MLA_V_DIM + MLA_V_ONES
Q_LORA_RANK = 384
KV_LORA_RANK = 256
ROPE_THETA = 10000.0
EPS = 1e-6

LANES = 128
SUBLANES = 8
HEAD_PAIR = 2
MLA_PAD_WIDTH = MLA_HEADS * LANES

_C_QSB = 0
_C_KSB = _C_QSB + SB_WIDTH
_C_VSB = _C_KSB + SB_WIDTH
_C_GSB = _C_VSB + SB_WIDTH
_C_CQ = _C_GSB + SB_WIDTH
_C_CKV = _C_CQ + Q_LORA_RANK
_C_KR = _C_CKV + KV_LORA_RANK
_C_GMLA = _C_KR + MLA_ROPE_DIM

ADA_COL_TILE = 512
ROW_TILE = 512
OUT_ROW_TILE = 2048
SB_TQ = 256
SB_TK = 256
SB_PAIRS = 4
SB_SUBS = 2
SB_FIRST_VISIT = 2
SB_STAGE_LAG = (2, 4)
MLA_GROUP = 2
MLA_UNROLL = 4
MLA_TQ = 512
MLA_TK = 256
VMEM_LIMIT = 56 * 1024 * 1024

SB_DEAD_LOG2_DROP = 160.0
SB_SOFTPLUS_CLAMP = 100.0
SB_Q_SCALE = math.log2(math.e) / math.sqrt(SB_HEAD_DIM)
MLA_Q_SCALE = math.log2(math.e) / math.sqrt(MLA_QK_DIM)


def _silu(g):
    return g * (1.0 / (1.0 + jnp.exp(-g)))


def _dot(a, b):
    return jnp.dot(a, b, preferred_element_type=F32)


def _dot_nt(a, b):
    return lax.dot_general(a, b, (((1,), (1,)), ((), ())), preferred_element_type=F32)


def _dot_tn(a, b):
    return lax.dot_general(a, b, (((0,), (0,)), ((), ())), preferred_element_type=F32)


def _adaln_kernel(ct_ref, w_ref, b_ref, o_ref):
    s_t = _silu(ct_ref[...])
    w = w_ref[...]
    rows = [jnp.sum(w * s_t[:, b:b + 1], axis=0, keepdims=True) for b in range(s_t.shape[1])]
    o_ref[...] = jnp.concatenate(rows, axis=0) + b_ref[...]


def _adaln(c_t, w_ada, b_ada):
    d, rows = c_t.shape
    n = w_ada.shape[1]
    bn = ADA_COL_TILE
    return pl.pallas_call(
        _adaln_kernel,
        grid=(n // bn,),
        in_specs=[pl.BlockSpec((d, rows), lambda j: (0, 0)),
                  pl.BlockSpec((d, bn), lambda j: (0, j)),
                  pl.BlockSpec((1, bn), lambda j: (0, j))],
        out_specs=pl.BlockSpec((rows, bn), lambda j: (0, j)),
        out_shape=jax.ShapeDtypeStruct((rows, n), F32),
        name="adaln",
    )(c_t, w_ada, b_ada)


def _rope_t(x1, x2, cos, sin):
    return x1 * cos - x2 * sin, x2 * cos + x1 * sin


def _inproj_kernel(x_ref, mod_ref, nw_ref, win_ref, wgm_ref, qln_ref, wuqt_ref, kvln_ref, wukt_ref, wuvt_ref,
                   gq_ref, gk_ref, cos_ref, sin_ref,
                   qsb_ref, ksb_ref, vsb_ref, gsb_ref, qmt_ref, km_ref, vmt_ref, gm_ref):
    half = MLA_ROPE_DIM // 2
    n0, n1, n2 = MLA_NOPE_DIM, MLA_NOPE_DIM + half, MLA_QK_DIM
    x = x_ref[0]
    ts = x.shape[0]
    y = x * lax.rsqrt(jnp.mean(x * x, axis=-1, keepdims=True) + EPS) * nw_ref[...]
    h = (y * (1.0 + mod_ref[0, 1:2, :]) + mod_ref[0, 0:1, :]).astype(BF16)

    def proj(lo, hi):
        return _dot(h, win_ref[:, lo:hi])

    def rms(t, w_ref):
        return t * lax.rsqrt(jnp.mean(t * t, axis=-1, keepdims=True) + EPS) * w_ref[...]

    latents = proj(_C_CQ, _C_KR + LANES)
    cq = latents[:, :Q_LORA_RANK]
    ckv = latents[:, Q_LORA_RANK:Q_LORA_RANK + KV_LORA_RANK]
    kr = latents[:, Q_LORA_RANK + KV_LORA_RANK:]
    qsb_ref[0] = (proj(_C_QSB, _C_KSB) * SB_Q_SCALE).astype(BF16)
    ksb_ref[0] = proj(_C_KSB, _C_VSB).astype(BF16)

    cqn = rms(cq, qln_ref).astype(BF16)
    ckvn = rms(ckv, kvln_ref).astype(BF16)
    q_t = _dot_nt(wuqt_ref[...], cqn)
    k_t = _dot_nt(wukt_ref[...], ckvn)
    v_t = _dot_nt(wuvt_ref[...], ckvn)
    vsb_ref[0] = proj(_C_VSB, _C_GSB).astype(BF16)
    gsb_ref[0] = proj(_C_GSB, _C_CQ)
    gm_ref[0] = _dot(h, wgm_ref[...])

    cos, sin = cos_ref[0], sin_ref[0]
    pad_rows = jnp.zeros((LANES - MLA_QK_DIM, ts), F32)

    gq = gq_ref[...]
    for hd in range(MLA_HEADS):
        blk = q_t[hd * MLA_QK_DIM:(hd + 1) * MLA_QK_DIM]
        inv = lax.rsqrt(jnp.sum(blk * blk, axis=0, keepdims=True) * (1.0 / MLA_QK_DIM) + EPS)
        g = blk * gq
        r1, r2 = _rope_t(g[n0:n1], g[n1:n2], cos, sin)
        out = jnp.concatenate([g[:n0], r1, r2, pad_rows], axis=0) * (inv * MLA_Q_SCALE)
        qmt_ref[0, hd * LANES:(hd + 1) * LANES, :] = out.astype(BF16)

    ones_rows = jnp.ones((MLA_V_ONES, ts), F32)
    v_rows = []
    for hd in range(MLA_HEADS):
        v_rows += [v_t[hd * MLA_V_DIM:(hd + 1) * MLA_V_DIM], ones_rows]
    vmt_ref[0] = jnp.concatenate(v_rows, axis=0)

    gk = gk_ref[...]
    kr_t = kr.T
    x1, x2 = kr_t[:half], kr_t[half:MLA_ROPE_DIM]
    kr_ssq = jnp.sum(x1 * x1 + x2 * x2, axis=0, keepdims=True)
    r1, r2 = _rope_t(x1 * gk[n0:n1], x2 * gk[n1:n2], cos, sin)
    for hd in range(MLA_HEADS):
        kn = k_t[hd * n0:(hd + 1) * n0]
        ssq = jnp.sum(kn * kn, axis=0, keepdims=True) + kr_ssq
        inv = lax.rsqrt(ssq * (1.0 / MLA_QK_DIM) + EPS)
        out = jnp.concatenate([kn * gk[:n0], r1, r2, pad_rows], axis=0) * inv
        km_ref[0, :, hd * LANES:(hd + 1) * LANES] = out.astype(BF16).T


def _inproj(x, mod, norm_w, w_in_b, w_gm, qln, w_uq_t, kvln, w_uk_t, w_uv, gq, gk, cos_t, sin_t):
    B, S, D = x.shape
    ts = min(ROW_TILE, S)
    grid = (B, S // ts)

    def whole(a):
        return pl.BlockSpec(a.shape, lambda b, i: (0,) * a.ndim)

    def rows(width):
        return pl.BlockSpec((1, ts, width), lambda b, i: (b, i, 0))

    def cols(height):
        return pl.BlockSpec((1, height, ts), lambda b, i: (b, 0, i))

    out_shapes = (
        jax.ShapeDtypeStruct((B, S, SB_WIDTH), BF16),
        jax.ShapeDtypeStruct((B, S, SB_WIDTH), BF16),
        jax.ShapeDtypeStruct((B, S, SB_WIDTH), BF16),
        jax.ShapeDtypeStruct((B, S, SB_WIDTH), F32),
        jax.ShapeDtypeStruct((B, MLA_PAD_WIDTH, S), BF16),
        jax.ShapeDtypeStruct((B, S, MLA_PAD_WIDTH), BF16),
        jax.ShapeDtypeStruct((B, MLA_HEADS * MLA_V_ROWS, S), F32),
        jax.ShapeDtypeStruct((B, S, MLA_WIDTH), F32),
    )
    out_specs = [rows(s.shape[-1]) for s in out_shapes]
    out_specs[4] = cols(MLA_PAD_WIDTH)
    out_specs[6] = cols(MLA_HEADS * MLA_V_ROWS)
    half = MLA_ROPE_DIM // 2
    return pl.pallas_call(
        _inproj_kernel,
        grid=grid,
        in_specs=[rows(D),
                  pl.BlockSpec((1, 3, D), lambda b, i: (b, 0, 0)),
                  whole(norm_w), whole(w_in_b), whole(w_gm), whole(qln), whole(w_uq_t), whole(kvln),
                  whole(w_uk_t), whole(w_uv), whole(gq), whole(gk),
                  cols(half), cols(half)],
        out_specs=out_specs,
        out_shape=out_shapes,
        compiler_params=pltpu.CompilerParams(
            dimension_semantics=("arbitrary", "arbitrary"), vmem_limit_bytes=VMEM_LIMIT),
        name="inproj",
    )(x, mod, norm_w, w_in_b, w_gm, qln, w_uq_t, kvln, w_uk_t, w_uv, gq, gk, cos_t, sin_t)


def _sb_kernel(q_ref, k_ref, v_ref, g_ref, o_ref, acc_ref, carry_ref):
    tq, tk = SB_TQ, SB_TK
    n_heads = SB_PAIRS * HEAD_PAIR
    i = pl.program_id(2)
    lane = lax.broadcasted_iota(jnp.int32, (tq, LANES), 1)
    q_heads = []
    for sub in range(SB_SUBS):
        heads = []
        for p in range(SB_PAIRS):
            q_pair = q_ref[0, sub * tq:(sub + 1) * tq, p * LANES:(p + 1) * LANES]
            for hh in range(HEAD_PAIR):
                heads.append(jnp.where((lane // SB_HEAD_DIM) == hh, q_pair, jnp.zeros_like(q_pair)))
        q_heads.append(heads)

    strict = (lax.broadcasted_iota(jnp.int32, (tk, tq), 0)
              < lax.broadcasted_iota(jnp.int32, (tk, tq), 1))
    l_row = lax.broadcasted_iota(jnp.int32, (tk + SUBLANES, tk), 0)
    l_col = lax.broadcasted_iota(jnp.int32, (tk + SUBLANES, tk), 1)
    later = jnp.where(jnp.logical_or(l_col > l_row, l_row >= tk), 1.0, 0.0).astype(F32)

    acc_ref[...] = jnp.zeros_like(acc_ref)
    carry_ref[...] = jnp.zeros_like(carry_ref)

    def visit(plan, diag_first):
        depth = max(len(tiles) for _, tiles in plan)
        items = [(sub, t, tiles[t], h) for t in range(depth) for sub, tiles in plan
                 if t < len(tiles) for h in range(n_heads)]

        def pair_block(ref, j, h):
            p = h // HEAD_PAIR
            return ref[0, pl.ds(pl.multiple_of(j * tk, tk), tk), p * LANES:(p + 1) * LANES]

        n = len(items)
        zs, log_betas, afters, col_sums = [None] * n, [None] * n, [None] * n, [None] * n
        carries = {(sub, h): carry_ref[sub, h] for sub, _ in plan for h in range(n_heads)}
        outs = {key: None for key in carries}

        def scores(idx):
            sub, _, j, h = items[idx]
            zs[idx] = _dot_nt(pair_block(k_ref, j, h), q_heads[sub][h])

        def drops(idx):
            _, t, _, _ = items[idx]
            z = zs[idx]
            drop = jnp.maximum(jnp.log2(1.0 + jnp.exp2(jnp.minimum(z, SB_SOFTPLUS_CLAMP))), z)
            if diag_first and t == 0:
                drop = jnp.where(strict, drop, 0.0)
            sums = _dot(later, drop)
            afters[idx] = sums[:tk]
            col_sums[idx] = sums[tk:tk + 1]
            log_betas[idx] = z - drop

        def weights(idx):
            sub, t, j, h = items[idx]
            w = jnp.exp2(log_betas[idx] - (afters[idx] + carries[sub, h]))
            if diag_first and t == 0:
                w = jnp.where(strict, w, 0.0)
            pv = _dot_tn(pair_block(v_ref, j, h), w.astype(BF16))
            lo_row = (h % HEAD_PAIR) * SB_HEAD_DIM
            pv = pv[lo_row:lo_row + SB_HEAD_DIM, :]
            outs[sub, h] = pv if outs[sub, h] is None else outs[sub, h] + pv
            carries[sub, h] = carries[sub, h] + col_sums[idx]

        for step in range(n + SB_STAGE_LAG[1]):
            if step < n:
                scores(step)
            if 0 <= step - SB_STAGE_LAG[0] < n:
                drops(step - SB_STAGE_LAG[0])
            if 0 <= step - SB_STAGE_LAG[1] < n:
                weights(step - SB_STAGE_LAG[1])
        alive = []
        for sub, _ in plan:
            dead = None
            for h in range(n_heads):
                acc_ref[sub, h] += outs[sub, h]
                carry_ref[sub, h] = carries[sub, h]
                dead = carries[sub, h] if dead is None else jnp.minimum(dead, carries[sub, h])
            alive.append(jnp.min(dead) < SB_DEAD_LOG2_DROP)
        return alive

    first = SB_SUBS * i

    def usual_step():
        return visit([(sub, [first + sub - t for t in range(SB_FIRST_VISIT)])
                      for sub in range(SB_SUBS)], True)

    def first_step():
        return visit([(sub, [sub - t for t in range(SB_FIRST_VISIT) if sub - t >= 0])
                      for sub in range(SB_SUBS)], True)

    assert SB_SUBS >= SB_FIRST_VISIT - 1
    alive = lax.cond(i > 0, usual_step, first_step)

    for sub in range(SB_SUBS):
        def cond(state):
            j, still_alive = state
            return jnp.logical_and(j >= 0, still_alive)

        def body(state, sub=sub):
            j, _ = state
            return j - 1, visit([(sub, [j])], False)[0]

        lax.while_loop(cond, body, (first + sub - SB_FIRST_VISIT, alive[sub]))

    for sub in range(SB_SUBS):
        rows = slice(sub * tq, (sub + 1) * tq)
        o_t = acc_ref[sub].reshape(n_heads * SB_HEAD_DIM, tq)
        o_ref[0, rows, :] = (o_t.T * _silu(g_ref[0, rows, :])).astype(o_ref.dtype)


def _sb_attention(q, k, v, g):
    B, S, W = q.shape
    rows = SB_SUBS * SB_TQ
    assert S % rows == 0 and SB_TQ == SB_TK
    wb = SB_PAIRS * LANES
    n_heads = SB_PAIRS * HEAD_PAIR
    grid = (B, W // wb, S // rows)
    q_spec = pl.BlockSpec((1, rows, wb), lambda b, p, i: (b, i, p))
    kv_spec = pl.BlockSpec((1, S, wb), lambda b, p, i: (b, 0, p))
    return pl.pallas_call(
        _sb_kernel,
        grid=grid,
        in_specs=[q_spec, kv_spec, kv_spec, q_spec],
        out_specs=q_spec,
        out_shape=jax.ShapeDtypeStruct((B, S, W), BF16),
        scratch_shapes=[pltpu.VMEM((SB_SUBS, n_heads, SB_HEAD_DIM, SB_TQ), F32),
                        pltpu.VMEM((SB_SUBS, n_heads, 1, SB_TQ), F32)],
        compiler_params=pltpu.CompilerParams(
            dimension_semantics=("arbitrary", "arbitrary", "arbitrary"),
            vmem_limit_bytes=VMEM_LIMIT),
        name="sb_attn",
    )(q, k, v, g)


def _mla_kernel(q_ref, qnext_ref, k_ref, v_ref, g_ref, o_ref, acc_ref, m_ref, s_ref, smax_ref):
    tq, tk = MLA_TQ, MLA_TK
    diag_tiles = tq // tk
    i = pl.program_id(2)
    neg = jnp.finfo(F32).min

    acc_ref[...] = jnp.zeros_like(acc_ref)
    m_ref[...] = jnp.full_like(m_ref, neg)

    def score(j, slot, diag=None, queries=q_ref):
        start = pl.multiple_of(j * tk, tk)
        lo = 0 if diag is None else diag * tk
        for hh in range(MLA_GROUP):
            q_t = queries[0, hh * LANES:(hh + 1) * LANES, lo:]
            k_blk = k_ref[0, pl.ds(start, tk), hh * LANES:(hh + 1) * LANES]
            s = _dot(k_blk, q_t)
            s_ref[slot, hh, :, lo:] = s
            if diag is None:
                smax_ref[slot, hh] = jnp.broadcast_to(jnp.max(s, axis=0, keepdims=True), (SUBLANES, tq))

    def consume(j, slot, diag=None):
        start = pl.multiple_of(j * tk, tk)
        lo = 0 if diag is None else diag * tk
        for hh in range(MLA_GROUP):
            v_t = v_ref[0, hh * MLA_V_ROWS:(hh + 1) * MLA_V_ROWS, pl.ds(start, tk)]
            s = s_ref[slot, hh, :, lo:]
            if diag is None:
                tile_max = smax_ref[slot, hh]
            else:
                causal = (lax.broadcasted_iota(jnp.int32, s.shape, 0)
                          <= lax.broadcasted_iota(jnp.int32, s.shape, 1))
                s = jnp.where(causal, s, neg)
                tile_max = jnp.max(s, axis=0, keepdims=True)
            m_old = m_ref[hh, :, lo:]
            m_new = jnp.maximum(m_old, tile_max)
            alpha = jnp.exp2(m_old[:1] - m_new[:1])
            p = jnp.exp2(s - m_new[:1])
            acc_ref[hh, :, lo:] = alpha * acc_ref[hh, :, lo:] + _dot(v_t, p)
            m_ref[hh, :, lo:] = m_new

    base = i * diag_tiles

    def two_tiles(t):
        score(t + 1, 1)
        consume(t, 0)
        score(t + 2, 0)
        consume(t + 1, 1)

    looped = jnp.maximum(base - 2, 0)
    done = 0
    pairs = MLA_UNROLL
    while pairs >= 1:
        def group(jj, carry, pairs=pairs, done=done):
            for r in range(pairs):
                two_tiles(done + 2 * pairs * jj + 2 * r)
            return carry

        n_groups = (looped - done) // (2 * pairs)
        lax.fori_loop(0, n_groups, group, 0)
        done = done + n_groups * 2 * pairs
        pairs //= 2

    def diagonal_block():
        for d in range(diag_tiles):
            if d + 1 < diag_tiles:
                score(base + d + 1, (d + 1) % 2, d + 1)
            consume(base + d, d % 2, d)
            if d == 0:
                score(0, 0, queries=qnext_ref)
        o_t = jnp.concatenate(
            [acc_ref[hh, :MLA_V_DIM] / acc_ref[hh, MLA_V_DIM:MLA_V_DIM + 1] for hh in range(MLA_GROUP)],
            axis=0)
        o_ref[0] = (o_t.T * _silu(g_ref[0])).astype(o_ref.dtype)

    @pl.when(i > 0)
    def _():
        two_tiles(base - 2)
        diagonal_block()

    @pl.when(i == 0)
    def _():
        score(0, 0)
        diagonal_block()


def _mla_attention(q, k, v, g):
    B, S, W = g.shape
    tq = MLA_TQ
    group_rows = MLA_GROUP * MLA_V_ROWS
    group_out = MLA_GROUP * MLA_V_DIM
    assert S % tq == 0 and MLA_TQ % (2 * MLA_TK) == 0 and group_out % LANES == 0
    grid = (B, W // group_out, S // tq)
    return pl.pallas_call(
        _mla_kernel,
        grid=grid,
        in_specs=[pl.BlockSpec((1, MLA_GROUP * LANES, tq), lambda b, p, i: (b, p, i)),
                  pl.BlockSpec((1, MLA_GROUP * LANES, tq),
                               lambda b, p, i: (b, p, jnp.minimum(i + 1, S // tq - 1))),
                  pl.BlockSpec((1, S, MLA_GROUP * LANES), lambda b, p, i: (b, 0, p)),
                  pl.BlockSpec((1, group_rows, S), lambda b, p, i: (b, p, 0)),
                  pl.BlockSpec((1, tq, group_out), lambda b, p, i: (b, i, p))],
        out_specs=pl.BlockSpec((1, tq, group_out), lambda b, p, i: (b, i, p)),
        out_shape=jax.ShapeDtypeStruct((B, S, W), BF16),
        scratch_shapes=[pltpu.VMEM((MLA_GROUP, MLA_V_ROWS, tq), F32),
                        pltpu.VMEM((MLA_GROUP, SUBLANES, tq), F32),
                        pltpu.VMEM((2, MLA_GROUP, MLA_TK, tq), F32),
                        pltpu.VMEM((2, MLA_GROUP, SUBLANES, tq), F32)],
        compiler_params=pltpu.CompilerParams(
            dimension_semantics=("arbitrary", "arbitrary", "arbitrary"),
            vmem_limit_bytes=VMEM_LIMIT),
        name="mla_attn",
    )(q, q, k, v, g)


def _outproj_kernel(msb_ref, mmla_ref, x_ref, mod_ref, w_ref, o_ref):
    y = _dot(msb_ref[0], w_ref[:SB_WIDTH, :]) + _dot(mmla_ref[0], w_ref[SB_WIDTH:, :])
    o_ref[0] = x_ref[0] + mod_ref[0, 2:3, :] * y


def _outproj(mixed_sb, mixed_mla, x, mod, w_out):
    B, S, D = x.shape
    ts = min(OUT_ROW_TILE, S)

    def rows(width):
        return pl.BlockSpec((1, ts, width), lambda b, i: (b, i, 0))

    return pl.pallas_call(
        _outproj_kernel,
        grid=(B, S // ts),
        in_specs=[rows(SB_WIDTH), rows(MLA_WIDTH), rows(D),
                  pl.BlockSpec((1, 3, D), lambda b, i: (b, 0, 0)),
                  pl.BlockSpec(w_out.shape, lambda b, i: (0, 0))],
        out_specs=rows(D),
        out_shape=jax.ShapeDtypeStruct((B, S, D), x.dtype),
        compiler_params=pltpu.CompilerParams(
            dimension_semantics=("arbitrary", "arbitrary"), vmem_limit_bytes=VMEM_LIMIT),
        name="outproj",
    )(mixed_sb, mixed_mla, x, mod, w_out)


def _layer(x, c, tables, w_ada, b_ada, norm_w, w_in, q_lora_norm, w_uq, kv_lora_norm, w_ukv,
           q_head_norm, k_head_norm, w_out):
    B, S, D = x.shape
    mod = _adaln(c.T, w_ada, b_ada[None, :]).reshape(B, 3, D)

    ts = min(ROW_TILE, S)
    w_in_b = w_in.astype(BF16)
    w_gm = w_in[:, _C_GMLA:].astype(BF16)
    w_uq_t = w_uq.T.astype(BF16)
    w_ukv_h = w_ukv.reshape(KV_LORA_RANK, MLA_HEADS, MLA_NOPE_DIM + MLA_V_DIM)
    w_uk_t = w_ukv_h[:, :, :MLA_NOPE_DIM].reshape(KV_LORA_RANK, -1).T.astype(BF16)
    w_uv = w_ukv_h[:, :, MLA_NOPE_DIM:].reshape(KV_LORA_RANK, MLA_WIDTH).T.astype(BF16)
    gq = jnp.broadcast_to(q_head_norm[:, None], (MLA_QK_DIM, ts))
    gk = jnp.broadcast_to(k_head_norm[:, None], (MLA_QK_DIM, ts))

    (q_sb, k_sb, v_sb, g_sb, q_m, k_m, v_m, g_m) = _inproj(
        x, mod, norm_w[None, :], w_in_b, w_gm, q_lora_norm[None, :], w_uq_t, kv_lora_norm[None, :],
        w_uk_t, w_uv, gq, gk, *tables)

    mixed_sb = _sb_attention(q_sb, k_sb, v_sb, g_sb)
    mixed_mla = _mla_attention(q_m, k_m, v_m, g_m)
    return _outproj(mixed_sb, mixed_mla, x, mod, w_out.astype(BF16))


def _rope_tables(positions, dtype):
    inv_freq = ROPE_THETA ** (-jnp.arange(0, MLA_ROPE_DIM, 2, dtype=F32) / MLA_ROPE_DIM)
    ang = inv_freq[None, :, None] * positions.astype(F32)[:, None, :]
    return jnp.cos(ang).astype(dtype), jnp.sin(ang).astype(dtype)


def kernel(x, c, positions, w_ada, b_ada, norm_w, w_in, q_lora_norm, w_uq, kv_lora_norm, w_ukv,
           q_head_norm, k_head_norm, w_out):
    tables = _rope_tables(positions, x.dtype)
    for l in range(w_ada.shape[0]):
        x = _layer(x, c, tables, w_ada[l], b_ada[l], norm_w[l], w_in[l], q_lora_norm[l], w_uq[l],
                   kv_lora_norm[l], w_ukv[l], q_head_norm[l], k_head_norm[l], w_out[l])
    return x
```

```python
import math

import jax
import jax.numpy as jnp
from jax import lax
from jax.experimental import pallas as pl
from jax.experimental.pallas import tpu as pltpu

F32 = jnp.float32
BF16 = jnp.bfloat16

SB_HEADS = 8
SB_HEAD_DIM = 64
SB_WIDTH = SB_HEADS * SB_HEAD_DIM
MLA_HEADS = 8
MLA_NOPE_DIM = 64
MLA_ROPE_DIM = 32
MLA_QK_DIM = MLA_NOPE_DIM + MLA_ROPE_DIM
MLA_V_DIM = 64
MLA_WIDTH = MLA_HEADS * MLA_V_DIM
MLA_V_ONES = 8
MLA_V_ROWS = MLA_V_DIM + MLA_V_ONES
Q_LORA_RANK = 384
KV_LORA_RANK = 256
ROPE_THETA = 10000.0
EPS = 1e-6

LANES = 128
SUBLANES = 8
HEAD_PAIR = 2
MLA_PAD_WIDTH = MLA_HEADS * LANES

_C_QSB = 0
_C_KSB = _C_QSB + SB_WIDTH
_C_VSB = _C_KSB + SB_WIDTH
_C_GSB = _C_VSB + SB_WIDTH
_C_CQ = _C_GSB + SB_WIDTH
_C_CKV = _C_CQ + Q_LORA_RANK
_C_KR = _C_CKV + KV_LORA_RANK
_C_GMLA = _C_KR + MLA_ROPE_DIM

ADA_COL_TILE = 512
PACK_ROW_TILE = 128
ROW_TILE = 512
OUT_ROW_TILE = 2048
SB_TQ = 256
SB_TK = 256
SB_PAIRS = 4
SB_SUBS = 2
SB_FIRST_VISIT = 2
SB_STAGE_LAG = (2, 4)
MLA_GROUP = 2
MLA_UNROLL = 4
MLA_TQ = 512
MLA_TK = 256
VMEM_LIMIT = 56 * 1024 * 1024

SB_DEAD_LOG2_DROP = 160.0
SB_SOFTPLUS_CLAMP = 100.0
SB_Q_SCALE = math.log2(math.e) / math.sqrt(SB_HEAD_DIM)
MLA_Q_SCALE = math.log2(math.e) / math.sqrt(MLA_QK_DIM)


def _silu(g):
    return g * (1.0 / (1.0 + jnp.exp(-g)))


def _dot(a, b):
    return jnp.dot(a, b, preferred_element_type=F32)


def _dot_nt(a, b):
    return lax.dot_general(a, b, (((1,), (1,)), ((), ())), preferred_element_type=F32)


def _dot_tn(a, b):
    return lax.dot_general(a, b, (((0,), (0,)), ((), ())), preferred_element_type=F32)


def _adaln_kernel(ct_ref, w_ref, b_ref, o_ref):
    s_t = _silu(ct_ref[...])
    w = w_ref[...]
    rows = [jnp.sum(w * s_t[:, b:b + 1], axis=0, keepdims=True) for b in range(s_t.shape[1])]
    o_ref[...] = jnp.concatenate(rows, axis=0) + b_ref[...]


def _adaln(c_t, w_ada, b_ada):
    d, rows = c_t.shape
    n = w_ada.shape[1]
    bn = ADA_COL_TILE
    return pl.pallas_call(
        _adaln_kernel,
        grid=(n // bn,),
        in_specs=[pl.BlockSpec((d, rows), lambda j: (0, 0)),
                  pl.BlockSpec((d, bn), lambda j: (0, j)),
                  pl.BlockSpec((1, bn), lambda j: (0, j))],
        out_specs=pl.BlockSpec((rows, bn), lambda j: (0, j)),
        out_shape=jax.ShapeDtypeStruct((rows, n), F32),
        name="adaln",
    )(c_t, w_ada, b_ada)


def _pack_w_in_kernel(w_ref, wb_ref, wgm_ref):
    w = w_ref[...]
    wb_ref[...] = w.astype(BF16)
    wgm_ref[...] = w[:, _C_GMLA:].astype(BF16)


def _pack_w_in(w_in):
    d, n = w_in.shape
    rows = PACK_ROW_TILE
    return pl.pallas_call(
        _pack_w_in_kernel,
        grid=(d // rows,),
        in_specs=[pl.BlockSpec((rows, n), lambda i: (i, 0))],
        out_specs=[pl.BlockSpec((rows, n), lambda i: (i, 0)),
                   pl.BlockSpec((rows, n - _C_GMLA), lambda i: (i, 0))],
        out_shape=(jax.ShapeDtypeStruct((d, n), BF16), jax.ShapeDtypeStruct((d, n - _C_GMLA), BF16)),
        name="pack_w_in",
    )(w_in)


def _rope_t(x1, x2, cos, sin):
    return x1 * cos - x2 * sin, x2 * cos + x1 * sin


def _inproj_kernel(x_ref, mod_ref, nw_ref, win_ref, wgm_ref, qln_ref, wuqt_ref, kvln_ref, wukt_ref, wuvt_ref,
                   gq_ref, gk_ref, cos_ref, sin_ref,
                   qsb_ref, ksb_ref, vsb_ref, gsb_ref, qmt_ref, km_ref, vmt_ref, gm_ref):
    half = MLA_ROPE_DIM // 2
    n0, n1, n2 = MLA_NOPE_DIM, MLA_NOPE_DIM + half, MLA_QK_DIM
    x = x_ref[0]
    ts = x.shape[0]
    y = x * lax.rsqrt(jnp.mean(x * x, axis=-1, keepdims=True) + EPS) * nw_ref[...]
    h = (y * (1.0 + mod_ref[0, 1:2, :]) + mod_ref[0, 0:1, :]).astype(BF16)

    def proj(lo, hi):
        return _dot(h, win_ref[:, lo:hi])

    def rms(t, w_ref):
        return t * lax.rsqrt(jnp.mean(t * t, axis=-1, keepdims=True) + EPS) * w_ref[...]

    latents = proj(_C_CQ, _C_KR + LANES)
    cq = latents[:, :Q_LORA_RANK]
    ckv = latents[:, Q_LORA_RANK:Q_LORA_RANK + KV_LORA_RANK]
    kr = latents[:, Q_LORA_RANK + KV_LORA_RANK:]
    qsb_ref[0] = (proj(_C_QSB, _C_KSB) * SB_Q_SCALE).astype(BF16)
    ksb_ref[0] = proj(_C_KSB, _C_VSB).astype(BF16)

    cqn = rms(cq, qln_ref).astype(BF16)
    ckvn = rms(ckv, kvln_ref).astype(BF16)
    q_t = _dot_nt(wuqt_ref[...], cqn)
    k_t = _dot_nt(wukt_ref[...], ckvn)
    v_t = _dot_nt(wuvt_ref[...], ckvn)
    vsb_ref[0] = proj(_C_VSB, _C_GSB).astype(BF16)
    gsb_ref[0] = proj(_C_GSB, _C_CQ)
    gm_ref[0] = _dot(h, wgm_ref[...])

    cos, sin = cos_ref[0], sin_ref[0]
    pad_rows = jnp.zeros((LANES - MLA_QK_DIM, ts), F32)

    gq = gq_ref[...]
    for hd in range(MLA_HEADS):
        blk = q_t[hd * MLA_QK_DIM:(hd + 1) * MLA_QK_DIM]
        inv = lax.rsqrt(jnp.sum(blk * blk, axis=0, keepdims=True) * (1.0 / MLA_QK_DIM) + EPS)
        g = blk * gq
        r1, r2 = _rope_t(g[n0:n1], g[n1:n2], cos, sin)
        out = jnp.concatenate([g[:n0], r1, r2, pad_rows], axis=0) * (inv * MLA_Q_SCALE)
        qmt_ref[0, hd * LANES:(hd + 1) * LANES, :] = out.astype(BF16)

    ones_rows = jnp.ones((MLA_V_ONES, ts), F32)
    v_rows = []
    for hd in range(MLA_HEADS):
        v_rows += [v_t[hd * MLA_V_DIM:(hd + 1) * MLA_V_DIM], ones_rows]
    vmt_ref[0] = jnp.concatenate(v_rows, axis=0)

    gk = gk_ref[...]
    kr_t = kr.T
    x1, x2 = kr_t[:half], kr_t[half:MLA_ROPE_DIM]
    kr_ssq = jnp.sum(x1 * x1 + x2 * x2, axis=0, keepdims=True)
    r1, r2 = _rope_t(x1 * gk[n0:n1], x2 * gk[n1:n2], cos, sin)
    for hd in range(MLA_HEADS):
        kn = k_t[hd * n0:(hd + 1) * n0]
        ssq = jnp.sum(kn * kn, axis=0, keepdims=True) + kr_ssq
        inv = lax.rsqrt(ssq * (1.0 / MLA_QK_DIM) + EPS)
        out = jnp.concatenate([kn * gk[:n0], r1, r2, pad_rows], axis=0) * inv
        km_ref[0, :, hd * LANES:(hd + 1) * LANES] = out.astype(BF16).T


def _inproj(x, mod, norm_w, w_in_b, w_gm, qln, w_uq_t, kvln, w_uk_t, w_uv, gq, gk, cos_t, sin_t):
    B, S, D = x.shape
    ts = min(ROW_TILE, S)
    grid = (B, S // ts)

    def whole(a):
        return pl.BlockSpec(a.shape, lambda b, i: (0,) * a.ndim)

    def rows(width):
        return pl.BlockSpec((1, ts, width), lambda b, i: (b, i, 0))

    def cols(height):
        return pl.BlockSpec((1, height, ts), lambda b, i: (b, 0, i))

    out_shapes = (
        jax.ShapeDtypeStruct((B, S, SB_WIDTH), BF16),
        jax.ShapeDtypeStruct((B, S, SB_WIDTH), BF16),
        jax.ShapeDtypeStruct((B, S, SB_WIDTH), BF16),
        jax.ShapeDtypeStruct((B, S, SB_WIDTH), F32),
        jax.ShapeDtypeStruct((B, MLA_PAD_WIDTH, S), BF16),
        jax.ShapeDtypeStruct((B, S, MLA_PAD_WIDTH), BF16),
        jax.ShapeDtypeStruct((B, MLA_HEADS * MLA_V_ROWS, S), F32),
        jax.ShapeDtypeStruct((B, S, MLA_WIDTH), F32),
    )
    out_specs = [rows(s.shape[-1]) for s in out_shapes]
    out_specs[4] = cols(MLA_PAD_WIDTH)
    out_specs[6] = cols(MLA_HEADS * MLA_V_ROWS)
    half = MLA_ROPE_DIM // 2
    return pl.pallas_call(
        _inproj_kernel,
        grid=grid,
        in_specs=[rows(D),
                  pl.BlockSpec((1, 3, D), lambda b, i: (b, 0, 0)),
                  whole(norm_w), whole(w_in_b), whole(w_gm), whole(qln), whole(w_uq_t), whole(kvln),
                  whole(w_uk_t), whole(w_uv), whole(gq), whole(gk),
                  cols(half), cols(half)],
        out_specs=out_specs,
        out_shape=out_shapes,
        compiler_params=pltpu.CompilerParams(
            dimension_semantics=("arbitrary", "arbitrary"), vmem_limit_bytes=VMEM_LIMIT),
        name="inproj",
    )(x, mod, norm_w, w_in_b, w_gm, qln, w_uq_t, kvln, w_uk_t, w_uv, gq, gk, cos_t, sin_t)


def _sb_kernel(q_ref, k_ref, v_ref, g_ref, o_ref, acc_ref, carry_ref):
    tq, tk = SB_TQ, SB_TK
    n_heads = SB_PAIRS * HEAD_PAIR
    i = pl.program_id(2)
    lane = lax.broadcasted_iota(jnp.int32, (tq, LANES), 1)
    q_heads = []
    for sub in range(SB_SUBS):
        heads = []
        for p in range(SB_PAIRS):
            q_pair = q_ref[0, sub * tq:(sub + 1) * tq, p * LANES:(p + 1) * LANES]
            for hh in range(HEAD_PAIR):
                heads.append(jnp.where((lane // SB_HEAD_DIM) == hh, q_pair, jnp.zeros_like(q_pair)))
        q_heads.append(heads)

    strict = (lax.broadcasted_iota(jnp.int32, (tk, tq), 0)
              < lax.broadcasted_iota(jnp.int32, (tk, tq), 1))
    l_row = lax.broadcasted_iota(jnp.int32, (tk + SUBLANES, tk), 0)
    l_col = lax.broadcasted_iota(jnp.int32, (tk + SUBLANES, tk), 1)
    later = jnp.where(jnp.logical_or(l_col > l_row, l_row >= tk), 1.0, 0.0).astype(F32)

    acc_ref[...] = jnp.zeros_like(acc_ref)
    carry_ref[...] = jnp.zeros_like(carry_ref)

    def visit(plan, diag_first):
        depth = max(len(tiles) for _, tiles in plan)
        items = [(sub, t, tiles[t], h) for t in range(depth) for sub, tiles in plan
                 if t < len(tiles) for h in range(n_heads)]

        def pair_block(ref, j, h):
            p = h // HEAD_PAIR
            return ref[0, pl.ds(pl.multiple_of(j * tk, tk), tk), p * LANES:(p + 1) * LANES]

        n = len(items)
        zs, log_betas, afters, col_sums = [None] * n, [None] * n, [None] * n, [None] * n
        carries = {(sub, h): carry_ref[sub, h] for sub, _ in plan for h in range(n_heads)}
        outs = {key: None for key in carries}

        def scores(idx):
            sub, _, j, h = items[idx]
            zs[idx] = _dot_nt(pair_block(k_ref, j, h), q_heads[sub][h])

        def drops(idx):
            _, t, _, _ = items[idx]
            z = zs[idx]
            drop = jnp.maximum(jnp.log2(1.0 + jnp.exp2(jnp.minimum(z, SB_SOFTPLUS_CLAMP))), z)
            if diag_first and t == 0:
                drop = jnp.where(strict, drop, 0.0)
            sums = _dot(later, drop)
            afters[idx] = sums[:tk]
            col_sums[idx] = sums[tk:tk + 1]
            log_betas[idx] = z - drop

        def weights(idx):
            sub, t, j, h = items[idx]
            w = jnp.exp2(log_betas[idx] - (afters[idx] + carries[sub, h]))
            if diag_first and t == 0:
                w = jnp.where(strict, w, 0.0)
            pv = _dot_tn(pair_block(v_ref, j, h), w.astype(BF16))
            lo_row = (h % HEAD_PAIR) * SB_HEAD_DIM
            pv = pv[lo_row:lo_row + SB_HEAD_DIM, :]
            outs[sub, h] = pv if outs[sub, h] is None else outs[sub, h] + pv
            carries[sub, h] = carries[sub, h] + col_sums[idx]

        for step in range(n + SB_STAGE_LAG[1]):
            if step < n:
                scores(step)
            if 0 <= step - SB_STAGE_LAG[0] < n:
                drops(step - SB_STAGE_LAG[0])
            if 0 <= step - SB_STAGE_LAG[1] < n:
                weights(step - SB_STAGE_LAG[1])
        alive = []
        for sub, _ in plan:
            dead = None
            for h in range(n_heads):
                acc_ref[sub, h] += outs[sub, h]
                carry_ref[sub, h] = carries[sub, h]
                dead = carries[sub, h] if dead is None else jnp.minimum(dead, carries[sub, h])
            alive.append(jnp.min(dead) < SB_DEAD_LOG2_DROP)
        return alive

    first = SB_SUBS * i

    def usual_step():
        return visit([(sub, [first + sub - t for t in range(SB_FIRST_VISIT)])
                      for sub in range(SB_SUBS)], True)

    def first_step():
        return visit([(sub, [sub - t for t in range(SB_FIRST_VISIT) if sub - t >= 0])
                      for sub in range(SB_SUBS)], True)

    assert SB_SUBS >= SB_FIRST_VISIT - 1
    alive = lax.cond(i > 0, usual_step, first_step)

    for sub in range(SB_SUBS):
        def cond(state):
            j, still_alive = state
            return jnp.logical_and(j >= 0, still_alive)

        def body(state, sub=sub):
            j, _ = state
            return j - 1, visit([(sub, [j])], False)[0]

        lax.while_loop(cond, body, (first + sub - SB_FIRST_VISIT, alive[sub]))

    for sub in range(SB_SUBS):
        rows = slice(sub * tq, (sub + 1) * tq)
        o_t = acc_ref[sub].reshape(n_heads * SB_HEAD_DIM, tq)
        o_ref[0, rows, :] = (o_t.T * _silu(g_ref[0, rows, :])).astype(o_ref.dtype)


def _sb_attention(q, k, v, g):
    B, S, W = q.shape
    rows = SB_SUBS * SB_TQ
    assert S % rows == 0 and SB_TQ == SB_TK
    wb = SB_PAIRS * LANES
    n_heads = SB_PAIRS * HEAD_PAIR
    grid = (B, W // wb, S // rows)
    q_spec = pl.BlockSpec((1, rows, wb), lambda b, p, i: (b, i, p))
    kv_spec = pl.BlockSpec((1, S, wb), lambda b, p, i: (b, 0, p))
    return pl.pallas_call(
        _sb_kernel,
        grid=grid,
        in_specs=[q_spec, kv_spec, kv_spec, q_spec],
        out_specs=q_spec,
        out_shape=jax.ShapeDtypeStruct((B, S, W), BF16),
        scratch_shapes=[pltpu.VMEM((SB_SUBS, n_heads, SB_HEAD_DIM, SB_TQ), F32),
                        pltpu.VMEM((SB_SUBS, n_heads, 1, SB_TQ), F32)],
        compiler_params=pltpu.CompilerParams(
            dimension_semantics=("arbitrary", "arbitrary", "arbitrary"),
            vmem_limit_bytes=VMEM_LIMIT),
        name="sb_attn",
    )(q, k, v, g)


def _mla_kernel(q_ref, qnext_ref, k_ref, v_ref, g_ref, o_ref, acc_ref, m_ref, s_ref, smax_ref):
    tq, tk = MLA_TQ, MLA_TK
    diag_tiles = tq // tk
    i = pl.program_id(2)
    neg = jnp.finfo(F32).min

    acc_ref[...] = jnp.zeros_like(acc_ref)
    m_ref[...] = jnp.full_like(m_ref, neg)

    def score(j, slot, diag=None, queries=q_ref):
        start = pl.multiple_of(j * tk, tk)
        lo = 0 if diag is None else diag * tk
        for hh in range(MLA_GROUP):
            q_t = queries[0, hh * LANES:(hh + 1) * LANES, lo:]
            k_blk = k_ref[0, pl.ds(start, tk), hh * LANES:(hh + 1) * LANES]
            s = _dot(k_blk, q_t)
            s_ref[slot, hh, :, lo:] = s
            if diag is None:
                smax_ref[slot, hh] = jnp.broadcast_to(jnp.max(s, axis=0, keepdims=True), (SUBLANES, tq))

    def consume(j, slot, diag=None):
        start = pl.multiple_of(j * tk, tk)
        lo = 0 if diag is None else diag * tk
        for hh in range(MLA_GROUP):
            v_t = v_ref[0, hh * MLA_V_ROWS:(hh + 1) * MLA_V_ROWS, pl.ds(start, tk)]
            s = s_ref[slot, hh, :, lo:]
            if diag is None:
                tile_max = smax_ref[slot, hh]
            else:
                causal = (lax.broadcasted_iota(jnp.int32, s.shape, 0)
                          <= lax.broadcasted_iota(jnp.int32, s.shape, 1))
                s = jnp.where(causal, s, neg)
                tile_max = jnp.max(s, axis=0, keepdims=True)
            m_old = m_ref[hh, :, lo:]
            m_new = jnp.maximum(m_old, tile_max)
            alpha = jnp.exp2(m_old[:1] - m_new[:1])
            p = jnp.exp2(s - m_new[:1])
            acc_ref[hh, :, lo:] = alpha * acc_ref[hh, :, lo:] + _dot(v_t, p)
            m_ref[hh, :, lo:] = m_new

    base = i * diag_tiles

    def two_tiles(t):
        score(t + 1, 1)
        consume(t, 0)
        score(t + 2, 0)
        consume(t + 1, 1)

    looped = jnp.maximum(base - 2, 0)
    done = 0
    pairs = MLA_UNROLL
    while pairs >= 1:
        def group(jj, carry, pairs=pairs, done=done):
            for r in range(pairs):
                two_tiles(done + 2 * pairs * jj + 2 * r)
            return carry

        n_groups = (looped - done) // (2 * pairs)
        lax.fori_loop(0, n_groups, group, 0)
        done = done + n_groups * 2 * pairs
        pairs //= 2

    def diagonal_block():
        for d in range(diag_tiles):
            if d + 1 < diag_tiles:
                score(base + d + 1, (d + 1) % 2, d + 1)
            consume(base + d, d % 2, d)
            if d == 0:
                score(0, 0, queries=qnext_ref)
        o_t = jnp.concatenate(
            [acc_ref[hh, :MLA_V_DIM] / acc_ref[hh, MLA_V_DIM:MLA_V_DIM + 1] for hh in range(MLA_GROUP)],
            axis=0)
        o_ref[0] = (o_t.T * _silu(g_ref[0])).astype(o_ref.dtype)

    @pl.when(i > 0)
    def _():
        two_tiles(base - 2)
        diagonal_block()

    @pl.when(i == 0)
    def _():
        score(0, 0)
        diagonal_block()


def _mla_attention(q, k, v, g):
    B, S, W = g.shape
    tq = MLA_TQ
    group_rows = MLA_GROUP * MLA_V_ROWS
    group_out = MLA_GROUP * MLA_V_DIM
    assert S % tq == 0 and MLA_TQ % (2 * MLA_TK) == 0 and group_out % LANES == 0
    grid = (B, W // group_out, S // tq)
    return pl.pallas_call(
        _mla_kernel,
        grid=grid,
        in_specs=[pl.BlockSpec((1, MLA_GROUP * LANES, tq), lambda b, p, i: (b, p, i)),
                  pl.BlockSpec((1, MLA_GROUP * LANES, tq),
                               lambda b, p, i: (b, p, jnp.minimum(i + 1, S // tq - 1))),
                  pl.BlockSpec((1, S, MLA_GROUP * LANES), lambda b, p, i: (b, 0, p)),
                  pl.BlockSpec((1, group_rows, S), lambda b, p, i: (b, p, 0)),
                  pl.BlockSpec((1, tq, group_out), lambda b, p, i: (b, i, p))],
        out_specs=pl.BlockSpec((1, tq, group_out), lambda b, p, i: (b, i, p)),
        out_shape=jax.ShapeDtypeStruct((B, S, W), BF16),
        scratch_shapes=[pltpu.VMEM((MLA_GROUP, MLA_V_ROWS, tq), F32),
                        pltpu.VMEM((MLA_GROUP, SUBLANES, tq), F32),
                        pltpu.VMEM((2, MLA_GROUP, MLA_TK, tq), F32),
                        pltpu.VMEM((2, MLA_GROUP, SUBLANES, tq), F32)],
        compiler_params=pltpu.CompilerParams(
            dimension_semantics=("arbitrary", "arbitrary", "arbitrary"),
            vmem_limit_bytes=VMEM_LIMIT),
        name="mla_attn",
    )(q, q, k, v, g)


def _outproj_kernel(msb_ref, mmla_ref, x_ref, mod_ref, w_ref, o_ref):
    y = _dot(msb_ref[0], w_ref[:SB_WIDTH, :]) + _dot(mmla_ref[0], w_ref[SB_WIDTH:, :])
    o_ref[0] = x_ref[0] + mod_ref[0, 2:3, :] * y


def _outproj(mixed_sb, mixed_mla, x, mod, w_out):
    B, S, D = x.shape
    ts = min(OUT_ROW_TILE, S)

    def rows(width):
        return pl.BlockSpec((1, ts, width), lambda b, i: (b, i, 0))

    return pl.pallas_call(
        _outproj_kernel,
        grid=(B, S // ts),
        in_specs=[rows(SB_WIDTH), rows(MLA_WIDTH), rows(D),
                  pl.BlockSpec((1, 3, D), lambda b, i: (b, 0, 0)),
                  pl.BlockSpec(w_out.shape, lambda b, i: (0, 0))],
        out_specs=rows(D),
        out_shape=jax.ShapeDtypeStruct((B, S, D), x.dtype),
        compiler_params=pltpu.CompilerParams(
            dimension_semantics=("arbitrary", "arbitrary"), vmem_limit_bytes=VMEM_LIMIT),
        name="outproj",
    )(mixed_sb, mixed_mla, x, mod, w_out)


def _layer(x, c, tables, w_ada, b_ada, norm_w, w_in, q_lora_norm, w_uq, kv_lora_norm, w_ukv,
           q_head_norm, k_head_norm, w_out):
    B, S, D = x.shape
    mod = _adaln(c.T, w_ada, b_ada[None, :]).reshape(B, 3, D)

    ts = min(ROW_TILE, S)
    w_in_b, w_gm = _pack_w_in(w_in)
    w_uq_t = w_uq.T.astype(BF16)
    w_ukv_h = w_ukv.reshape(KV_LORA_RANK, MLA_HEADS, MLA_NOPE_DIM + MLA_V_DIM)
    w_uk_t = w_ukv_h[:, :, :MLA_NOPE_DIM].reshape(KV_LORA_RANK, -1).T.astype(BF16)
    w_uv = w_ukv_h[:, :, MLA_NOPE_DIM:].reshape(KV_LORA_RANK, MLA_WIDTH).T.astype(BF16)
    gq = jnp.broadcast_to(q_head_norm[:, None], (MLA_QK_DIM, ts))
    gk = jnp.broadcast_to(k_head_norm[:, None], (MLA_QK_DIM, ts))

    (q_sb, k_sb, v_sb, g_sb, q_m, k_m, v_m, g_m) = _inproj(
        x, mod, norm_w[None, :], w_in_b, w_gm, q_lora_norm[None, :], w_uq_t, kv_lora_norm[None, :],
        w_uk_t, w_uv, gq, gk, *tables)

    mixed_sb = _sb_attention(q_sb, k_sb, v_sb, g_sb)
    mixed_mla = _mla_attention(q_m, k_m, v_m, g_m)
    return _outproj(mixed_sb, mixed_mla, x, mod, w_out.astype(BF16))


def _rope_tables(positions, dtype):
    inv_freq = ROPE_THETA ** (-jnp.arange(0, MLA_ROPE_DIM, 2, dtype=F32) / MLA_ROPE_DIM)
    ang = inv_freq[None, :, None] * positions.astype(F32)[:, None, :]
    return jnp.cos(ang).astype(dtype), jnp.sin(ang).astype(dtype)


def kernel(x, c, positions, w_ada, b_ada, norm_w, w_in, q_lora_norm, w_uq, kv_lora_norm, w_ukv,
           q_head_norm, k_head_norm, w_out):
    tables = _rope_tables(positions, x.dtype)
    for l in range(w_ada.shape[0]):
        x = _layer(x, c, tables, w_ada[l], b_ada[l], norm_w[l], w_in[l], q_lora_norm[l], w_uq[l],
                   kv_lora_norm[l], w_ukv[l], q_head_norm[l], k_head_norm[l], w_out[l])
    return x
```

```python
import math

import jax
import jax.numpy as jnp
from jax import lax
from jax.experimental import pallas as pl
from jax.experimental.pallas import tpu as pltpu

F32 = jnp.float32
BF16 = jnp.bfloat16

SB_HEADS = 8
SB_HEAD_DIM = 64
SB_WIDTH = SB_HEADS * SB_HEAD_DIM
MLA_HEADS = 8
MLA_NOPE_DIM = 64
MLA_ROPE_DIM = 32
MLA_QK_DIM = MLA_NOPE_DIM + MLA_ROPE_DIM
MLA_V_DIM = 64
MLA_WIDTH = MLA_HEADS * MLA_V_DIM
MLA_V_ONES = 8
MLA_V_ROWS = MLA_V_DIM + MLA_V_ONES
Q_LORA_RANK = 384
KV_LORA_RANK = 256
ROPE_THETA = 10000.0
EPS = 1e-6

LANES = 128
SUBLANES = 8
HEAD_PAIR = 2
MLA_PAD_WIDTH = MLA_HEADS * LANES

_C_QSB = 0
_C_KSB = _C_QSB + SB_WIDTH
_C_VSB = _C_KSB + SB_WIDTH
_C_GSB = _C_VSB + SB_WIDTH
_C_CQ = _C_GSB + SB_WIDTH
_C_CKV = _C_CQ + Q_LORA_RANK
_C_KR = _C_CKV + KV_LORA_RANK
_C_GMLA = _C_KR + MLA_ROPE_DIM

ADA_COL_TILE = 512
PACK_ROW_TILE = 128
ROW_TILE = 512
OUT_ROW_TILE = 2048
SB_TQ = 256
SB_TK = 256
SB_PAIRS = 4
SB_SUBS = 2
SB_FIRST_VISIT = 2
SB_STAGE_LAG = (2, 4)
MLA_GROUP = 2
MLA_UNROLL = 4
MLA_TQ = 512
MLA_TK = 256
VMEM_LIMIT = 56 * 1024 * 1024

SB_DEAD_LOG2_DROP = 160.0
SB_SOFTPLUS_CLAMP = 100.0
SB_Q_SCALE = math.log2(math.e) / math.sqrt(SB_HEAD_DIM)
MLA_Q_SCALE = math.log2(math.e) / math.sqrt(MLA_QK_DIM)


def _silu(g):
    return g * (1.0 / (1.0 + jnp.exp(-g)))


def _dot(a, b):
    return jnp.dot(a, b, preferred_element_type=F32)


def _dot_nt(a, b):
    return lax.dot_general(a, b, (((1,), (1,)), ((), ())), preferred_element_type=F32)


def _dot_tn(a, b):
    return lax.dot_general(a, b, (((0,), (0,)), ((), ())), preferred_element_type=F32)


def _adaln_kernel(ct_ref, w_ref, b_ref, o_ref):
    s_t = _silu(ct_ref[...])
    w = w_ref[...]
    rows = [jnp.sum(w * s_t[:, b:b + 1], axis=0, keepdims=True) for b in range(s_t.shape[1])]
    o_ref[...] = jnp.concatenate(rows, axis=0) + b_ref[...]


def _adaln(c_t, w_ada, b_ada):
    d, rows = c_t.shape
    n = w_ada.shape[1]
    bn = ADA_COL_TILE
    return pl.pallas_call(
        _adaln_kernel,
        grid=(n // bn,),
        in_specs=[pl.BlockSpec((d, rows), lambda j: (0, 0)),
                  pl.BlockSpec((d, bn), lambda j: (0, j)),
                  pl.BlockSpec((1, bn), lambda j: (0, j))],
        out_specs=pl.BlockSpec((rows, bn), lambda j: (0, j)),
        out_shape=jax.ShapeDtypeStruct((rows, n), F32),
        name="adaln",
    )(c_t, w_ada, b_ada)


def _pack_w_in_kernel(w_ref, wb_ref, wgm_ref):
    w = w_ref[0]
    wb_ref[...] = w.astype(BF16)
    wgm_ref[...] = w[:, _C_GMLA:].astype(BF16)


def _pack_w_in(w_in, layer):
    _, d, n = w_in.shape
    rows = PACK_ROW_TILE
    return pl.pallas_call(
        _pack_w_in_kernel,
        grid=(d // rows,),
        in_specs=[pl.BlockSpec((1, rows, n), lambda i: (layer, i, 0))],
        out_specs=[pl.BlockSpec((rows, n), lambda i: (i, 0)),
                   pl.BlockSpec((rows, n - _C_GMLA), lambda i: (i, 0))],
        out_shape=(jax.ShapeDtypeStruct((d, n), BF16), jax.ShapeDtypeStruct((d, n - _C_GMLA), BF16)),
        name="pack_w_in",
    )(w_in)


def _rope_t(x1, x2, cos, sin):
    return x1 * cos - x2 * sin, x2 * cos + x1 * sin


def _inproj_kernel(x_ref, mod_ref, nw_ref, win_ref, wgm_ref, qln_ref, wuqt_ref, kvln_ref, wukt_ref, wuvt_ref,
                   gq_ref, gk_ref, cos_ref, sin_ref,
                   qsb_ref, ksb_ref, vsb_ref, gsb_ref, qmt_ref, km_ref, vmt_ref, gm_ref):
    half = MLA_ROPE_DIM // 2
    n0, n1, n2 = MLA_NOPE_DIM, MLA_NOPE_DIM + half, MLA_QK_DIM
    x = x_ref[0]
    ts = x.shape[0]
    y = x * lax.rsqrt(jnp.mean(x * x, axis=-1, keepdims=True) + EPS) * nw_ref[...]
    h = (y * (1.0 + mod_ref[0, 1:2, :]) + mod_ref[0, 0:1, :]).astype(BF16)

    def proj(lo, hi):
        return _dot(h, win_ref[:, lo:hi])

    def rms(t, w_ref):
        return t * lax.rsqrt(jnp.mean(t * t, axis=-1, keepdims=True) + EPS) * w_ref[...]

    latents = proj(_C_CQ, _C_KR + LANES)
    cq = latents[:, :Q_LORA_RANK]
    ckv = latents[:, Q_LORA_RANK:Q_LORA_RANK + KV_LORA_RANK]
    kr = latents[:, Q_LORA_RANK + KV_LORA_RANK:]
    qsb_ref[0] = (proj(_C_QSB, _C_KSB) * SB_Q_SCALE).astype(BF16)
    ksb_ref[0] = proj(_C_KSB, _C_VSB).astype(BF16)

    cqn = rms(cq, qln_ref).astype(BF16)
    ckvn = rms(ckv, kvln_ref).astype(BF16)
    q_t = _dot_nt(wuqt_ref[...], cqn)
    k_t = _dot_nt(wukt_ref[...], ckvn)
    v_t = _dot_nt(wuvt_ref[...], ckvn)
    vsb_ref[0] = proj(_C_VSB, _C_GSB).astype(BF16)
    gsb_ref[0] = proj(_C_GSB, _C_CQ)
    gm_ref[0] = _dot(h, wgm_ref[...])

    cos, sin = cos_ref[0], sin_ref[0]
    pad_rows = jnp.zeros((LANES - MLA_QK_DIM, ts), F32)

    gq = gq_ref[...]
    for hd in range(MLA_HEADS):
        blk = q_t[hd * MLA_QK_DIM:(hd + 1) * MLA_QK_DIM]
        inv = lax.rsqrt(jnp.sum(blk * blk, axis=0, keepdims=True) * (1.0 / MLA_QK_DIM) + EPS)
        g = blk * gq
        r1, r2 = _rope_t(g[n0:n1], g[n1:n2], cos, sin)
        out = jnp.concatenate([g[:n0], r1, r2, pad_rows], axis=0) * (inv * MLA_Q_SCALE)
        qmt_ref[0, hd * LANES:(hd + 1) * LANES, :] = out.astype(BF16)

    ones_rows = jnp.ones((MLA_V_ONES, ts), F32)
    v_rows = []
    for hd in range(MLA_HEADS):
        v_rows += [v_t[hd * MLA_V_DIM:(hd + 1) * MLA_V_DIM], ones_rows]
    vmt_ref[0] = jnp.concatenate(v_rows, axis=0)

    gk = gk_ref[...]
    kr_t = kr.T
    x1, x2 = kr_t[:half], kr_t[half:MLA_ROPE_DIM]
    kr_ssq = jnp.sum(x1 * x1 + x2 * x2, axis=0, keepdims=True)
    r1, r2 = _rope_t(x1 * gk[n0:n1], x2 * gk[n1:n2], cos, sin)
    for hd in range(MLA_HEADS):
        kn = k_t[hd * n0:(hd + 1) * n0]
        ssq = jnp.sum(kn * kn, axis=0, keepdims=True) + kr_ssq
        inv = lax.rsqrt(ssq * (1.0 / MLA_QK_DIM) + EPS)
        out = jnp.concatenate([kn * gk[:n0], r1, r2, pad_rows], axis=0) * inv
        km_ref[0, :, hd * LANES:(hd + 1) * LANES] = out.astype(BF16).T


def _inproj(x, mod, norm_w, w_in_b, w_gm, qln, w_uq_t, kvln, w_uk_t, w_uv, gq, gk, cos_t, sin_t):
    B, S, D = x.shape
    ts = min(ROW_TILE, S)
    grid = (B, S // ts)

    def whole(a):
        return pl.BlockSpec(a.shape, lambda b, i: (0,) * a.ndim)

    def rows(width):
        return pl.BlockSpec((1, ts, width), lambda b, i: (b, i, 0))

    def cols(height):
        return pl.BlockSpec((1, height, ts), lambda b, i: (b, 0, i))

    out_shapes = (
        jax.ShapeDtypeStruct((B, S, SB_WIDTH), BF16),
        jax.ShapeDtypeStruct((B, S, SB_WIDTH), BF16),
        jax.ShapeDtypeStruct((B, S, SB_WIDTH), BF16),
        jax.ShapeDtypeStruct((B, S, SB_WIDTH), F32),
        jax.ShapeDtypeStruct((B, MLA_PAD_WIDTH, S), BF16),
        jax.ShapeDtypeStruct((B, S, MLA_PAD_WIDTH), BF16),
        jax.ShapeDtypeStruct((B, MLA_HEADS * MLA_V_ROWS, S), F32),
        jax.ShapeDtypeStruct((B, S, MLA_WIDTH), F32),
    )
    out_specs = [rows(s.shape[-1]) for s in out_shapes]
    out_specs[4] = cols(MLA_PAD_WIDTH)
    out_specs[6] = cols(MLA_HEADS * MLA_V_ROWS)
    half = MLA_ROPE_DIM // 2
    return pl.pallas_call(
        _inproj_kernel,
        grid=grid,
        in_specs=[rows(D),
                  pl.BlockSpec((1, 3, D), lambda b, i: (b, 0, 0)),
                  whole(norm_w), whole(w_in_b), whole(w_gm), whole(qln), whole(w_uq_t), whole(kvln),
                  whole(w_uk_t), whole(w_uv), whole(gq), whole(gk),
                  cols(half), cols(half)],
        out_specs=out_specs,
        out_shape=out_shapes,
        compiler_params=pltpu.CompilerParams(
            dimension_semantics=("arbitrary", "arbitrary"), vmem_limit_bytes=VMEM_LIMIT),
        name="inproj",
    )(x, mod, norm_w, w_in_b, w_gm, qln, w_uq_t, kvln, w_uk_t, w_uv, gq, gk, cos_t, sin_t)


def _sb_kernel(q_ref, k_ref, v_ref, g_ref, o_ref, acc_ref, carry_ref):
    tq, tk = SB_TQ, SB_TK
    n_heads = SB_PAIRS * HEAD_PAIR
    i = pl.program_id(2)
    lane = lax.broadcasted_iota(jnp.int32, (tq, LANES), 1)
    q_heads = []
    for sub in range(SB_SUBS):
        heads = []
        for p in range(SB_PAIRS):
            q_pair = q_ref[0, sub * tq:(sub + 1) * tq, p * LANES:(p + 1) * LANES]
            for hh in range(HEAD_PAIR):
                heads.append(jnp.where((lane // SB_HEAD_DIM) == hh, q_pair, jnp.zeros_like(q_pair)))
        q_heads.append(heads)

    strict = (lax.broadcasted_iota(jnp.int32, (tk, tq), 0)
              < lax.broadcasted_iota(jnp.int32, (tk, tq), 1))
    l_row = lax.broadcasted_iota(jnp.int32, (tk + SUBLANES, tk), 0)
    l_col = lax.broadcasted_iota(jnp.int32, (tk + SUBLANES, tk), 1)
    later = jnp.where(jnp.logical_or(l_col > l_row, l_row >= tk), 1.0, 0.0).astype(F32)

    acc_ref[...] = jnp.zeros_like(acc_ref)
    carry_ref[...] = jnp.zeros_like(carry_ref)

    def visit(plan, diag_first):
        depth = max(len(tiles) for _, tiles in plan)
        items = [(sub, t, tiles[t], h) for t in range(depth) for sub, tiles in plan
                 if t < len(tiles) for h in range(n_heads)]

        def pair_block(ref, j, h):
            p = h // HEAD_PAIR
            return ref[0, pl.ds(pl.multiple_of(j * tk, tk), tk), p * LANES:(p + 1) * LANES]

        n = len(items)
        zs, log_betas, afters, col_sums = [None] * n, [None] * n, [None] * n, [None] * n
        carries = {(sub, h): carry_ref[sub, h] for sub, _ in plan for h in range(n_heads)}
        outs = {key: None for key in carries}

        def scores(idx):
            sub, _, j, h = items[idx]
            zs[idx] = _dot_nt(pair_block(k_ref, j, h), q_heads[sub][h])

        def drops(idx):
            _, t, _, _ = items[idx]
            z = zs[idx]
            drop = jnp.maximum(jnp.log2(1.0 + jnp.exp2(jnp.minimum(z, SB_SOFTPLUS_CLAMP))), z)
            if diag_first and t == 0:
                drop = jnp.where(strict, drop, 0.0)
            sums = _dot(later, drop)
            afters[idx] = sums[:tk]
            col_sums[idx] = sums[tk:tk + 1]
            log_betas[idx] = z - drop

        def weights(idx):
            sub, t, j, h = items[idx]
            w = jnp.exp2(log_betas[idx] - (afters[idx] + carries[sub, h]))
            if diag_first and t == 0:
                w = jnp.where(strict, w, 0.0)
            pv = _dot_tn(pair_block(v_ref, j, h), w.astype(BF16))
            lo_row = (h % HEAD_PAIR) * SB_HEAD_DIM
            pv = pv[lo_row:lo_row + SB_HEAD_DIM, :]
            outs[sub, h] = pv if outs[sub, h] is None else outs[sub, h] + pv
            carries[sub, h] = carries[sub, h] + col_sums[idx]

        for step in range(n + SB_STAGE_LAG[1]):
            if step < n:
                scores(step)
            if 0 <= step - SB_STAGE_LAG[0] < n:
                drops(step - SB_STAGE_LAG[0])
            if 0 <= step - SB_STAGE_LAG[1] < n:
                weights(step - SB_STAGE_LAG[1])
        alive = []
        for sub, _ in plan:
            dead = None
            for h in range(n_heads):
                acc_ref[sub, h] += outs[sub, h]
                carry_ref[sub, h] = carries[sub, h]
                dead = carries[sub, h] if dead is None else jnp.minimum(dead, carries[sub, h])
            alive.append(jnp.min(dead) < SB_DEAD_LOG2_DROP)
        return alive

    first = SB_SUBS * i

    def usual_step():
        return visit([(sub, [first + sub - t for t in range(SB_FIRST_VISIT)])
                      for sub in range(SB_SUBS)], True)

    def first_step():
        return visit([(sub, [sub - t for t in range(SB_FIRST_VISIT) if sub - t >= 0])
                      for sub in range(SB_SUBS)], True)

    assert SB_SUBS >= SB_FIRST_VISIT - 1
    alive = lax.cond(i > 0, usual_step, first_step)

    for sub in range(SB_SUBS):
        def cond(state):
            j, still_alive = state
            return jnp.logical_and(j >= 0, still_alive)

        def body(state, sub=sub):
            j, _ = state
            return j - 1, visit([(sub, [j])], False)[0]

        lax.while_loop(cond, body, (first + sub - SB_FIRST_VISIT, alive[sub]))

    for sub in range(SB_SUBS):
        rows = slice(sub * tq, (sub + 1) * tq)
        o_t = acc_ref[sub].reshape(n_heads * SB_HEAD_DIM, tq)
        o_ref[0, rows, :] = (o_t.T * _silu(g_ref[0, rows, :])).astype(o_ref.dtype)


def _sb_attention(q, k, v, g):
    B, S, W = q.shape
    rows = SB_SUBS * SB_TQ
    assert S % rows == 0 and SB_TQ == SB_TK
    wb = SB_PAIRS * LANES
    n_heads = SB_PAIRS * HEAD_PAIR
    grid = (B, W // wb, S // rows)
    q_spec = pl.BlockSpec((1, rows, wb), lambda b, p, i: (b, i, p))
    kv_spec = pl.BlockSpec((1, S, wb), lambda b, p, i: (b, 0, p))
    return pl.pallas_call(
        _sb_kernel,
        grid=grid,
        in_specs=[q_spec, kv_spec, kv_spec, q_spec],
        out_specs=q_spec,
        out_shape=jax.ShapeDtypeStruct((B, S, W), BF16),
        scratch_shapes=[pltpu.VMEM((SB_SUBS, n_heads, SB_HEAD_DIM, SB_TQ), F32),
                        pltpu.VMEM((SB_SUBS, n_heads, 1, SB_TQ), F32)],
        compiler_params=pltpu.CompilerParams(
            dimension_semantics=("arbitrary", "arbitrary", "arbitrary"),
            vmem_limit_bytes=VMEM_LIMIT),
        name="sb_attn",
    )(q, k, v, g)


def _mla_kernel(q_ref, qnext_ref, k_ref, v_ref, g_ref, o_ref, acc_ref, m_ref, s_ref, smax_ref):
    tq, tk = MLA_TQ, MLA_TK
    diag_tiles = tq // tk
    i = pl.program_id(2)
    neg = jnp.finfo(F32).min

    acc_ref[...] = jnp.zeros_like(acc_ref)
    m_ref[...] = jnp.full_like(m_ref, neg)

    def score(j, slot, diag=None, queries=q_ref):
        start = pl.multiple_of(j * tk, tk)
        lo = 0 if diag is None else diag * tk
        for hh in range(MLA_GROUP):
            q_t = queries[0, hh * LANES:(hh + 1) * LANES, lo:]
            k_blk = k_ref[0, pl.ds(start, tk), hh * LANES:(hh + 1) * LANES]
            s = _dot(k_blk, q_t)
            s_ref[slot, hh, :, lo:] = s
            if diag is None:
                smax_ref[slot, hh] = jnp.broadcast_to(jnp.max(s, axis=0, keepdims=True), (SUBLANES, tq))

    def consume(j, slot, diag=None):
        start = pl.multiple_of(j * tk, tk)
        lo = 0 if diag is None else diag * tk
        for hh in range(MLA_GROUP):
            v_t = v_ref[0, hh * MLA_V_ROWS:(hh + 1) * MLA_V_ROWS, pl.ds(start, tk)]
            s = s_ref[slot, hh, :, lo:]
            if diag is None:
                tile_max = smax_ref[slot, hh]
            else:
                causal = (lax.broadcasted_iota(jnp.int32, s.shape, 0)
                          <= lax.broadcasted_iota(jnp.int32, s.shape, 1))
                s = jnp.where(causal, s, neg)
                tile_max = jnp.max(s, axis=0, keepdims=True)
            m_old = m_ref[hh, :, lo:]
            m_new = jnp.maximum(m_old, tile_max)
            alpha = jnp.exp2(m_old[:1] - m_new[:1])
            p = jnp.exp2(s - m_new[:1])
            acc_ref[hh, :, lo:] = alpha * acc_ref[hh, :, lo:] + _dot(v_t, p)
            m_ref[hh, :, lo:] = m_new

    base = i * diag_tiles

    def two_tiles(t):
        score(t + 1, 1)
        consume(t, 0)
        score(t + 2, 0)
        consume(t + 1, 1)

    looped = jnp.maximum(base - 2, 0)
    done = 0
    pairs = MLA_UNROLL
    while pairs >= 1:
        def group(jj, carry, pairs=pairs, done=done):
            for r in range(pairs):
                two_tiles(done + 2 * pairs * jj + 2 * r)
            return carry

        n_groups = (looped - done) // (2 * pairs)
        lax.fori_loop(0, n_groups, group, 0)
        done = done + n_groups * 2 * pairs
        pairs //= 2

    def diagonal_block():
        for d in range(diag_tiles):
            if d + 1 < diag_tiles:
                score(base + d + 1, (d + 1) % 2, d + 1)
            consume(base + d, d % 2, d)
            if d == 0:
                score(0, 0, queries=qnext_ref)
        o_t = jnp.concatenate(
            [acc_ref[hh, :MLA_V_DIM] / acc_ref[hh, MLA_V_DIM:MLA_V_DIM + 1] for hh in range(MLA_GROUP)],
            axis=0)
        o_ref[0] = (o_t.T * _silu(g_ref[0])).astype(o_ref.dtype)

    @pl.when(i > 0)
    def _():
        two_tiles(base - 2)
        diagonal_block()

    @pl.when(i == 0)
    def _():
        score(0, 0)
        diagonal_block()


def _mla_attention(q, k, v, g):
    B, S, W = g.shape
    tq = MLA_TQ
    group_rows = MLA_GROUP * MLA_V_ROWS
    group_out = MLA_GROUP * MLA_V_DIM
    assert S % tq == 0 and MLA_TQ % (2 * MLA_TK) == 0 and group_out % LANES == 0
    grid = (B, W // group_out, S // tq)
    return pl.pallas_call(
        _mla_kernel,
        grid=grid,
        in_specs=[pl.BlockSpec((1, MLA_GROUP * LANES, tq), lambda b, p, i: (b, p, i)),
                  pl.BlockSpec((1, MLA_GROUP * LANES, tq),
                               lambda b, p, i: (b, p, jnp.minimum(i + 1, S // tq - 1))),
                  pl.BlockSpec((1, S, MLA_GROUP * LANES), lambda b, p, i: (b, 0, p)),
                  pl.BlockSpec((1, group_rows, S), lambda b, p, i: (b, p, 0)),
                  pl.BlockSpec((1, tq, group_out), lambda b, p, i: (b, i, p))],
        out_specs=pl.BlockSpec((1, tq, group_out), lambda b, p, i: (b, i, p)),
        out_shape=jax.ShapeDtypeStruct((B, S, W), BF16),
        scratch_shapes=[pltpu.VMEM((MLA_GROUP, MLA_V_ROWS, tq), F32),
                        pltpu.VMEM((MLA_GROUP, SUBLANES, tq), F32),
                        pltpu.VMEM((2, MLA_GROUP, MLA_TK, tq), F32),
                        pltpu.VMEM((2, MLA_GROUP, SUBLANES, tq), F32)],
        compiler_params=pltpu.CompilerParams(
            dimension_semantics=("arbitrary", "arbitrary", "arbitrary"),
            vmem_limit_bytes=VMEM_LIMIT),
        name="mla_attn",
    )(q, q, k, v, g)


def _outproj_kernel(msb_ref, mmla_ref, x_ref, mod_ref, w_ref, o_ref):
    y = _dot(msb_ref[0], w_ref[:SB_WIDTH, :]) + _dot(mmla_ref[0], w_ref[SB_WIDTH:, :])
    o_ref[0] = x_ref[0] + mod_ref[0, 2:3, :] * y


def _outproj(mixed_sb, mixed_mla, x, mod, w_out):
    B, S, D = x.shape
    ts = min(OUT_ROW_TILE, S)

    def rows(width):
        return pl.BlockSpec((1, ts, width), lambda b, i: (b, i, 0))

    return pl.pallas_call(
        _outproj_kernel,
        grid=(B, S // ts),
        in_specs=[rows(SB_WIDTH), rows(MLA_WIDTH), rows(D),
                  pl.BlockSpec((1, 3, D), lambda b, i: (b, 0, 0)),
                  pl.BlockSpec(w_out.shape, lambda b, i: (0, 0))],
        out_specs=rows(D),
        out_shape=jax.ShapeDtypeStruct((B, S, D), x.dtype),
        compiler_params=pltpu.CompilerParams(
            dimension_semantics=("arbitrary", "arbitrary"), vmem_limit_bytes=VMEM_LIMIT),
        name="outproj",
    )(mixed_sb, mixed_mla, x, mod, w_out)


def _layer(x, c, tables, w_ada, b_ada, norm_w, w_in_packed, q_lora_norm, w_uq, kv_lora_norm, w_ukv,
           q_head_norm, k_head_norm, w_out):
    B, S, D = x.shape
    mod = _adaln(c.T, w_ada, b_ada[None, :]).reshape(B, 3, D)

    ts = min(ROW_TILE, S)
    w_in_b, w_gm = w_in_packed
    w_uq_t = w_uq.T.astype(BF16)
    w_ukv_h = w_ukv.reshape(KV_LORA_RANK, MLA_HEADS, MLA_NOPE_DIM + MLA_V_DIM)
    w_uk_t = w_ukv_h[:, :, :MLA_NOPE_DIM].reshape(KV_LORA_RANK, -1).T.astype(BF16)
    w_uv = w_ukv_h[:, :, MLA_NOPE_DIM:].reshape(KV_LORA_RANK, MLA_WIDTH).T.astype(BF16)
    gq = jnp.broadcast_to(q_head_norm[:, None], (MLA_QK_DIM, ts))
    gk = jnp.broadcast_to(k_head_norm[:, None], (MLA_QK_DIM, ts))

    (q_sb, k_sb, v_sb, g_sb, q_m, k_m, v_m, g_m) = _inproj(
        x, mod, norm_w[None, :], w_in_b, w_gm, q_lora_norm[None, :], w_uq_t, kv_lora_norm[None, :],
        w_uk_t, w_uv, gq, gk, *tables)

    mixed_sb = _sb_attention(q_sb, k_sb, v_sb, g_sb)
    mixed_mla = _mla_attention(q_m, k_m, v_m, g_m)
    return _outproj(mixed_sb, mixed_mla, x, mod, w_out.astype(BF16))


def _rope_tables(positions, dtype):
    inv_freq = ROPE_THETA ** (-jnp.arange(0, MLA_ROPE_DIM, 2, dtype=F32) / MLA_ROPE_DIM)
    ang = inv_freq[None, :, None] * positions.astype(F32)[:, None, :]
    return jnp.cos(ang).astype(dtype), jnp.sin(ang).astype(dtype)


def kernel(x, c, positions, w_ada, b_ada, norm_w, w_in, q_lora_norm, w_uq, kv_lora_norm, w_ukv,
           q_head_norm, k_head_norm, w_out):
    tables = _rope_tables(positions, x.dtype)
    for l in range(w_ada.shape[0]):
        x = _layer(x, c, tables, w_ada[l], b_ada[l], norm_w[l], _pack_w_in(w_in, l), q_lora_norm[l], w_uq[l],
                   kv_lora_norm[l], w_ukv[l], q_head_norm[l], k_head_norm[l], w_out[l])
    return x
```
